```python
import math
import jax, jax.numpy as jnp
from jax import lax
import numpy as np

D_MODEL = 2048
BATCH = 4
SEQ = 4096
DEPTH = 1

S5_WIDTH = D_MODEL // 2
S5_GROUP = 16
S5_GROUPS = S5_WIDTH // S5_GROUP
S5_STATE = 64
DT_MIN = 1e-3
DT_MAX = 1e-1
S5_MAX_RE = -1e-4
HGRN_WIDTH = D_MODEL // 2
HGRN_EXPAND = 128
HGRN_HEADS = HGRN_WIDTH // HGRN_EXPAND
HGRN_HEAD_DIM = HGRN_EXPAND
HGRN_CHUNK = 64
D_FF = 5632
CONV_WIDTH = 3
RMS_EPS = 1e-6
N_IN = S5_WIDTH + 4 * HGRN_WIDTH + 2 * D_MODEL

kernel_name = 'hybrid_s5_hgrn2_gated_merge_block'


def _rmsnorm(x, g):
    xf = x.astype(jnp.float32)
    y = xf * lax.rsqrt(jnp.mean(xf * xf, axis=-1, keepdims=True) + RMS_EPS)
    return (y * g.astype(jnp.float32)).astype(x.dtype)


def _in_splits():
    sizes = [S5_WIDTH, HGRN_WIDTH, HGRN_WIDTH, HGRN_WIDTH, HGRN_WIDTH, D_MODEL, D_MODEL]
    offs, acc = [], 0
    for s in sizes[:-1]:
        acc += s
        offs.append(acc)
    return offs


def _complex_affine_combine(first, second):
    a1r, a1i, b1r, b1i = first
    a2r, a2i, b2r, b2i = second
    return (a2r * a1r - a2i * a1i,
            a2r * a1i + a2i * a1r,
            a2r * b1r - a2i * b1i + b2r,
            a2r * b1i + a2i * b1r + b2i)


def _s5_branch(u, a_re, a_im, log_dt, b_re, b_im, c_re, c_im, d, w_glu, b_glu):
    bsz, L, _ = u.shape
    f32 = jnp.float32
    uf = u.astype(f32).reshape(bsz, L, S5_GROUPS, S5_GROUP)
    lam_re = jnp.minimum(a_re.astype(f32), S5_MAX_RE)
    lam_im = a_im.astype(f32)
    dt = jnp.exp(log_dt.astype(f32))[:, None]
    mag = jnp.exp(lam_re * dt)
    abar_re = mag * jnp.cos(lam_im * dt)
    abar_im = mag * jnp.sin(lam_im * dt)
    den = lam_re * lam_re + lam_im * lam_im
    nr = abar_re - 1.0
    ni = abar_im
    coef_re = (nr * lam_re + ni * lam_im) / den
    coef_im = (ni * lam_re - nr * lam_im) / den
    bu_re = jnp.einsum('blgc,gpc->blgp', uf, b_re.astype(f32))
    bu_im = jnp.einsum('blgc,gpc->blgp', uf, b_im.astype(f32))
    bb_re = coef_re * bu_re - coef_im * bu_im
    bb_im = coef_re * bu_im + coef_im * bu_re
    a_seq_re = jnp.broadcast_to(abar_re, (1, L, S5_GROUPS, S5_STATE))
    a_seq_im = jnp.broadcast_to(abar_im, (1, L, S5_GROUPS, S5_STATE))
    _, _, s_re, s_im = lax.associative_scan(
        _complex_affine_combine, (a_seq_re, a_seq_im, bb_re, bb_im), axis=1)
    y = (jnp.einsum('blgp,gcp->blgc', s_re, c_re.astype(f32))
         - jnp.einsum('blgp,gcp->blgc', s_im, c_im.astype(f32))
         + d.astype(f32) * uf)
    y = y.reshape(bsz, L, S5_WIDTH)
    z = jax.nn.gelu(y)
    z = z * jax.nn.sigmoid(z @ w_glu.astype(f32) + b_glu.astype(f32))
    return z.astype(u.dtype)


def _hgrn2_branch(q_in, f_in, i_in, g_in, lb, norm_g):
    bsz, L, _ = q_in.shape
    f32 = jnp.float32
    n_chunks = L // HGRN_CHUNK

    def heads(t):
        return t.reshape(bsz, n_chunks, HGRN_CHUNK, HGRN_HEADS, HGRN_HEAD_DIM).transpose(1, 0, 3, 2, 4)

    q = jax.nn.silu(q_in.astype(f32))
    zf = f_in.astype(f32)
    lbf = lb.astype(f32)
    log_f = jnp.logaddexp(jnp.log(lbf), jnp.log1p(-lbf) + jax.nn.log_sigmoid(zf))
    k = (1.0 - lbf) * jax.nn.sigmoid(-zf)
    v = i_in.astype(f32)
    causal = jnp.tril(jnp.ones((HGRN_CHUNK, HGRN_CHUNK), dtype=bool))

    def step(S, blk):
        qc, lfc, kc, vc = blk
        b = jnp.cumsum(lfc, axis=2)
        b_last = b[:, :, -1:, :]
        inter = jnp.einsum('bhtk,bhkv->bhtv', qc * jnp.exp(b), S)
        diff = b[:, :, :, None, :] - b[:, :, None, :, :]
        decay = jnp.exp(jnp.where(causal[None, None, :, :, None], diff, -jnp.inf))
        scores = jnp.einsum('bhtk,bhsk,bhtsk->bhts', qc, kc, decay)
        intra = jnp.einsum('bhts,bhsv->bhtv', scores, vc)
        S_new = (jnp.exp(b_last[:, :, 0, :])[..., None] * S
                 + jnp.einsum('bhsk,bhsv->bhkv', kc * jnp.exp(b_last - b), vc))
        return S_new, inter + intra

    S0 = jnp.zeros((bsz, HGRN_HEADS, HGRN_HEAD_DIM, HGRN_HEAD_DIM), f32)
    _, o = lax.scan(step, S0, (heads(q), heads(log_f), heads(k), heads(v)))
    o = o.transpose(1, 0, 3, 2, 4).reshape(bsz, L, HGRN_HEADS, HGRN_HEAD_DIM)
    o = o * lax.rsqrt(jnp.mean(o * o, axis=-1, keepdims=True) + RMS_EPS)
    o = o * norm_g.astype(f32).reshape(HGRN_HEADS, HGRN_HEAD_DIM)
    o = o.reshape(bsz, L, HGRN_WIDTH) * jax.nn.silu(g_in.astype(f32))
    return o.astype(q_in.dtype)


def _conv_glu_ffn(h, w_up, conv_w, conv_b, w_down):
    L = h.shape[1]
    up = h @ w_up
    up_pad = jnp.pad(up, ((0, 0), (CONV_WIDTH - 1, 0), (0, 0)))
    conv = conv_b
    for j in range(CONV_WIDTH):
        conv = conv + conv_w[j] * up_pad[:, j:j + L, :]
    gate, val = jnp.split(conv, 2, axis=-1)
    return (jax.nn.silu(gate) * val) @ w_down


def setup_inputs(seed: int = 0) -> dict:
    key = jax.random.key(seed)
    ks = jax.random.split(key, 24)
    f32 = jnp.float32

    def nrm(k, shape, scale):
        return jax.random.normal(k, shape, f32) * scale

    G, P = S5_GROUPS, S5_STATE
    return {
        'x': nrm(ks[0], (BATCH, SEQ, D_MODEL), 1.0),
        'ln_mix_g': 1.0 + nrm(ks[1], (DEPTH, D_MODEL), 0.01),
        'w_in': nrm(ks[2], (DEPTH, D_MODEL, N_IN), D_MODEL ** -0.5),
        's5_a_re': -0.5 + nrm(ks[3], (DEPTH, G, P), 0.01),
        's5_a_im': math.pi * jnp.arange(P, dtype=f32) + nrm(ks[4], (DEPTH, G, P), 0.01),
        's5_log_dt': jax.random.uniform(ks[5], (DEPTH, G), f32, math.log(DT_MIN), math.log(DT_MAX)),
        's5_b_re': nrm(ks[6], (DEPTH, G, P, S5_GROUP), (2 * S5_GROUP) ** -0.5),
        's5_b_im': nrm(ks[7], (DEPTH, G, P, S5_GROUP), (2 * S5_GROUP) ** -0.5),
        's5_c_re': nrm(ks[8], (DEPTH, G, S5_GROUP, P), S5_STATE ** -0.5),
        's5_c_im': nrm(ks[9], (DEPTH, G, S5_GROUP, P), S5_STATE ** -0.5),
        's5_d': nrm(ks[10], (DEPTH, G, S5_GROUP), 1.0),
        's5_w_glu': nrm(ks[11], (DEPTH, S5_WIDTH, S5_WIDTH), S5_WIDTH ** -0.5),
        's5_b_glu': nrm(ks[12], (DEPTH, S5_WIDTH), 0.01),
        'w_proj_s5': nrm(ks[13], (DEPTH, S5_WIDTH, D_MODEL), S5_WIDTH ** -0.5),
        'hgrn_lb_logits': nrm(ks[14], (DEPTH + 1, HGRN_WIDTH), 0.1),
        'hgrn_norm_g': 1.0 + nrm(ks[15], (DEPTH, HGRN_WIDTH), 0.01),
        'w_proj_hgrn': nrm(ks[16], (DEPTH, HGRN_WIDTH, D_MODEL), HGRN_WIDTH ** -0.5),
        'w_out': nrm(ks[17], (DEPTH, D_MODEL, D_MODEL), D_MODEL ** -0.5),
        'ln_ffn_g': 1.0 + nrm(ks[18], (DEPTH, D_MODEL), 0.01),
        'w_up': nrm(ks[19], (DEPTH, D_MODEL, 2 * D_FF), D_MODEL ** -0.5),
        'conv_w': nrm(ks[20], (DEPTH, CONV_WIDTH, 2 * D_FF), CONV_WIDTH ** -0.5),
        'conv_b': nrm(ks[21], (DEPTH, 2 * D_FF), 0.01),
        'w_down': nrm(ks[22], (DEPTH, D_FF, D_MODEL), D_FF ** -0.5),
        'ln_final_g': 1.0 + nrm(ks[23], (D_MODEL,), 0.01),
    }


def reference(x, ln_mix_g, w_in, s5_a_re, s5_a_im, s5_log_dt, s5_b_re, s5_b_im, s5_c_re, s5_c_im,
              s5_d, s5_w_glu, s5_b_glu, w_proj_s5, hgrn_lb_logits, hgrn_norm_g, w_proj_hgrn,
              w_out, ln_ffn_g, w_up, conv_w, conv_b, w_down, ln_final_g):
    lb_all = jnp.cumsum(jax.nn.softmax(hgrn_lb_logits.astype(jnp.float32), axis=0), axis=0)
    for l in range(DEPTH):
        h = _rmsnorm(x, ln_mix_g[l])
        proj = h @ w_in[l]
        u_s5, q_h, f_h, i_h, g_h, gate_s5, gate_hgrn = jnp.split(proj, _in_splits(), axis=-1)
        y_s5 = _s5_branch(u_s5, s5_a_re[l], s5_a_im[l], s5_log_dt[l], s5_b_re[l], s5_b_im[l],
                          s5_c_re[l], s5_c_im[l], s5_d[l], s5_w_glu[l], s5_b_glu[l]) @ w_proj_s5[l]
        y_hgrn = _hgrn2_branch(q_h, f_h, i_h, g_h, lb_all[l], hgrn_norm_g[l]) @ w_proj_hgrn[l]
        merged = jax.nn.sigmoid(gate_s5) * y_s5 + jax.nn.sigmoid(gate_hgrn) * y_hgrn
        x = x + merged @ w_out[l]
        x = x + _conv_glu_ffn(_rmsnorm(x, ln_ffn_g[l]), w_up[l], conv_w[l], conv_b[l], w_down[l])
    return _rmsnorm(x, ln_final_g)
```

```python
import functools
import math

import jax
import jax.numpy as jnp
from jax import lax
from jax.experimental import pallas as pl
from jax.experimental.pallas import tpu as pltpu

F32 = jnp.float32
BF16 = jnp.bfloat16

RMS_EPS = 1e-6
S5_MAX_RE = -1e-4
S5_GROUP = 16
S5_STATE = 64
HEAD_DIM = 128
CONV_WIDTH = 3

LANES_V7X = 128
MXU_DIM_V7X = 256
CHUNK = 16
S5_BLOCK_GROUPS = MXU_DIM_V7X // S5_GROUP
S5_BLOCK_STATE = S5_BLOCK_GROUPS * S5_STATE
VMEM_LIMIT_BYTES = 56 * 1024 * 1024

TOKEN_TILE = 512
S5_ROW_TILE = 128
HGRN_TOKEN_TILE = 256
FFN_TOKEN_TILE = 512
FFN_FF_TILE = 512


def _sigmoid(y):
    return 1.0 / (1.0 + jnp.exp(-y))


def _rmsnorm(x, g):
    return x * lax.rsqrt(jnp.mean(x * x, axis=-1, keepdims=True) + RMS_EPS) * g


def _params(*sem):
    return pltpu.CompilerParams(dimension_semantics=sem, vmem_limit_bytes=VMEM_LIMIT_BYTES)


def _resident(shape):
    return pl.BlockSpec(shape, lambda *_: (0,) * len(shape), pipeline_mode=pl.Buffered(1))


def _inproj_a_kernel(x_ref, g_ref, w_ref, lbl_ref, u_ref, q_ref, k_ref, v_ref, og_ref, lf_ref, *, width):
    h = _rmsnorm(x_ref[...], g_ref[...]).astype(BF16)

    def proj(i):
        return jnp.dot(h, w_ref[:, i * width:(i + 1) * width], preferred_element_type=F32)

    u_ref[...] = proj(0).astype(BF16)
    a = proj(1)
    q_ref[...] = (a * _sigmoid(a)).astype(BF16)
    lbl = lbl_ref[...]
    e = jnp.exp(lbl - jnp.max(lbl, axis=0, keepdims=True))
    lb = e[0:1, :] / jnp.sum(e, axis=0, keepdims=True)
    sg = _sigmoid(proj(2))
    lf_ref[...] = jnp.log(lb + (1.0 - lb) * sg)
    k_ref[...] = ((1.0 - lb) * (1.0 - sg)).astype(BF16)
    v_ref[...] = proj(3).astype(BF16)
    a = proj(4)
    og_ref[...] = (a * _sigmoid(a)).astype(BF16)


def _inproj_b_kernel(x_ref, g_ref, w_ref, gs_ref, gh_ref, *, width):
    h = _rmsnorm(x_ref[...], g_ref[...]).astype(BF16)
    for i, o_ref in enumerate((gs_ref, gh_ref)):
        o_ref[...] = _sigmoid(jnp.dot(h, w_ref[:, i * width:(i + 1) * width],
                                      preferred_element_type=F32)).astype(BF16)


def _s5_prep_kernel(bre_ref, bim_ref, cre_ref, cim_ref, are_ref, aim_ref, ldt_ref, d_ref,
                    bc_ref, cc_ref, klag_ref, pw_ref):
    ns = S5_BLOCK_STATE
    lam_re = jnp.minimum(are_ref[0], S5_MAX_RE)
    lam_im = aim_ref[0]
    dt = jnp.exp(ldt_ref[0])
    mag = jnp.exp(lam_re * dt)
    ab_re = mag * jnp.cos(lam_im * dt)
    ab_im = mag * jnp.sin(lam_im * dt)
    den = lam_re * lam_re + lam_im * lam_im
    nr = ab_re - 1.0
    ni = ab_im
    cf_re = (nr * lam_re + ni * lam_im) / den
    cf_im = (ni * lam_re - nr * lam_im) / den
    bre = bre_ref[0]
    bim = bim_ref[0]
    b_re = cf_re * bre - cf_im * bim
    b_im = cf_re * bim + cf_im * bre
    bc_ref[0] = jnp.concatenate([b_re, b_im], axis=-1).astype(BF16)
    cre = cre_ref[0].astype(BF16)
    cim = cim_ref[0].astype(BF16)
    cc_ref[0] = jnp.concatenate([cre, -cim], axis=0)

    p_re = jnp.ones_like(ab_re)
    p_im = jnp.zeros_like(ab_re)
    pw_ref[...] = jnp.zeros_like(pw_ref)
    for n in range(CHUNK + 1):
        pw_ref[0, n:n + 1, :] = jnp.concatenate([p_re, p_im], axis=-1)
        if n < CHUNK:
            t_re = (b_re * p_re - b_im * p_im).astype(BF16)
            t_im = (b_re * p_im + b_im * p_re).astype(BF16)
            tap = (jnp.dot(t_re, cre, preferred_element_type=F32)
                   - jnp.dot(t_im, cim, preferred_element_type=F32))
            if n == 0:
                r = lax.broadcasted_iota(jnp.int32, tap.shape, 0)
                c = lax.broadcasted_iota(jnp.int32, tap.shape, 1)
                tap = tap + jnp.where(r == c, d_ref[0], 0.0)
            klag_ref[0, n] = tap.astype(BF16)
        p_re, p_im = p_re * ab_re - p_im * ab_im, p_re * ab_im + p_im * ab_re


def _blockdiag(t, nb):
    gb = t.shape[0] // nb
    t = t.reshape(nb, gb, t.shape[1], t.shape[2])
    eye = jnp.eye(gb, dtype=jnp.bool_)[None, :, None, :, None]
    out = jnp.where(eye, t[:, :, :, None, :], jnp.zeros((), t.dtype))
    return out.reshape(nb, gb * t.shape[2], gb * t.shape[3])


def _s5_scan_kernel(u_ref, bc_ref, cc_ref, klag_ref, pw_ref, y_ref, z_ref, carry_ref, *,
                    nblk, tiles_per_seq):
    rt = u_ref.shape[0]
    ns = S5_BLOCK_STATE
    width = nblk * MXU_DIM_V7X

    def u_tile(s, j):
        lo = s * width + j * MXU_DIM_V7X
        return u_ref[:, lo:lo + MXU_DIM_V7X]

    def pw(j, n):
        return pw_ref[j, n:n + 1, 0:ns], pw_ref[j, n:n + 1, ns:2 * ns]

    @pl.when(pl.program_id(0) % tiles_per_seq == 0)
    def _():
        carry_ref[...] = jnp.zeros_like(carry_ref)

    for j in range(nblk):
        acc_re = jnp.zeros((rt, ns), F32)
        acc_im = jnp.zeros((rt, ns), F32)
        for s in range(CHUNK):
            bb = jnp.dot(u_tile(s, j), bc_ref[j], preferred_element_type=F32)
            p_re, p_im = pw(j, CHUNK - 1 - s)
            acc_re = acc_re + bb[:, :ns] * p_re - bb[:, ns:] * p_im
            acc_im = acc_im + bb[:, :ns] * p_im + bb[:, ns:] * p_re
        z_ref[:, 2 * ns * j:2 * ns * j + ns] = acc_re
        z_ref[:, 2 * ns * j + ns:2 * ns * (j + 1)] = acc_im

    for j in range(nblk):
        a_re, a_im = pw(j, CHUNK)
        lo_re = 2 * ns * j
        lo_im = lo_re + ns

        def body(r, carry):
            s_re, s_im = carry
            z_re = z_ref[pl.ds(r, 1), lo_re:lo_re + ns]
            z_im = z_ref[pl.ds(r, 1), lo_im:lo_im + ns]
            z_ref[pl.ds(r, 1), lo_re:lo_re + ns] = s_re
            z_ref[pl.ds(r, 1), lo_im:lo_im + ns] = s_im
            return (a_re * s_re - a_im * s_im + z_re, a_re * s_im + a_im * s_re + z_im)

        s_re, s_im = lax.fori_loop(
            0, rt, body, (carry_ref[0:1, lo_re:lo_re + ns], carry_ref[0:1, lo_im:lo_im + ns]))
        carry_ref[0:1, lo_re:lo_re + ns] = s_re
        carry_ref[0:1, lo_im:lo_im + ns] = s_im

    for t in range(CHUNK):
        for j in range(nblk):
            y = jnp.dot(u_tile(t, j), klag_ref[j, 0], preferred_element_type=F32)
            for s in range(t):
                y = y + jnp.dot(u_tile(s, j), klag_ref[j, t - s], preferred_element_type=F32)
            p_re, p_im = pw(j, t + 1)
            s_re = z_ref[:, 2 * ns * j:2 * ns * j + ns]
            s_im = z_ref[:, 2 * ns * j + ns:2 * ns * (j + 1)]
            rot = jnp.concatenate([s_re * p_re - s_im * p_im, s_re * p_im + s_im * p_re], axis=-1)
            y = y + jnp.dot(rot.astype(BF16), cc_ref[j], preferred_element_type=F32)
            lo = t * width + j * MXU_DIM_V7X
            y_ref[:, lo:lo + MXU_DIM_V7X] = y.astype(BF16)


def _s5_tail_kernel(y_ref, gs_ref, wg_ref, bg_ref, wp_ref, o_ref):
    y = y_ref[...].astype(F32)
    z = 0.5 * y * (1.0 + jnp.tanh(math.sqrt(2.0 / math.pi) * (y + 0.044715 * (y * y * y))))
    gl = jnp.dot(z.astype(BF16), wg_ref[...], preferred_element_type=F32) + bg_ref[...]
    zz = (z * _sigmoid(gl)).astype(BF16)
    ys = jnp.dot(zz, wp_ref[...], preferred_element_type=F32)
    o_ref[...] = (gs_ref[...].astype(F32) * ys).astype(BF16)


def _split3(x):
    hi = x.astype(BF16)
    r = x - hi.astype(F32)
    mid = r.astype(BF16)
    lo = (r - mid.astype(F32)).astype(BF16)
    return hi, mid, lo


def _hgrn_kernel(q_ref, k_ref, v_ref, og_ref, lf_ref, gh_ref, ms_ref, x_ref,
                 wph_ref, wout_ref, ng_ref, g2_ref, x1_ref, h2_ref,
                 st_ref, qt_ref, kh_ref, b_ref, gt_ref, o_ref, *, heads):
    tl = q_ref.shape[0]
    hd = HEAD_DIM

    @pl.when(pl.program_id(1) == 0)
    def _():
        st_ref[...] = jnp.zeros_like(st_ref)

    r = lax.broadcasted_iota(jnp.int32, (tl, tl), 0)
    c = lax.broadcasted_iota(jnp.int32, (tl, tl), 1)
    same = (r // CHUNK) == (c // CHUNK)
    tri = jnp.where(same, jnp.where(c <= r, 1.0, 0.0), 0.0).astype(BF16)
    blk = jnp.where(same, 1.0, 0.0).astype(BF16)
    b = jnp.zeros((tl, heads * hd), F32)
    bt = jnp.zeros((tl, heads * hd), F32)
    for piece in _split3(lf_ref[...]):
        b = b + jnp.dot(tri, piece, preferred_element_type=F32)
        bt = bt + jnp.dot(blk, piece, preferred_element_type=F32)
    b_ref[...] = b
    gt_ref[...] = jnp.exp(bt)
    qt_ref[...] = (q_ref[...].astype(F32) * jnp.exp(b)).astype(BF16)
    kh_ref[...] = (k_ref[...].astype(F32) * jnp.exp(bt - b)).astype(BF16)

    half = CHUNK // 2
    row = lax.broadcasted_iota(jnp.int32, (half, 1), 0)

    def chunk_step(n, _):
        r0 = pl.multiple_of(n * CHUNK, CHUNK)
        rows = pl.ds(r0, CHUNK)
        for h in range(heads):
            lanes = slice(h * hd, (h + 1) * hd)
            st = st_ref[h]
            inter = lax.dot_general(qt_ref[rows, lanes], st.astype(BF16),
                                    (((1,), (1,)), ((), ())), preferred_element_type=F32)
            upd = lax.dot_general(v_ref[rows, lanes], kh_ref[rows, lanes],
                                  (((0,), (0,)), ((), ())), preferred_element_type=F32)
            st_ref[h] = gt_ref[pl.ds(r0, 1), lanes] * st + upd
            q16 = q_ref[rows, lanes].astype(F32)
            k16 = k_ref[rows, lanes].astype(F32)
            v16 = v_ref[rows, lanes].astype(F32)
            b16 = b_ref[rows, lanes]
            q_top, q_bot = q16[:half], q16[half:]
            b_top, b_bot = b16[:half], b16[half:]
            acc_top = jnp.zeros((half, hd), F32)
            acc_bot = jnp.zeros((half, hd), F32)
            for s in range(CHUNK):
                bs = b16[s:s + 1]
                ks = k16[s:s + 1]
                vs = v16[s:s + 1]
                if s < half:
                    w = jnp.sum(q_top * ks * jnp.exp(jnp.minimum(b_top - bs, 0.0)), axis=-1, keepdims=True)
                    acc_top = acc_top + jnp.where(row >= s, w, 0.0) * vs
                    w = jnp.sum(q_bot * ks * jnp.exp(b_bot - bs), axis=-1, keepdims=True)
                    acc_bot = acc_bot + w * vs
                else:
                    w = jnp.sum(q_bot * ks * jnp.exp(jnp.minimum(b_bot - bs, 0.0)), axis=-1, keepdims=True)
                    acc_bot = acc_bot + jnp.where(row + half >= s, w, 0.0) * vs
            o_ref[rows, lanes] = inter + jnp.concatenate([acc_top, acc_bot], axis=0)
        return 0

    lax.fori_loop(0, tl // CHUNK, chunk_step, 0)

    parts = []
    for h in range(heads):
        lanes = slice(h * hd, (h + 1) * hd)
        o = o_ref[:, lanes]
        o = o * lax.rsqrt(jnp.mean(o * o, axis=-1, keepdims=True) + RMS_EPS)
        parts.append((o * ng_ref[:, lanes] * og_ref[:, lanes].astype(F32)).astype(BF16))
    on = jnp.concatenate(parts, axis=-1)
    yh = jnp.dot(on, wph_ref[...], preferred_element_type=F32)
    merged = ms_ref[...].astype(F32) + gh_ref[...].astype(F32) * yh
    x1 = x_ref[...] + jnp.dot(merged.astype(BF16), wout_ref[...], preferred_element_type=F32)
    x1_ref[...] = x1
    h2_ref[...] = _rmsnorm(x1, g2_ref[...]).astype(BF16)


def _ffn_kernel(h_ref, halo_ref, wug_ref, wuv_ref, cwg_ref, cwv_ref, cbg_ref, cbv_ref, wd_ref,
                x1_ref, gf_ref, o_ref, *, tiles_per_seq):
    i = pl.program_id(0)
    j = pl.program_id(1)
    tm = h_ref.shape[0]
    nh = halo_ref.shape[0]
    keep = jnp.where(i % tiles_per_seq == 0, 0.0, 1.0)
    h = h_ref[...]
    hh = halo_ref[...]

    def conv_up(w_ref, cw_ref, cb_ref):
        up = jnp.dot(h, w_ref[...], preferred_element_type=F32)
        prev = jnp.dot(hh, w_ref[...], preferred_element_type=F32) * keep
        ext = jnp.concatenate([prev, up], axis=0)
        out = cb_ref[...] + cw_ref[CONV_WIDTH - 1:CONV_WIDTH, :] * up
        for tap in range(CONV_WIDTH - 1):
            back = CONV_WIDTH - 1 - tap
            out = out + cw_ref[tap:tap + 1, :] * ext[nh - back:nh - back + tm]
        return out

    gate = conv_up(wug_ref, cwg_ref, cbg_ref)
    val = conv_up(wuv_ref, cwv_ref, cbv_ref)
    act = (gate * _sigmoid(gate) * val).astype(BF16)

    @pl.when(j == 0)
    def _():
        o_ref[...] = jnp.zeros_like(o_ref)

    o_ref[...] += jnp.dot(act, wd_ref[...], preferred_element_type=F32)

    @pl.when(j == pl.num_programs(1) - 1)
    def _():
        o_ref[...] = _rmsnorm(x1_ref[...] + o_ref[...], gf_ref[...])


def _tile(n, want):
    t = min(n, want)
    assert n % t == 0, (n, want)
    return t


def kernel(x, ln_mix_g, w_in, s5_a_re, s5_a_im, s5_log_dt, s5_b_re, s5_b_im, s5_c_re, s5_c_im,
           s5_d, s5_w_glu, s5_b_glu, w_proj_s5, hgrn_lb_logits, hgrn_norm_g, w_proj_hgrn,
           w_out, ln_ffn_g, w_up, conv_w, conv_b, w_down, ln_final_g):
    bsz, seq, dm = x.shape
    assert ln_mix_g.shape[0] == 1, "single-layer block"
    tok = bsz * seq
    ws5 = s5_w_glu.shape[1]
    wh = hgrn_norm_g.shape[1]
    heads = wh // HEAD_DIM
    dff = w_down.shape[1]
    groups = s5_a_re.shape[1]
    nblk = groups // S5_BLOCK_GROUPS
    assert ws5 == wh == nblk * MXU_DIM_V7X and seq % CHUNK == 0
    assert w_in.shape[2] == ws5 + 4 * wh + 2 * dm

    x2 = x.reshape(tok, dm)
    w_in_b = w_in[0].astype(BF16)
    row = lambda v: v.reshape(1, -1).astype(F32)

    tm = _tile(tok, TOKEN_TILE)
    n_a = ws5 + 4 * wh
    tok_spec = lambda w: pl.BlockSpec((tm, w), lambda i: (i, 0))
    u, q, k, v, og, lf = pl.pallas_call(
        functools.partial(_inproj_a_kernel, width=wh),
        grid=(tok // tm,),
        in_specs=[tok_spec(dm), _resident((1, dm)), _resident((dm, n_a)),
                  _resident(hgrn_lb_logits.shape)],
        out_specs=[tok_spec(wh)] * 6,
        out_shape=[jax.ShapeDtypeStruct((tok, wh), BF16)] * 5 + [jax.ShapeDtypeStruct((tok, wh), F32)],
        compiler_params=_params("parallel"),
        name="inproj_a",
    )(x2, row(ln_mix_g[0]), w_in_b[:, :n_a], hgrn_lb_logits.astype(F32))
    gs, gh = pl.pallas_call(
        functools.partial(_inproj_b_kernel, width=dm),
        grid=(tok // tm,),
        in_specs=[tok_spec(dm), _resident((1, dm)), _resident((dm, 2 * dm))],
        out_specs=[tok_spec(dm)] * 2,
        out_shape=[jax.ShapeDtypeStruct((tok, dm), BF16)] * 2,
        compiler_params=_params("parallel"),
        name="inproj_b",
    )(x2, row(ln_mix_g[0]), w_in_b[:, n_a:])

    ns = S5_BLOCK_STATE
    tile_c = S5_BLOCK_GROUPS * S5_GROUP
    per_state = lambda t: t.astype(F32).reshape(nblk, 1, ns)
    blk3 = lambda r, c: pl.BlockSpec((1, r, c), lambda j: (j, 0, 0))
    pw_rows = 24
    bc, cc, klag, pw = pl.pallas_call(
        _s5_prep_kernel,
        grid=(nblk,),
        in_specs=[blk3(tile_c, ns), blk3(tile_c, ns), blk3(ns, tile_c), blk3(ns, tile_c),
                  blk3(1, ns), blk3(1, ns), blk3(1, ns), blk3(1, tile_c)],
        out_specs=[blk3(tile_c, 2 * ns), blk3(2 * ns, tile_c),
                   pl.BlockSpec((1, CHUNK, tile_c, tile_c), lambda j: (j, 0, 0, 0)), blk3(pw_rows, 2 * ns)],
        out_shape=[jax.ShapeDtypeStruct((nblk, tile_c, 2 * ns), BF16),
                   jax.ShapeDtypeStruct((nblk, 2 * ns, tile_c), BF16),
                   jax.ShapeDtypeStruct((nblk, CHUNK, tile_c, tile_c), BF16),
                   jax.ShapeDtypeStruct((nblk, pw_rows, 2 * ns), F32)],
        compiler_params=_params("parallel"),
        name="s5_prep",
    )(_blockdiag(jnp.swapaxes(s5_b_re[0], 1, 2).astype(F32), nblk),
      _blockdiag(jnp.swapaxes(s5_b_im[0], 1, 2).astype(F32), nblk),
      _blockdiag(jnp.swapaxes(s5_c_re[0], 1, 2).astype(F32), nblk),
      _blockdiag(jnp.swapaxes(s5_c_im[0], 1, 2).astype(F32), nblk),
      per_state(s5_a_re[0]), per_state(s5_a_im[0]),
      per_state(jnp.broadcast_to(s5_log_dt[0][:, None], (groups, S5_STATE))),
      s5_d[0].astype(F32).reshape(nblk, 1, tile_c))

    rows_total = tok // CHUNK
    rows_seq = seq // CHUNK
    rt = _tile(rows_seq, S5_ROW_TILE)
    y = pl.pallas_call(
        functools.partial(_s5_scan_kernel, nblk=nblk, tiles_per_seq=rows_seq // rt),
        grid=(rows_total // rt,),
        in_specs=[pl.BlockSpec((rt, CHUNK * ws5), lambda i: (i, 0)),
                  _resident(bc.shape), _resident(cc.shape), _resident(klag.shape), _resident(pw.shape)],
        out_specs=pl.BlockSpec((rt, CHUNK * ws5), lambda i: (i, 0)),
        out_shape=jax.ShapeDtypeStruct((rows_total, CHUNK * ws5), BF16),
        scratch_shapes=[pltpu.VMEM((rt, nblk * 2 * ns), F32), pltpu.VMEM((8, nblk * 2 * ns), F32)],
        compiler_params=_params("arbitrary"),
        name="s5_scan",
    )(u.reshape(rows_total, CHUNK * ws5), bc, cc, klag, pw)
    y = y.reshape(tok, ws5)

    ms = pl.pallas_call(
        _s5_tail_kernel,
        grid=(tok // tm,),
        in_specs=[tok_spec(ws5), tok_spec(dm), _resident((ws5, ws5)), _resident((1, ws5)),
                  _resident((ws5, dm))],
        out_specs=tok_spec(dm),
        out_shape=jax.ShapeDtypeStruct((tok, dm), BF16),
        compiler_params=_params("parallel"),
        name="s5_tail",
    )(y, gs, s5_w_glu[0].astype(BF16), row(s5_b_glu[0]), w_proj_s5[0].astype(BF16))

    tl = _tile(seq, HGRN_TOKEN_TILE)
    lt = seq // tl
    seq_spec = lambda w: pl.BlockSpec((tl, w), lambda b, l: (b * lt + l, 0))
    x1, h2 = pl.pallas_call(
        functools.partial(_hgrn_kernel, heads=heads),
        grid=(bsz, lt),
        in_specs=[seq_spec(wh)] * 5 + [seq_spec(dm)] * 3
                 + [_resident((wh, dm)), _resident((dm, dm)), _resident((1, wh)), _resident((1, dm))],
        out_specs=[seq_spec(dm)] * 2,
        out_shape=[jax.ShapeDtypeStruct((tok, dm), F32), jax.ShapeDtypeStruct((tok, dm), BF16)],
        scratch_shapes=[pltpu.VMEM((heads, HEAD_DIM, HEAD_DIM), F32),
                        pltpu.VMEM((tl, wh), BF16), pltpu.VMEM((tl, wh), BF16),
                        pltpu.VMEM((tl, wh), F32), pltpu.VMEM((tl, wh), F32), pltpu.VMEM((tl, wh), F32)],
        compiler_params=_params("arbitrary", "arbitrary"),
        name="hgrn_merge",
    )(q, k, v, og, lf, gh, ms, x2, w_proj_hgrn[0].astype(BF16), w_out[0].astype(BF16),
      row(hgrn_norm_g[0]), row(ln_ffn_g[0]))

    tf = _tile(dff, FFN_FF_TILE)
    nf = dff // tf
    tmf = _tile(seq, FFN_TOKEN_TILE)
    halo = 16
    w_up_b = w_up[0].astype(BF16)
    cw = conv_w[0].astype(F32)
    cb = conv_b[0].astype(F32).reshape(1, -1)
    out = pl.pallas_call(
        functools.partial(_ffn_kernel, tiles_per_seq=seq // tmf),
        grid=(tok // tmf, nf),
        in_specs=[pl.BlockSpec((tmf, dm), lambda i, j: (i, 0)),
                  pl.BlockSpec((halo, dm), lambda i, j: (jnp.maximum(i * (tmf // halo) - 1, 0), 0)),
                  pl.BlockSpec((dm, tf), lambda i, j: (0, j)),
                  pl.BlockSpec((dm, tf), lambda i, j: (0, nf + j)),
                  pl.BlockSpec((CONV_WIDTH, tf), lambda i, j: (0, j)),
                  pl.BlockSpec((CONV_WIDTH, tf), lambda i, j: (0, nf + j)),
                  pl.BlockSpec((1, tf), lambda i, j: (0, j)),
                  pl.BlockSpec((1, tf), lambda i, j: (0, nf + j)),
                  pl.BlockSpec((tf, dm), lambda i, j: (j, 0)),
                  pl.BlockSpec((tmf, dm), lambda i, j: (i, 0)),
                  pl.BlockSpec((1, dm), lambda i, j: (0, 0))],
        out_specs=pl.BlockSpec((tmf, dm), lambda i, j: (i, 0)),
        out_shape=jax.ShapeDtypeStruct((tok, dm), F32),
        compiler_params=_params("parallel", "arbitrary"),
        name="ffn",
    )(h2, h2, w_up_b, w_up_b, cw, cw, cb, cb, w_down[0].astype(BF16), x1, row(ln_final_g))
    return out.reshape(bsz, seq, dm)
```

```python
import functools
import math

import jax
import jax.numpy as jnp
from jax import lax
from jax.experimental import pallas as pl
from jax.experimental.pallas import tpu as pltpu

F32 = jnp.float32
BF16 = jnp.bfloat16

RMS_EPS = 1e-6
S5_MAX_RE = -1e-4
S5_GROUP = 16
S5_STATE = 64
HEAD_DIM = 128
CONV_WIDTH = 3

LANES_V7X = 128
MXU_DIM_V7X = 256
CHUNK = 16
S5_BLOCK_GROUPS = MXU_DIM_V7X // S5_GROUP
S5_BLOCK_STATE = S5_BLOCK_GROUPS * S5_STATE
VMEM_LIMIT_BYTES = 56 * 1024 * 1024

TOKEN_TILE = 512
S5_ROW_TILE = 128
HGRN_TOKEN_TILE = 256
FFN_TOKEN_TILE = 512
FFN_FF_TILE = 512
FFN_ROW_CHUNKS = 2


def _sigmoid(y):
    return 1.0 / (1.0 + jnp.exp(-y))


def _rmsnorm(x, g):
    return x * lax.rsqrt(jnp.mean(x * x, axis=-1, keepdims=True) + RMS_EPS) * g


def _params(*sem):
    return pltpu.CompilerParams(dimension_semantics=sem, vmem_limit_bytes=VMEM_LIMIT_BYTES)


def _resident(shape):
    return pl.BlockSpec(shape, lambda *_: (0,) * len(shape), pipeline_mode=pl.Buffered(1))


def _inproj_a_kernel(x_ref, g_ref, w_ref, lbl_ref, u_ref, q_ref, k_ref, v_ref, og_ref, lf_ref, us_ref, *,
                     width):
    h = _rmsnorm(x_ref[...], g_ref[...]).astype(BF16)
    tm = x_ref.shape[0]

    def proj(i):
        return jnp.dot(h, w_ref[:, i * width:(i + 1) * width], preferred_element_type=F32)

    a = proj(0)
    slabs = width // LANES_V7X
    for c in range(slabs):
        us_ref[c] = a[:, c * LANES_V7X:(c + 1) * LANES_V7X]
    for s in range(CHUNK):
        for c in range(slabs):
            lo = s * width + c * LANES_V7X
            u_ref[:, lo:lo + LANES_V7X] = us_ref[c, pl.ds(s, tm // CHUNK, stride=CHUNK), :].astype(BF16)
    a = proj(1)
    q_ref[...] = (a * _sigmoid(a)).astype(BF16)
    lbl = lbl_ref[...]
    e = jnp.exp(lbl - jnp.max(lbl, axis=0, keepdims=True))
    lb = e[0:1, :] / jnp.sum(e, axis=0, keepdims=True)
    sg = _sigmoid(proj(2))
    lf_ref[...] = jnp.log(lb + (1.0 - lb) * sg)
    k_ref[...] = ((1.0 - lb) * (1.0 - sg)).astype(BF16)
    v_ref[...] = proj(3).astype(BF16)
    a = proj(4)
    og_ref[...] = (a * _sigmoid(a)).astype(BF16)


def _inproj_b_kernel(x_ref, g_ref, w_ref, gs_ref, gh_ref, *, width):
    h = _rmsnorm(x_ref[...], g_ref[...]).astype(BF16)
    for i, o_ref in enumerate((gs_ref, gh_ref)):
        o_ref[...] = _sigmoid(jnp.dot(h, w_ref[:, i * width:(i + 1) * width],
                                      preferred_element_type=F32)).astype(BF16)


def _s5_prep_kernel(bre_ref, bim_ref, cre_ref, cim_ref, are_ref, aim_ref, ldt_ref, d_ref,
                    bc_ref, cc_ref, klag_ref, pw_ref):
    ns = S5_BLOCK_STATE
    lam_re = jnp.minimum(are_ref[0], S5_MAX_RE)
    lam_im = aim_ref[0]
    dt = jnp.exp(ldt_ref[0])
    mag = jnp.exp(lam_re * dt)
    ab_re = mag * jnp.cos(lam_im * dt)
    ab_im = mag * jnp.sin(lam_im * dt)
    den = lam_re * lam_re + lam_im * lam_im
    nr = ab_re - 1.0
    ni = ab_im
    cf_re = (nr * lam_re + ni * lam_im) / den
    cf_im = (ni * lam_re - nr * lam_im) / den
    bre = bre_ref[0]
    bim = bim_ref[0]
    b_re = cf_re * bre - cf_im * bim
    b_im = cf_re * bim + cf_im * bre
    bc_ref[0] = jnp.concatenate([b_re, b_im], axis=-1).astype(BF16)
    cre = cre_ref[0].astype(BF16)
    cim = cim_ref[0].astype(BF16)
    cc_ref[0] = jnp.concatenate([cre, -cim], axis=0)

    p_re = jnp.ones_like(ab_re)
    p_im = jnp.zeros_like(ab_re)
    pw_ref[...] = jnp.zeros_like(pw_ref)
    for n in range(CHUNK + 1):
        pw_ref[0, n:n + 1, :] = jnp.concatenate([p_re, p_im], axis=-1)
        if n < CHUNK:
            t_re = (b_re * p_re - b_im * p_im).astype(BF16)
            t_im = (b_re * p_im + b_im * p_re).astype(BF16)
            tap = (jnp.dot(t_re, cre, preferred_element_type=F32)
                   - jnp.dot(t_im, cim, preferred_element_type=F32))
            if n == 0:
                r = lax.broadcasted_iota(jnp.int32, tap.shape, 0)
                c = lax.broadcasted_iota(jnp.int32, tap.shape, 1)
                tap = tap + jnp.where(r == c, d_ref[0], 0.0)
            klag_ref[0, n] = tap.astype(BF16)
        p_re, p_im = p_re * ab_re - p_im * ab_im, p_re * ab_im + p_im * ab_re


def _blockdiag(t, nb):
    gb = t.shape[0] // nb
    t = t.reshape(nb, gb, t.shape[1], t.shape[2])
    eye = jnp.eye(gb, dtype=jnp.bool_)[None, :, None, :, None]
    out = jnp.where(eye, t[:, :, :, None, :], jnp.zeros((), t.dtype))
    return out.reshape(nb, gb * t.shape[2], gb * t.shape[3])


def _s5_scan_kernel(u_ref, bc_ref, cc_ref, klag_ref, pw_ref, y_ref, z_ref, carry_ref, yn_ref, *,
                    nblk, tiles_per_seq):
    rt = u_ref.shape[0]
    ns = S5_BLOCK_STATE
    width = nblk * MXU_DIM_V7X

    def u_tile(s, j):
        lo = s * width + j * MXU_DIM_V7X
        return u_ref[:, lo:lo + MXU_DIM_V7X]

    def pw(j, n):
        return pw_ref[j, n:n + 1, 0:ns], pw_ref[j, n:n + 1, ns:2 * ns]

    @pl.when(pl.program_id(0) % tiles_per_seq == 0)
    def _():
        carry_ref[...] = jnp.zeros_like(carry_ref)

    for j in range(nblk):
        acc_re = jnp.zeros((rt, ns), F32)
        acc_im = jnp.zeros((rt, ns), F32)
        for s in range(CHUNK):
            bb = jnp.dot(u_tile(s, j), bc_ref[j], preferred_element_type=F32)
            p_re, p_im = pw(j, CHUNK - 1 - s)
            acc_re = acc_re + bb[:, :ns] * p_re - bb[:, ns:] * p_im
            acc_im = acc_im + bb[:, :ns] * p_im + bb[:, ns:] * p_re
        z_ref[:, 2 * ns * j:2 * ns * j + ns] = acc_re
        z_ref[:, 2 * ns * j + ns:2 * ns * (j + 1)] = acc_im

    for j in range(nblk):
        a_re, a_im = pw(j, CHUNK)
        lo_re = 2 * ns * j
        lo_im = lo_re + ns

        def body(r, carry):
            s_re, s_im = carry
            z_re = z_ref[pl.ds(r, 1), lo_re:lo_re + ns]
            z_im = z_ref[pl.ds(r, 1), lo_im:lo_im + ns]
            z_ref[pl.ds(r, 1), lo_re:lo_re + ns] = s_re
            z_ref[pl.ds(r, 1), lo_im:lo_im + ns] = s_im
            return (a_re * s_re - a_im * s_im + z_re, a_re * s_im + a_im * s_re + z_im)

        s_re, s_im = lax.fori_loop(
            0, rt, body, (carry_ref[0:1, lo_re:lo_re + ns], carry_ref[0:1, lo_im:lo_im + ns]))
        carry_ref[0:1, lo_re:lo_re + ns] = s_re
        carry_ref[0:1, lo_im:lo_im + ns] = s_im

    for t in range(CHUNK):
        for j in range(nblk):
            y = jnp.dot(u_tile(t, j), klag_ref[j, 0], preferred_element_type=F32)
            for s in range(t):
                y = y + jnp.dot(u_tile(s, j), klag_ref[j, t - s], preferred_element_type=F32)
            p_re, p_im = pw(j, t + 1)
            s_re = z_ref[:, 2 * ns * j:2 * ns * j + ns]
            s_im = z_ref[:, 2 * ns * j + ns:2 * ns * (j + 1)]
            rot = jnp.concatenate([s_re * p_re - s_im * p_im, s_re * p_im + s_im * p_re], axis=-1)
            y = y + jnp.dot(rot.astype(BF16), cc_ref[j], preferred_element_type=F32)
            for half in range(MXU_DIM_V7X // LANES_V7X):
                yn_ref[j * (MXU_DIM_V7X // LANES_V7X) + half, pl.ds(t, rt, stride=CHUNK), :] = (
                    y[:, half * LANES_V7X:(half + 1) * LANES_V7X])
    for c in range(width // LANES_V7X):
        y_ref[:, c * LANES_V7X:(c + 1) * LANES_V7X] = yn_ref[c].astype(BF16)


def _s5_tail_kernel(y_ref, gs_ref, wg_ref, bg_ref, wp_ref, o_ref):
    y = y_ref[...].astype(F32)
    z = 0.5 * y * (1.0 + jnp.tanh(math.sqrt(2.0 / math.pi) * (y + 0.044715 * (y * y * y))))
    gl = jnp.dot(z.astype(BF16), wg_ref[...], preferred_element_type=F32) + bg_ref[...]
    zz = (z * _sigmoid(gl)).astype(BF16)
    ys = jnp.dot(zz, wp_ref[...], preferred_element_type=F32)
    o_ref[...] = (gs_ref[...].astype(F32) * ys).astype(BF16)


def _split3(x):
    hi = x.astype(BF16)
    r = x - hi.astype(F32)
    mid = r.astype(BF16)
    lo = (r - mid.astype(F32)).astype(BF16)
    return hi, mid, lo


def _hgrn_kernel(q_ref, k_ref, v_ref, og_ref, lf_ref, gh_ref, ms_ref, x_ref,
                 wph_ref, wout_ref, ng_ref, g2_ref, x1_ref, h2_ref,
                 st_ref, qt_ref, kh_ref, b_ref, gt_ref, o_ref, *, heads):
    tl = q_ref.shape[0]
    hd = HEAD_DIM

    @pl.when(pl.program_id(1) == 0)
    def _():
        st_ref[...] = jnp.zeros_like(st_ref)

    r = lax.broadcasted_iota(jnp.int32, (tl, tl), 0)
    c = lax.broadcasted_iota(jnp.int32, (tl, tl), 1)
    same = (r // CHUNK) == (c // CHUNK)
    tri = jnp.where(same, jnp.where(c <= r, 1.0, 0.0), 0.0).astype(BF16)
    blk = jnp.where(same, 1.0, 0.0).astype(BF16)
    b = jnp.zeros((tl, heads * hd), F32)
    bt = jnp.zeros((tl, heads * hd), F32)
    for piece in _split3(lf_ref[...]):
        b = b + jnp.dot(tri, piece, preferred_element_type=F32)
        bt = bt + jnp.dot(blk, piece, preferred_element_type=F32)
    b_ref[...] = b
    gt_ref[...] = jnp.exp(bt)
    qt_ref[...] = (q_ref[...].astype(F32) * jnp.exp(b)).astype(BF16)
    kh_ref[...] = (k_ref[...].astype(F32) * jnp.exp(bt - b)).astype(BF16)

    half = CHUNK // 2
    row = lax.broadcasted_iota(jnp.int32, (half, 1), 0)

    def chunk_step(n, _):
        r0 = pl.multiple_of(n * CHUNK, CHUNK)
        rows = pl.ds(r0, CHUNK)
        for h in range(heads):
            lanes = slice(h * hd, (h + 1) * hd)
            st = st_ref[h]
            inter = lax.dot_general(qt_ref[rows, lanes], st.astype(BF16),
                                    (((1,), (1,)), ((), ())), preferred_element_type=F32)
            upd = lax.dot_general(v_ref[rows, lanes], kh_ref[rows, lanes],
                                  (((0,), (0,)), ((), ())), preferred_element_type=F32)
            st_ref[h] = gt_ref[pl.ds(r0, 1), lanes] * st + upd
            q16 = q_ref[rows, lanes].astype(F32)
            k16 = k_ref[rows, lanes].astype(F32)
            v16 = v_ref[rows, lanes].astype(F32)
            b16 = b_ref[rows, lanes]
            q_top, q_bot = q16[:half], q16[half:]
            b_top, b_bot = b16[:half], b16[half:]
            acc_top = jnp.zeros((half, hd), F32)
            acc_bot = jnp.zeros((half, hd), F32)
            for s in range(CHUNK):
                bs = b16[s:s + 1]
                ks = k16[s:s + 1]
                vs = v16[s:s + 1]
                if s < half:
                    w = jnp.sum(q_top * ks * jnp.exp(jnp.minimum(b_top - bs, 0.0)), axis=-1, keepdims=True)
                    acc_top = acc_top + jnp.where(row >= s, w, 0.0) * vs
                    w = jnp.sum(q_bot * ks * jnp.exp(b_bot - bs), axis=-1, keepdims=True)
                    acc_bot = acc_bot + w * vs
                else:
                    w = jnp.sum(q_bot * ks * jnp.exp(jnp.minimum(b_bot - bs, 0.0)), axis=-1, keepdims=True)
                    acc_bot = acc_bot + jnp.where(row + half >= s, w, 0.0) * vs
            o_ref[rows, lanes] = inter + jnp.concatenate([acc_top, acc_bot], axis=0)
        return 0

    lax.fori_loop(0, tl // CHUNK, chunk_step, 0)

    parts = []
    for h in range(heads):
        lanes = slice(h * hd, (h + 1) * hd)
        o = o_ref[:, lanes]
        o = o * lax.rsqrt(jnp.mean(o * o, axis=-1, keepdims=True) + RMS_EPS)
        parts.append((o * ng_ref[:, lanes] * og_ref[:, lanes].astype(F32)).astype(BF16))
    on = jnp.concatenate(parts, axis=-1)
    yh = jnp.dot(on, wph_ref[...], preferred_element_type=F32)
    merged = ms_ref[...].astype(F32) + gh_ref[...].astype(F32) * yh
    x1 = x_ref[...] + jnp.dot(merged.astype(BF16), wout_ref[...], preferred_element_type=F32)
    x1_ref[...] = x1
    h2_ref[...] = _rmsnorm(x1, g2_ref[...]).astype(BF16)


def _ffn_kernel(h_ref, halo_ref, wug_ref, wuv_ref, cwg_ref, cwv_ref, cbg_ref, cbv_ref, wd_ref, wdp_ref,
                x1_ref, gf_ref, o_ref, hcat_ref, act_ref, *, tiles_per_seq, row_chunks):
    i = pl.program_id(0)
    j = pl.program_id(1)
    tm = h_ref.shape[0]
    nh = halo_ref.shape[0]
    rc = tm // row_chunks
    slot = j % 2
    last_rows = slice(tm - rc, tm)

    @pl.when(j == 0)
    def _():
        keep = jnp.where(i % tiles_per_seq == 0, 0.0, 1.0).astype(BF16)
        hcat_ref[0:nh] = halo_ref[...] * keep
        hcat_ref[nh:nh + tm] = h_ref[...]
        o_ref[...] = jnp.zeros_like(o_ref)
        act_ref[1] = jnp.zeros(act_ref.shape[1:], BF16)

    def up(c, w_ref):
        return jnp.dot(hcat_ref[c * rc:c * rc + nh + rc], w_ref[...], preferred_element_type=F32)

    def conv(ext, cw_ref, cb_ref):
        out = cb_ref[...]
        for tap in range(CONV_WIDTH):
            back = CONV_WIDTH - 1 - tap
            out = out + cw_ref[tap:tap + 1, :] * ext[nh - back:nh - back + rc]
        return out

    def act_of(exts):
        gate = conv(exts[0], cwg_ref, cbg_ref)
        return (gate * _sigmoid(gate) * conv(exts[1], cwv_ref, cbv_ref)).astype(BF16)

    exts = (up(0, wug_ref), up(0, wuv_ref))
    o_ref[last_rows] += jnp.dot(act_ref[1 - slot], wdp_ref[...], preferred_element_type=F32)
    for c in range(row_chunks):
        nxt = (up(c + 1, wug_ref), up(c + 1, wuv_ref)) if c + 1 < row_chunks else None
        act = act_of(exts)
        if c + 1 < row_chunks:
            o_ref[c * rc:(c + 1) * rc] += jnp.dot(act, wd_ref[...], preferred_element_type=F32)
        else:
            act_ref[slot] = act
        exts = nxt

    @pl.when(j == pl.num_programs(1) - 1)
    def _():
        o_ref[last_rows] += jnp.dot(act_ref[slot], wd_ref[...], preferred_element_type=F32)
        o_ref[...] = _rmsnorm(x1_ref[...] + o_ref[...], gf_ref[...])


def _tile(n, want):
    t = min(n, want)
    assert n % t == 0, (n, want)
    return t


def _row(v):
    return v.reshape(1, -1).astype(F32)


def _inproj(x2, g, w_a, w_g, lb_logits, *, wh):
    tok, dm = x2.shape
    tm = _tile(tok, TOKEN_TILE)
    tok_spec = lambda w: pl.BlockSpec((tm, w), lambda i: (i, 0))
    u, q, k, v, og, lf = pl.pallas_call(
        functools.partial(_inproj_a_kernel, width=wh),
        grid=(tok // tm,),
        in_specs=[tok_spec(dm), _resident((1, dm)), _resident(w_a.shape), _resident(lb_logits.shape)],
        out_specs=[pl.BlockSpec((tm // CHUNK, CHUNK * wh), lambda i: (i, 0))] + [tok_spec(wh)] * 5,
        out_shape=[jax.ShapeDtypeStruct((tok // CHUNK, CHUNK * wh), BF16)]
                  + [jax.ShapeDtypeStruct((tok, wh), BF16)] * 4 + [jax.ShapeDtypeStruct((tok, wh), F32)],
        scratch_shapes=[pltpu.VMEM((wh // LANES_V7X, tm, LANES_V7X), F32)],
        compiler_params=_params("parallel"),
        name="inproj_a",
    )(x2, g, w_a, lb_logits)
    gs, gh = pl.pallas_call(
        functools.partial(_inproj_b_kernel, width=dm),
        grid=(tok // tm,),
        in_specs=[tok_spec(dm), _resident((1, dm)), _resident(w_g.shape)],
        out_specs=[tok_spec(dm)] * 2,
        out_shape=[jax.ShapeDtypeStruct((tok, dm), BF16)] * 2,
        compiler_params=_params("parallel"),
        name="inproj_b",
    )(x2, g, w_g)
    return u, q, k, v, og, lf, gs, gh


def _s5_prep(a_re, a_im, log_dt, b_re, b_im, c_re, c_im, d):
    groups = a_re.shape[0]
    nblk = groups // S5_BLOCK_GROUPS
    ns = S5_BLOCK_STATE
    tile_c = S5_BLOCK_GROUPS * S5_GROUP
    per_state = lambda t: t.astype(F32).reshape(nblk, 1, ns)
    blk3 = lambda r, c: pl.BlockSpec((1, r, c), lambda j: (j, 0, 0))
    pw_rows = 24
    return pl.pallas_call(
        _s5_prep_kernel,
        grid=(nblk,),
        in_specs=[blk3(tile_c, ns), blk3(tile_c, ns), blk3(ns, tile_c), blk3(ns, tile_c),
                  blk3(1, ns), blk3(1, ns), blk3(1, ns), blk3(1, tile_c)],
        out_specs=[blk3(tile_c, 2 * ns), blk3(2 * ns, tile_c),
                   pl.BlockSpec((1, CHUNK, tile_c, tile_c), lambda j: (j, 0, 0, 0)), blk3(pw_rows, 2 * ns)],
        out_shape=[jax.ShapeDtypeStruct((nblk, tile_c, 2 * ns), BF16),
                   jax.ShapeDtypeStruct((nblk, 2 * ns, tile_c), BF16),
                   jax.ShapeDtypeStruct((nblk, CHUNK, tile_c, tile_c), BF16),
                   jax.ShapeDtypeStruct((nblk, pw_rows, 2 * ns), F32)],
        compiler_params=_params("parallel"),
        name="s5_prep",
    )(_blockdiag(jnp.swapaxes(b_re, 1, 2).astype(F32), nblk),
      _blockdiag(jnp.swapaxes(b_im, 1, 2).astype(F32), nblk),
      _blockdiag(jnp.swapaxes(c_re, 1, 2).astype(F32), nblk),
      _blockdiag(jnp.swapaxes(c_im, 1, 2).astype(F32), nblk),
      per_state(a_re), per_state(a_im),
      per_state(jnp.broadcast_to(log_dt[:, None], (groups, S5_STATE))),
      d.astype(F32).reshape(nblk, 1, tile_c))


def _s5_scan(u_rows, bc, cc, klag, pw, *, seq):
    rows_total, lanes = u_rows.shape
    nblk = bc.shape[0]
    width = lanes // CHUNK
    ns = S5_BLOCK_STATE
    rows_seq = seq // CHUNK
    rt = _tile(rows_seq, S5_ROW_TILE)
    return pl.pallas_call(
        functools.partial(_s5_scan_kernel, nblk=nblk, tiles_per_seq=rows_seq // rt),
        grid=(rows_total // rt,),
        in_specs=[pl.BlockSpec((rt, lanes), lambda i: (i, 0)),
                  _resident(bc.shape), _resident(cc.shape), _resident(klag.shape), _resident(pw.shape)],
        out_specs=pl.BlockSpec((rt * CHUNK, width), lambda i: (i, 0)),
        out_shape=jax.ShapeDtypeStruct((rows_total * CHUNK, width), BF16),
        scratch_shapes=[pltpu.VMEM((rt, nblk * 2 * ns), F32), pltpu.VMEM((8, nblk * 2 * ns), F32),
                        pltpu.VMEM((width // LANES_V7X, rt * CHUNK, LANES_V7X), F32)],
        compiler_params=_params("arbitrary"),
        name="s5_scan",
    )(u_rows, bc, cc, klag, pw)


def _s5_tail(y, gs, w_glu, b_glu, w_proj):
    tok, ws5 = y.shape
    dm = gs.shape[1]
    tm = _tile(tok, TOKEN_TILE)
    tok_spec = lambda w: pl.BlockSpec((tm, w), lambda i: (i, 0))
    return pl.pallas_call(
        _s5_tail_kernel,
        grid=(tok // tm,),
        in_specs=[tok_spec(ws5), tok_spec(dm), _resident(w_glu.shape), _resident(b_glu.shape),
                  _resident(w_proj.shape)],
        out_specs=tok_spec(dm),
        out_shape=jax.ShapeDtypeStruct((tok, dm), BF16),
        compiler_params=_params("parallel"),
        name="s5_tail",
    )(y, gs, w_glu, b_glu, w_proj)


def _hgrn_merge(q, k, v, og, lf, gh, ms, x2, w_proj, w_out, norm_g, ffn_g, *, seq):
    tok, wh = q.shape
    dm = x2.shape[1]
    heads = wh // HEAD_DIM
    tl = _tile(seq, HGRN_TOKEN_TILE)
    lt = seq // tl
    seq_spec = lambda w: pl.BlockSpec((tl, w), lambda b, l: (b * lt + l, 0))
    return pl.pallas_call(
        functools.partial(_hgrn_kernel, heads=heads),
        grid=(tok // seq, lt),
        in_specs=[seq_spec(wh)] * 5 + [seq_spec(dm)] * 3
                 + [_resident(w_proj.shape), _resident(w_out.shape), _resident((1, wh)), _resident((1, dm))],
        out_specs=[seq_spec(dm)] * 2,
        out_shape=[jax.ShapeDtypeStruct((tok, dm), F32), jax.ShapeDtypeStruct((tok, dm), BF16)],
        scratch_shapes=[pltpu.VMEM((heads, HEAD_DIM, HEAD_DIM), F32),
                        pltpu.VMEM((tl, wh), BF16), pltpu.VMEM((tl, wh), BF16),
                        pltpu.VMEM((tl, wh), F32), pltpu.VMEM((tl, wh), F32), pltpu.VMEM((tl, wh), F32)],
        compiler_params=_params("arbitrary", "arbitrary"),
        name="hgrn_merge",
    )(q, k, v, og, lf, gh, ms, x2, w_proj, w_out, norm_g, ffn_g)


def _ffn(h2, x1, w_up, cw, cb, w_down, final_g, *, seq):
    tok, dm = h2.shape
    dff = w_down.shape[0]
    tf = _tile(dff, FFN_FF_TILE)
    nf = dff // tf
    tmf = _tile(seq, FFN_TOKEN_TILE)
    halo = 16
    return pl.pallas_call(
        functools.partial(_ffn_kernel, tiles_per_seq=seq // tmf, row_chunks=FFN_ROW_CHUNKS),
        grid=(tok // tmf, nf),
        in_specs=[pl.BlockSpec((tmf, dm), lambda i, j: (i, 0)),
                  pl.BlockSpec((halo, dm), lambda i, j: (jnp.maximum(i * (tmf // halo) - 1, 0), 0)),
                  pl.BlockSpec((dm, tf), lambda i, j: (0, j)),
                  pl.BlockSpec((dm, tf), lambda i, j: (0, nf + j)),
                  pl.BlockSpec((CONV_WIDTH, tf), lambda i, j: (0, j)),
                  pl.BlockSpec((CONV_WIDTH, tf), lambda i, j: (0, nf + j)),
                  pl.BlockSpec((1, tf), lambda i, j: (0, j)),
                  pl.BlockSpec((1, tf), lambda i, j: (0, nf + j)),
                  pl.BlockSpec((tf, dm), lambda i, j: (j, 0)),
                  pl.BlockSpec((tf, dm), lambda i, j: (jnp.maximum(j - 1, 0), 0)),
                  pl.BlockSpec((tmf, dm), lambda i, j: (i, 0)),
                  pl.BlockSpec((1, dm), lambda i, j: (0, 0))],
        out_specs=pl.BlockSpec((tmf, dm), lambda i, j: (i, 0)),
        out_shape=jax.ShapeDtypeStruct((tok, dm), F32),
        scratch_shapes=[pltpu.VMEM((halo + tmf, dm), BF16),
                        pltpu.VMEM((2, tmf // FFN_ROW_CHUNKS, tf), BF16)],
        compiler_params=_params("parallel", "arbitrary"),
        name="ffn",
    )(h2, h2, w_up, w_up, cw, cw, cb, cb, w_down, w_down, x1, final_g)


def kernel(x, ln_mix_g, w_in, s5_a_re, s5_a_im, s5_log_dt, s5_b_re, s5_b_im, s5_c_re, s5_c_im,
           s5_d, s5_w_glu, s5_b_glu, w_proj_s5, hgrn_lb_logits, hgrn_norm_g, w_proj_hgrn,
           w_out, ln_ffn_g, w_up, conv_w, conv_b, w_down, ln_final_g):
    bsz, seq, dm = x.shape
    assert ln_mix_g.shape[0] == 1, "single-layer block"
    tok = bsz * seq
    ws5 = s5_w_glu.shape[1]
    wh = hgrn_norm_g.shape[1]
    assert ws5 == wh and ws5 % MXU_DIM_V7X == 0 and seq % CHUNK == 0
    assert w_in.shape[2] == ws5 + 4 * wh + 2 * dm

    x2 = x.reshape(tok, dm)
    n_a = ws5 + 4 * wh
    u, q, k, v, og, lf, gs, gh = _inproj(
        x2, _row(ln_mix_g[0]), w_in[0, :, :n_a].astype(BF16), w_in[0, :, n_a:].astype(BF16),
        hgrn_lb_logits.astype(F32), wh=wh)

    bc, cc, klag, pw = _s5_prep(s5_a_re[0], s5_a_im[0], s5_log_dt[0], s5_b_re[0], s5_b_im[0],
                                s5_c_re[0], s5_c_im[0], s5_d[0])
    y = _s5_scan(u, bc, cc, klag, pw, seq=seq)
    ms = _s5_tail(y, gs, s5_w_glu[0].astype(BF16), _row(s5_b_glu[0]), w_proj_s5[0].astype(BF16))

    x1, h2 = _hgrn_merge(q, k, v, og, lf, gh, ms, x2, w_proj_hgrn[0].astype(BF16), w_out[0].astype(BF16),
                         _row(hgrn_norm_g[0]), _row(ln_ffn_g[0]), seq=seq)

    out = _ffn(h2, x1, w_up[0].astype(BF16), conv_w[0].astype(F32), conv_b[0].astype(F32).reshape(1, -1),
               w_down[0].astype(BF16), _row(ln_final_g), seq=seq)
    return out.reshape(bsz, seq, dm)
```

```python
import functools
import math

import jax
import jax.numpy as jnp
from jax import lax
from jax.experimental import pallas as pl
from jax.experimental.pallas import tpu as pltpu

F32 = jnp.float32
BF16 = jnp.bfloat16

RMS_EPS = 1e-6
S5_MAX_RE = -1e-4
S5_GROUP = 16
S5_STATE = 64
HEAD_DIM = 128
CONV_WIDTH = 3

LANES_V7X = 128
MXU_DIM_V7X = 256
CHUNK = 16
S5_BLOCK_GROUPS = MXU_DIM_V7X // S5_GROUP
S5_BLOCK_STATE = S5_BLOCK_GROUPS * S5_STATE
VMEM_LIMIT_BYTES = 56 * 1024 * 1024
HGRN_FACTOR_LIMIT = 60.0
HGRN_WIDE_CHUNK = 64

TOKEN_TILE = 512
S5_ROW_TILE = 256
HGRN_TOKEN_TILE = 256
FFN_TOKEN_TILE = 512
FFN_FF_TILE = 512


def _sigmoid(y):
    return 1.0 / (1.0 + jnp.exp(-y))


def _rmsnorm(x, g):
    return x * lax.rsqrt(jnp.mean(x * x, axis=-1, keepdims=True) + RMS_EPS) * g


def _params(*sem):
    return pltpu.CompilerParams(dimension_semantics=sem, vmem_limit_bytes=VMEM_LIMIT_BYTES)


def _resident(shape):
    return pl.BlockSpec(shape, lambda *_: (0,) * len(shape), pipeline_mode=pl.Buffered(1))


def _inproj_a_kernel(x_ref, g_ref, w_ref, lbl_ref, u_ref, q_ref, k_ref, v_ref, og_ref, lf_ref, *, width):
    h = _rmsnorm(x_ref[...], g_ref[...]).astype(BF16)

    def proj(i):
        return jnp.dot(h, w_ref[:, i * width:(i + 1) * width], preferred_element_type=F32)

    a = proj(0).astype(BF16)
    for j in range(u_ref.shape[0]):
        u_ref[j] = a[:, j * MXU_DIM_V7X:(j + 1) * MXU_DIM_V7X]
    a = proj(1)
    q_ref[...] = (a * _sigmoid(a)).astype(BF16)
    lbl = lbl_ref[...]
    e = jnp.exp(lbl - jnp.max(lbl, axis=0, keepdims=True))
    lb = e[0:1, :] / jnp.sum(e, axis=0, keepdims=True)
    sg = _sigmoid(proj(2))
    lf_ref[...] = jnp.log(lb + (1.0 - lb) * sg)
    k_ref[...] = ((1.0 - lb) * (1.0 - sg)).astype(BF16)
    v_ref[...] = proj(3).astype(BF16)
    a = proj(4)
    og_ref[...] = (a * _sigmoid(a)).astype(BF16)


def _inproj_b_kernel(x_ref, g_ref, w_ref, gs_ref, gh_ref, *, width):
    h = _rmsnorm(x_ref[...], g_ref[...]).astype(BF16)
    for i, o_ref in enumerate((gs_ref, gh_ref)):
        o_ref[...] = _sigmoid(jnp.dot(h, w_ref[:, i * width:(i + 1) * width],
                                      preferred_element_type=F32)).astype(BF16)


def _s5_prep_kernel(bre_ref, bim_ref, cre_ref, cim_ref, are_ref, aim_ref, ldt_ref, d_ref,
                    bc_ref, cc_ref, klag_ref, pw_ref):
    ns = S5_BLOCK_STATE
    lam_re = jnp.minimum(are_ref[0], S5_MAX_RE)
    lam_im = aim_ref[0]
    dt = jnp.exp(ldt_ref[0])
    mag = jnp.exp(lam_re * dt)
    ab_re = mag * jnp.cos(lam_im * dt)
    ab_im = mag * jnp.sin(lam_im * dt)
    den = lam_re * lam_re + lam_im * lam_im
    nr = ab_re - 1.0
    ni = ab_im
    cf_re = (nr * lam_re + ni * lam_im) / den
    cf_im = (ni * lam_re - nr * lam_im) / den
    bre = bre_ref[0]
    bim = bim_ref[0]
    b_re = cf_re * bre - cf_im * bim
    b_im = cf_re * bim + cf_im * bre
    bc_ref[0] = jnp.concatenate([b_re, b_im], axis=-1).astype(BF16)
    cre = cre_ref[0].astype(BF16)
    cim = cim_ref[0].astype(BF16)
    cc_ref[0] = jnp.concatenate([cre, -cim], axis=0)

    p_re = jnp.ones_like(ab_re)
    p_im = jnp.zeros_like(ab_re)
    pw_ref[...] = jnp.zeros_like(pw_ref)
    for n in range(CHUNK + 1):
        pw_ref[0, n:n + 1, :] = jnp.concatenate([p_re, p_im], axis=-1)
        if n < CHUNK:
            t_re = (b_re * p_re - b_im * p_im).astype(BF16)
            t_im = (b_re * p_im + b_im * p_re).astype(BF16)
            tap = (jnp.dot(t_re, cre, preferred_element_type=F32)
                   - jnp.dot(t_im, cim, preferred_element_type=F32))
            if n == 0:
                r = lax.broadcasted_iota(jnp.int32, tap.shape, 0)
                c = lax.broadcasted_iota(jnp.int32, tap.shape, 1)
                tap = tap + jnp.where(r == c, d_ref[0], 0.0)
            klag_ref[0, n] = tap.astype(BF16)
        p_re, p_im = p_re * ab_re - p_im * ab_im, p_re * ab_im + p_im * ab_re


def _blockdiag(t, nb):
    gb = t.shape[0] // nb
    t = t.reshape(nb, gb, t.shape[1], t.shape[2])
    eye = jnp.eye(gb, dtype=jnp.bool_)[None, :, None, :, None]
    out = jnp.where(eye, t[:, :, :, None, :], jnp.zeros((), t.dtype))
    return out.reshape(nb, gb * t.shape[2], gb * t.shape[3])


def _s5_scan_kernel(u_ref, bc_ref, cc_ref, klag_ref, pw_ref, y_ref, z_ref, carry_ref, *, tiles_per_seq):
    rt = u_ref.shape[0]
    ns = S5_BLOCK_STATE
    tw = MXU_DIM_V7X

    def u_tile(s):
        return u_ref[:, s * tw:(s + 1) * tw]

    def pw(n):
        return pw_ref[n:n + 1, 0:ns], pw_ref[n:n + 1, ns:2 * ns]

    @pl.when(pl.program_id(1) % tiles_per_seq == 0)
    def _():
        carry_ref[...] = jnp.zeros_like(carry_ref)

    acc_re = jnp.zeros((rt, ns), F32)
    acc_im = jnp.zeros((rt, ns), F32)
    for s in range(CHUNK):
        bb = jnp.dot(u_tile(s), bc_ref[...], preferred_element_type=F32)
        p_re, p_im = pw(CHUNK - 1 - s)
        acc_re = acc_re + bb[:, :ns] * p_re - bb[:, ns:] * p_im
        acc_im = acc_im + bb[:, :ns] * p_im + bb[:, ns:] * p_re
    z_ref[:, 0:ns] = acc_re
    z_ref[:, ns:2 * ns] = acc_im

    a_re, a_im = pw(CHUNK)

    def body(r, carry):
        s_re, s_im = carry
        z_re = z_ref[pl.ds(r, 1), 0:ns]
        z_im = z_ref[pl.ds(r, 1), ns:2 * ns]
        z_ref[pl.ds(r, 1), 0:ns] = s_re
        z_ref[pl.ds(r, 1), ns:2 * ns] = s_im
        return (a_re * s_re - a_im * s_im + z_re, a_re * s_im + a_im * s_re + z_im)

    s_re, s_im = lax.fori_loop(0, rt, body, (carry_ref[0:1, 0:ns], carry_ref[0:1, ns:2 * ns]))
    carry_ref[0:1, 0:ns] = s_re
    carry_ref[0:1, ns:2 * ns] = s_im

    s_re = z_ref[:, 0:ns]
    s_im = z_ref[:, ns:2 * ns]
    for t in range(CHUNK):
        y = jnp.dot(u_tile(t), klag_ref[0], preferred_element_type=F32)
        for s in range(t):
            y = y + jnp.dot(u_tile(s), klag_ref[t - s], preferred_element_type=F32)
        p_re, p_im = pw(t + 1)
        rot = jnp.concatenate([s_re * p_re - s_im * p_im, s_re * p_im + s_im * p_re], axis=-1)
        y = y + jnp.dot(rot.astype(BF16), cc_ref[...], preferred_element_type=F32)
        y_ref[:, t * tw:(t + 1) * tw] = y.astype(BF16)


def _s5_tail_kernel(y_ref, gs_ref, wg_ref, bg_ref, wp_ref, o_ref):
    y = jnp.concatenate([y_ref[j] for j in range(y_ref.shape[0])], axis=-1).astype(F32)
    z = 0.5 * y * (1.0 + jnp.tanh(math.sqrt(2.0 / math.pi) * (y + 0.044715 * (y * y * y))))
    gl = jnp.dot(z.astype(BF16), wg_ref[...], preferred_element_type=F32) + bg_ref[...]
    zz = (z * _sigmoid(gl)).astype(BF16)
    ys = jnp.dot(zz, wp_ref[...], preferred_element_type=F32)
    o_ref[...] = (gs_ref[...].astype(F32) * ys).astype(BF16)


def _split3(x):
    hi = x.astype(BF16)
    r = x - hi.astype(F32)
    mid = r.astype(BF16)
    lo = (r - mid.astype(F32)).astype(BF16)
    return hi, mid, lo


def _hgrn_kernel(q_ref, k_ref, v_ref, og_ref, lf_ref, gh_ref, ms_ref, x_ref,
                 wph_ref, wout_ref, ng_ref, g2_ref, x1_ref, h2_ref,
                 st_ref, qt_ref, kh_ref, b_ref, gt_ref, o_ref, *, heads):
    tl = q_ref.shape[0]
    hd = HEAD_DIM

    @pl.when(pl.program_id(1) == 0)
    def _():
        st_ref[...] = jnp.zeros_like(st_ref)

    r = lax.broadcasted_iota(jnp.int32, (tl, tl), 0)
    c = lax.broadcasted_iota(jnp.int32, (tl, tl), 1)
    pieces = _split3(lf_ref[...])

    def chunk_sums(chunk):
        same = (r // chunk) == (c // chunk)
        causal = jnp.logical_and(same, c <= r)
        tri = jnp.where(causal, 1.0, 0.0).astype(BF16)
        blk = jnp.where(same, 1.0, 0.0).astype(BF16)
        b = jnp.zeros((tl, heads * hd), F32)
        bt = jnp.zeros((tl, heads * hd), F32)
        for piece in pieces:
            b = b + jnp.dot(tri, piece, preferred_element_type=F32)
            bt = bt + jnp.dot(blk, piece, preferred_element_type=F32)
        return causal, b, bt

    def store_decayed(b, bt):
        b_ref[...] = b
        gt_ref[...] = jnp.exp(bt)
        qt_ref[...] = (q_ref[...].astype(F32) * jnp.exp(b)).astype(BF16)
        kh_ref[...] = (k_ref[...].astype(F32) * jnp.exp(bt - b)).astype(BF16)

    half = CHUNK // 2
    row = lax.broadcasted_iota(jnp.int32, (half, 1), 0)

    def pairs_exact(rows, lanes):
        q16 = q_ref[rows, lanes].astype(F32)
        k16 = k_ref[rows, lanes].astype(F32)
        v16 = v_ref[rows, lanes].astype(F32)
        b16 = b_ref[rows, lanes]
        q_top, q_bot = q16[:half], q16[half:]
        b_top, b_bot = b16[:half], b16[half:]
        acc_top = jnp.zeros((half, hd), F32)
        acc_bot = jnp.zeros((half, hd), F32)
        for s in range(CHUNK):
            bs = b16[s:s + 1]
            ks = k16[s:s + 1]
            vs = v16[s:s + 1]
            if s < half:
                w = jnp.sum(q_top * ks * jnp.exp(jnp.minimum(b_top - bs, 0.0)), axis=-1, keepdims=True)
                acc_top = acc_top + jnp.where(row >= s, w, 0.0) * vs
                w = jnp.sum(q_bot * ks * jnp.exp(b_bot - bs), axis=-1, keepdims=True)
                acc_bot = acc_bot + w * vs
            else:
                w = jnp.sum(q_bot * ks * jnp.exp(jnp.minimum(b_bot - bs, 0.0)), axis=-1, keepdims=True)
                acc_bot = acc_bot + jnp.where(row + half >= s, w, 0.0) * vs
        return jnp.concatenate([acc_top, acc_bot], axis=0)

    def carry_state(rows, first_row, h):
        lanes = slice(h * hd, (h + 1) * hd)
        st = st_ref[h]
        inter = lax.dot_general(qt_ref[rows, lanes], st.astype(BF16),
                                (((1,), (1,)), ((), ())), preferred_element_type=F32)
        upd = lax.dot_general(v_ref[rows, lanes], kh_ref[rows, lanes],
                              (((0,), (0,)), ((), ())), preferred_element_type=F32)
        st_ref[h] = gt_ref[first_row, lanes] * st + upd
        return inter

    def exact_step(n, _):
        r0 = pl.multiple_of(n * CHUNK, CHUNK)
        rows = pl.ds(r0, CHUNK)
        for h in range(heads):
            lanes = slice(h * hd, (h + 1) * hd)
            o_ref[rows, lanes] = carry_state(rows, pl.ds(r0, 1), h) + pairs_exact(rows, lanes)
        return 0

    causal, b, bt = chunk_sums(HGRN_WIDE_CHUNK)
    worst = jnp.max(jnp.max(-bt, axis=0, keepdims=True), axis=1, keepdims=True)[0, 0]
    factorable = worst <= HGRN_FACTOR_LIMIT

    @pl.when(factorable)
    def _():
        store_decayed(b, bt)
        kx = (k_ref[...].astype(F32) * jnp.exp(-b)).astype(BF16)
        for h in range(heads):
            lanes = slice(h * hd, (h + 1) * hd)
            sc = lax.dot_general(qt_ref[:, lanes], kx[:, lanes], (((1,), (1,)), ((), ())),
                                 preferred_element_type=F32)
            o_ref[:, lanes] = jnp.dot(jnp.where(causal, sc, 0.0).astype(BF16), v_ref[:, lanes],
                                      preferred_element_type=F32)
        for n in range(tl // HGRN_WIDE_CHUNK):
            rows = slice(n * HGRN_WIDE_CHUNK, (n + 1) * HGRN_WIDE_CHUNK)
            for h in range(heads):
                lanes = slice(h * hd, (h + 1) * hd)
                o_ref[rows, lanes] += carry_state(rows, slice(rows.start, rows.start + 1), h)

    @pl.when(jnp.logical_not(factorable))
    def _():
        store_decayed(*chunk_sums(CHUNK)[1:])
        lax.fori_loop(0, tl // CHUNK, exact_step, 0)

    parts = []
    for h in range(heads):
        lanes = slice(h * hd, (h + 1) * hd)
        o = o_ref[:, lanes]
        o = o * lax.rsqrt(jnp.mean(o * o, axis=-1, keepdims=True) + RMS_EPS)
        parts.append((o * ng_ref[:, lanes] * og_ref[:, lanes].astype(F32)).astype(BF16))
    on = jnp.concatenate(parts, axis=-1)
    yh = jnp.dot(on, wph_ref[...], preferred_element_type=F32)
    merged = ms_ref[...].astype(F32) + gh_ref[...].astype(F32) * yh
    x1 = x_ref[...] + jnp.dot(merged.astype(BF16), wout_ref[...], preferred_element_type=F32)
    x1_ref[...] = x1
    h2_ref[...] = _rmsnorm(x1, g2_ref[...]).astype(BF16)


def _ffn_kernel(h_ref, halo_ref, wug_ref, wuv_ref, cwg_ref, cwv_ref, cbg_ref, cbv_ref, wd_ref,
                x1_ref, gf_ref, o_ref, hcat_ref, *, tiles_per_seq):
    i = pl.program_id(0)
    j = pl.program_id(1)
    tm = h_ref.shape[0]
    nh = halo_ref.shape[0]

    @pl.when(j == 0)
    def _():
        keep = jnp.where(i % tiles_per_seq == 0, 0.0, 1.0).astype(BF16)
        hcat_ref[0:nh] = halo_ref[...] * keep
        hcat_ref[nh:nh + tm] = h_ref[...]
        o_ref[...] = jnp.zeros_like(o_ref)

    def conv_up(w_ref, cw_ref, cb_ref):
        ext = jnp.dot(hcat_ref[...], w_ref[...], preferred_element_type=F32)
        out = cb_ref[...]
        for tap in range(CONV_WIDTH):
            back = CONV_WIDTH - 1 - tap
            out = out + cw_ref[tap:tap + 1, :] * ext[nh - back:nh - back + tm]
        return out

    gate = conv_up(wug_ref, cwg_ref, cbg_ref)
    val = conv_up(wuv_ref, cwv_ref, cbv_ref)
    act = (gate * _sigmoid(gate) * val).astype(BF16)
    o_ref[...] += jnp.dot(act, wd_ref[...], preferred_element_type=F32)

    @pl.when(j == pl.num_programs(1) - 1)
    def _():
        o_ref[...] = _rmsnorm(x1_ref[...] + o_ref[...], gf_ref[...])


def _tile(n, want):
    t = min(n, want)
    assert n % t == 0, (n, want)
    return t


def _row(v):
    return v.reshape(1, -1).astype(F32)


def _inproj(x2, g, w_a, w_g, lb_logits, *, wh):
    tok, dm = x2.shape
    tm = _tile(tok, TOKEN_TILE)
    nblk = wh // MXU_DIM_V7X
    tok_spec = lambda w: pl.BlockSpec((tm, w), lambda i: (i, 0))
    u, q, k, v, og, lf = pl.pallas_call(
        functools.partial(_inproj_a_kernel, width=wh),
        grid=(tok // tm,),
        in_specs=[tok_spec(dm), _resident((1, dm)), _resident(w_a.shape), _resident(lb_logits.shape)],
        out_specs=[pl.BlockSpec((nblk, tm, MXU_DIM_V7X), lambda i: (0, i, 0))] + [tok_spec(wh)] * 5,
        out_shape=[jax.ShapeDtypeStruct((nblk, tok, MXU_DIM_V7X), BF16)]
                  + [jax.ShapeDtypeStruct((tok, wh), BF16)] * 4 + [jax.ShapeDtypeStruct((tok, wh), F32)],
        compiler_params=_params("parallel"),
        name="inproj_a",
    )(x2, g, w_a, lb_logits)
    gs, gh = pl.pallas_call(
        functools.partial(_inproj_b_kernel, width=dm),
        grid=(tok // tm,),
        in_specs=[tok_spec(dm), _resident((1, dm)), _resident(w_g.shape)],
        out_specs=[tok_spec(dm)] * 2,
        out_shape=[jax.ShapeDtypeStruct((tok, dm), BF16)] * 2,
        compiler_params=_params("parallel"),
        name="inproj_b",
    )(x2, g, w_g)
    return u, q, k, v, og, lf, gs, gh


def _s5_prep(a_re, a_im, log_dt, b_re, b_im, c_re, c_im, d):
    groups = a_re.shape[0]
    nblk = groups // S5_BLOCK_GROUPS
    ns = S5_BLOCK_STATE
    tile_c = S5_BLOCK_GROUPS * S5_GROUP
    per_state = lambda t: t.astype(F32).reshape(nblk, 1, ns)
    blk3 = lambda r, c: pl.BlockSpec((1, r, c), lambda j: (j, 0, 0))
    pw_rows = 24
    return pl.pallas_call(
        _s5_prep_kernel,
        grid=(nblk,),
        in_specs=[blk3(tile_c, ns), blk3(tile_c, ns), blk3(ns, tile_c), blk3(ns, tile_c),
                  blk3(1, ns), blk3(1, ns), blk3(1, ns), blk3(1, tile_c)],
        out_specs=[blk3(tile_c, 2 * ns), blk3(2 * ns, tile_c),
                   pl.BlockSpec((1, CHUNK, tile_c, tile_c), lambda j: (j, 0, 0, 0)), blk3(pw_rows, 2 * ns)],
        out_shape=[jax.ShapeDtypeStruct((nblk, tile_c, 2 * ns), BF16),
                   jax.ShapeDtypeStruct((nblk, 2 * ns, tile_c), BF16),
                   jax.ShapeDtypeStruct((nblk, CHUNK, tile_c, tile_c), BF16),
                   jax.ShapeDtypeStruct((nblk, pw_rows, 2 * ns), F32)],
        compiler_params=_params("parallel"),
        name="s5_prep",
    )(_blockdiag(jnp.swapaxes(b_re, 1, 2).astype(F32), nblk),
      _blockdiag(jnp.swapaxes(b_im, 1, 2).astype(F32), nblk),
      _blockdiag(jnp.swapaxes(c_re, 1, 2).astype(F32), nblk),
      _blockdiag(jnp.swapaxes(c_im, 1, 2).astype(F32), nblk),
      per_state(a_re), per_state(a_im),
      per_state(jnp.broadcast_to(log_dt[:, None], (groups, S5_STATE))),
      d.astype(F32).reshape(nblk, 1, tile_c))


def _s5_scan(u_rows, bc, cc, klag, pw, *, seq):
    nblk, rows_total, lanes = u_rows.shape
    ns = S5_BLOCK_STATE
    rows_seq = seq // CHUNK
    rt = _tile(rows_seq, S5_ROW_TILE)
    per_tile = lambda a: pl.BlockSpec((None,) + a.shape[1:], lambda j, i: (j,) + (0,) * (a.ndim - 1))
    rows_spec = pl.BlockSpec((None, rt, lanes), lambda j, i: (j, i, 0))
    return pl.pallas_call(
        functools.partial(_s5_scan_kernel, tiles_per_seq=rows_seq // rt),
        grid=(nblk, rows_total // rt),
        in_specs=[rows_spec, per_tile(bc), per_tile(cc), per_tile(klag), per_tile(pw)],
        out_specs=rows_spec,
        out_shape=jax.ShapeDtypeStruct(u_rows.shape, BF16),
        scratch_shapes=[pltpu.VMEM((rt, 2 * ns), F32), pltpu.VMEM((8, 2 * ns), F32)],
        compiler_params=_params("parallel", "arbitrary"),
        name="s5_scan",
    )(u_rows, bc, cc, klag, pw)


def _s5_tail(y, gs, w_glu, b_glu, w_proj):
    nblk, tok, tw = y.shape
    dm = gs.shape[1]
    tm = _tile(tok, TOKEN_TILE)
    tok_spec = lambda w: pl.BlockSpec((tm, w), lambda i: (i, 0))
    return pl.pallas_call(
        _s5_tail_kernel,
        grid=(tok // tm,),
        in_specs=[pl.BlockSpec((nblk, tm, tw), lambda i: (0, i, 0)), tok_spec(dm), _resident(w_glu.shape),
                  _resident(b_glu.shape), _resident(w_proj.shape)],
        out_specs=tok_spec(dm),
        out_shape=jax.ShapeDtypeStruct((tok, dm), BF16),
        compiler_params=_params("parallel"),
        name="s5_tail",
    )(y, gs, w_glu, b_glu, w_proj)


def _hgrn_merge(q, k, v, og, lf, gh, ms, x2, w_proj, w_out, norm_g, ffn_g, *, seq):
    tok, wh = q.shape
    dm = x2.shape[1]
    heads = wh // HEAD_DIM
    tl = _tile(seq, HGRN_TOKEN_TILE)
    lt = seq // tl
    seq_spec = lambda w: pl.BlockSpec((tl, w), lambda b, l: (b * lt + l, 0))
    return pl.pallas_call(
        functools.partial(_hgrn_kernel, heads=heads),
        grid=(tok // seq, lt),
        in_specs=[seq_spec(wh)] * 5 + [seq_spec(dm)] * 3
                 + [_resident(w_proj.shape), _resident(w_out.shape), _resident((1, wh)), _resident((1, dm))],
        out_specs=[seq_spec(dm)] * 2,
        out_shape=[jax.ShapeDtypeStruct((tok, dm), F32), jax.ShapeDtypeStruct((tok, dm), BF16)],
        scratch_shapes=[pltpu.VMEM((heads, HEAD_DIM, HEAD_DIM), F32),
                        pltpu.VMEM((tl, wh), BF16), pltpu.VMEM((tl, wh), BF16),
                        pltpu.VMEM((tl, wh), F32), pltpu.VMEM((tl, wh), F32), pltpu.VMEM((tl, wh), F32)],
        compiler_params=_params("arbitrary", "arbitrary"),
        name="hgrn_merge",
    )(q, k, v, og, lf, gh, ms, x2, w_proj, w_out, norm_g, ffn_g)


def _ffn(h2, x1, w_up, cw, cb, w_down, final_g, *, seq):
    tok, dm = h2.shape
    dff = w_down.shape[0]
    tf = _tile(dff, FFN_FF_TILE)
    nf = dff // tf
    tmf = _tile(seq, FFN_TOKEN_TILE)
    halo = 16
    return pl.pallas_call(
        functools.partial(_ffn_kernel, tiles_per_seq=seq // tmf),
        grid=(tok // tmf, nf),
        in_specs=[pl.BlockSpec((tmf, dm), lambda i, j: (i, 0)),
                  pl.BlockSpec((halo, dm), lambda i, j: (jnp.maximum(i * (tmf // halo) - 1, 0), 0)),
                  pl.BlockSpec((dm, tf), lambda i, j: (0, j)),
                  pl.BlockSpec((dm, tf), lambda i, j: (0, nf + j)),
                  pl.BlockSpec((CONV_WIDTH, tf), lambda i, j: (0, j)),
                  pl.BlockSpec((CONV_WIDTH, tf), lambda i, j: (0, nf + j)),
                  pl.BlockSpec((1, tf), lambda i, j: (0, j)),
                  pl.BlockSpec((1, tf), lambda i, j: (0, nf + j)),
                  pl.BlockSpec((tf, dm), lambda i, j: (j, 0)),
                  pl.BlockSpec((tmf, dm), lambda i, j: (i, 0)),
                  pl.BlockSpec((1, dm), lambda i, j: (0, 0))],
        out_specs=pl.BlockSpec((tmf, dm), lambda i, j: (i, 0)),
        out_shape=jax.ShapeDtypeStruct((tok, dm), F32),
        scratch_shapes=[pltpu.VMEM((halo + tmf, dm), BF16)],
        compiler_params=_params("parallel", "arbitrary"),
        name="ffn",
    )(h2, h2, w_up, w_up, cw, cw, cb, cb, w_down, x1, final_g)


def kernel(x, ln_mix_g, w_in, s5_a_re, s5_a_im, s5_log_dt, s5_b_re, s5_b_im, s5_c_re, s5_c_im,
           s5_d, s5_w_glu, s5_b_glu, w_proj_s5, hgrn_lb_logits, hgrn_norm_g, w_proj_hgrn,
           w_out, ln_ffn_g, w_up, conv_w, conv_b, w_down, ln_final_g):
    bsz, seq, dm = x.shape
    assert ln_mix_g.shape[0] == 1, "single-layer block"
    tok = bsz * seq
    ws5 = s5_w_glu.shape[1]
    wh = hgrn_norm_g.shape[1]
    assert ws5 == wh and ws5 % MXU_DIM_V7X == 0 and seq % CHUNK == 0
    assert w_in.shape[2] == ws5 + 4 * wh + 2 * dm

    x2 = x.reshape(tok, dm)
    n_a = ws5 + 4 * wh
    u, q, k, v, og, lf, gs, gh = _inproj(
        x2, _row(ln_mix_g[0]), w_in[0, :, :n_a].astype(BF16), w_in[0, :, n_a:].astype(BF16),
        hgrn_lb_logits.astype(F32), wh=wh)

    bc, cc, klag, pw = _s5_prep(s5_a_re[0], s5_a_im[0], s5_log_dt[0], s5_b_re[0], s5_b_im[0],
                                s5_c_re[0], s5_c_im[0], s5_d[0])
    nblk, _, tw = u.shape
    y = _s5_scan(u.reshape(nblk, tok // CHUNK, CHUNK * tw), bc, cc, klag, pw, seq=seq).reshape(u.shape)
    ms = _s5_tail(y, gs, s5_w_glu[0].astype(BF16), _row(s5_b_glu[0]), w_proj_s5[0].astype(BF16))

    x1, h2 = _hgrn_merge(q, k, v, og, lf, gh, ms, x2, w_proj_hgrn[0].astype(BF16), w_out[0].astype(BF16),
                         _row(hgrn_norm_g[0]), _row(ln_ffn_g[0]), seq=seq)

    out = _ffn(h2, x1, w_up[0].astype(BF16), conv_w[0].astype(F32), conv_b[0].astype(F32).reshape(1, -1),
               w_down[0].astype(BF16), _row(ln_final_g), seq=seq)
    return out.reshape(bsz, seq, dm)
```

```python
import functools
import math

import jax
import jax.numpy as jnp
from jax import lax
from jax.experimental import pallas as pl
from jax.experimental.pallas import tpu as pltpu

F32 = jnp.float32
BF16 = jnp.bfloat16

RMS_EPS = 1e-6
S5_MAX_RE = -1e-4
S5_GROUP = 16
S5_STATE = 64
HEAD_DIM = 128
CONV_WIDTH = 3

LANES_V7X = 128
MXU_DIM_V7X = 256
CHUNK = 16
S5_SUB = 4
S5_BLOCK_GROUPS = MXU_DIM_V7X // S5_GROUP
S5_BLOCK_STATE = S5_BLOCK_GROUPS * S5_STATE
VMEM_LIMIT_BYTES = 56 * 1024 * 1024
HGRN_FACTOR_LIMIT = 60.0
HGRN_WIDE_CHUNK = 64

TOKEN_TILE = 512
S5_ROW_TILE = 256
HGRN_TOKEN_TILE = 256
FFN_TOKEN_TILE = 512
FFN_FF_TILE = 512


def _sigmoid(y):
    return 1.0 / (1.0 + jnp.exp(-y))


def _rmsnorm(x, g):
    return x * lax.rsqrt(jnp.mean(x * x, axis=-1, keepdims=True) + RMS_EPS) * g


def _params(*sem):
    return pltpu.CompilerParams(dimension_semantics=sem, vmem_limit_bytes=VMEM_LIMIT_BYTES)


def _resident(shape):
    return pl.BlockSpec(shape, lambda *_: (0,) * len(shape), pipeline_mode=pl.Buffered(1))


def _inproj_a_kernel(x_ref, g_ref, w_ref, lbl_ref, u_ref, q_ref, k_ref, v_ref, og_ref, lf_ref, *, width):
    h = _rmsnorm(x_ref[...], g_ref[...]).astype(BF16)

    def proj(i):
        return jnp.dot(h, w_ref[:, i * width:(i + 1) * width], preferred_element_type=F32)

    a = proj(0).astype(BF16)
    for j in range(u_ref.shape[0]):
        u_ref[j] = a[:, j * MXU_DIM_V7X:(j + 1) * MXU_DIM_V7X]
    a = proj(1)
    q_ref[...] = (a * _sigmoid(a)).astype(BF16)
    lbl = lbl_ref[...]
    e = jnp.exp(lbl - jnp.max(lbl, axis=0, keepdims=True))
    lb = e[0:1, :] / jnp.sum(e, axis=0, keepdims=True)
    sg = _sigmoid(proj(2))
    lf_ref[...] = jnp.log(lb + (1.0 - lb) * sg)
    k_ref[...] = ((1.0 - lb) * (1.0 - sg)).astype(BF16)
    v_ref[...] = proj(3).astype(BF16)
    a = proj(4)
    og_ref[...] = (a * _sigmoid(a)).astype(BF16)


def _inproj_b_kernel(x_ref, g_ref, *refs):
    w_refs, o_refs = refs[:-2], refs[-2:]
    per_out = len(w_refs) // len(o_refs)
    h = _rmsnorm(x_ref[...], g_ref[...]).astype(BF16)
    for n, w_ref in enumerate(w_refs):
        cols = w_ref.shape[1]
        lo = (n % per_out) * cols
        o_refs[n // per_out][:, lo:lo + cols] = _sigmoid(
            jnp.dot(h, w_ref[...], preferred_element_type=F32)).astype(BF16)


def _s5_prep_kernel(bre_ref, bim_ref, cre_ref, cim_ref, are_ref, aim_ref, ldt_ref, d_ref,
                    bc_ref, cc_ref, klag_ref, pw_ref):
    ns = S5_BLOCK_STATE
    lam_re = jnp.minimum(are_ref[0], S5_MAX_RE)
    lam_im = aim_ref[0]
    dt = jnp.exp(ldt_ref[0])
    mag = jnp.exp(lam_re * dt)
    ab_re = mag * jnp.cos(lam_im * dt)
    ab_im = mag * jnp.sin(lam_im * dt)
    den = lam_re * lam_re + lam_im * lam_im
    nr = ab_re - 1.0
    ni = ab_im
    cf_re = (nr * lam_re + ni * lam_im) / den
    cf_im = (ni * lam_re - nr * lam_im) / den
    bre = bre_ref[0]
    bim = bim_ref[0]
    b_re = cf_re * bre - cf_im * bim
    b_im = cf_re * bim + cf_im * bre
    cre = cre_ref[0].astype(BF16)
    cim = cim_ref[0].astype(BF16)
    cc_ref[0] = jnp.concatenate([cre, -cim], axis=0)

    p_re = jnp.ones_like(ab_re)
    p_im = jnp.zeros_like(ab_re)
    pw_ref[...] = jnp.zeros_like(pw_ref)
    for n in range(CHUNK + 1):
        pw_ref[0, n:n + 1, :] = jnp.concatenate([p_re, p_im], axis=-1)
        if n < S5_SUB:
            bc_ref[0, S5_SUB - 1 - n] = jnp.concatenate(
                [b_re * p_re - b_im * p_im, b_re * p_im + b_im * p_re], axis=-1).astype(BF16)
        if n < CHUNK:
            t_re = (b_re * p_re - b_im * p_im).astype(BF16)
            t_im = (b_re * p_im + b_im * p_re).astype(BF16)
            tap = (jnp.dot(t_re, cre, preferred_element_type=F32)
                   - jnp.dot(t_im, cim, preferred_element_type=F32))
            if n == 0:
                r = lax.broadcasted_iota(jnp.int32, tap.shape, 0)
                c = lax.broadcasted_iota(jnp.int32, tap.shape, 1)
                tap = tap + jnp.where(r == c, d_ref[0], 0.0)
            klag_ref[0, n] = tap.astype(BF16)
        p_re, p_im = p_re * ab_re - p_im * ab_im, p_re * ab_im + p_im * ab_re


def _blockdiag(t, nb):
    gb = t.shape[0] // nb
    t = t.reshape(nb, gb, t.shape[1], t.shape[2])
    eye = jnp.eye(gb, dtype=jnp.bool_)[None, :, None, :, None]
    out = jnp.where(eye, t[:, :, :, None, :], jnp.zeros((), t.dtype))
    return out.reshape(nb, gb * t.shape[2], gb * t.shape[3])


def _s5_scan_kernel(u_ref, bc_ref, cc_ref, klag_ref, pw_ref, y_ref, z_ref, carry_ref, *, tiles_per_seq):
    rt = u_ref.shape[0]
    ns = S5_BLOCK_STATE
    tw = MXU_DIM_V7X

    def u_tile(s):
        return u_ref[:, s * tw:(s + 1) * tw]

    def pw(n):
        return pw_ref[n:n + 1, 0:ns], pw_ref[n:n + 1, ns:2 * ns]

    @pl.when(pl.program_id(1) % tiles_per_seq == 0)
    def _():
        carry_ref[...] = jnp.zeros_like(carry_ref)

    acc_re = jnp.zeros((rt, ns), F32)
    acc_im = jnp.zeros((rt, ns), F32)
    for sub in range(CHUNK // S5_SUB):
        bb = jnp.dot(u_tile(sub * S5_SUB), bc_ref[0], preferred_element_type=F32)
        for r in range(1, S5_SUB):
            bb = bb + jnp.dot(u_tile(sub * S5_SUB + r), bc_ref[r], preferred_element_type=F32)
        p_re, p_im = pw(CHUNK - S5_SUB * (sub + 1))
        acc_re = acc_re + bb[:, :ns] * p_re - bb[:, ns:] * p_im
        acc_im = acc_im + bb[:, :ns] * p_im + bb[:, ns:] * p_re
    z_ref[:, 0:ns] = acc_re
    z_ref[:, ns:2 * ns] = acc_im

    a_re, a_im = pw(CHUNK)

    def body(r, carry):
        s_re, s_im = carry
        z_re = z_ref[pl.ds(r, 1), 0:ns]
        z_im = z_ref[pl.ds(r, 1), ns:2 * ns]
        z_ref[pl.ds(r, 1), 0:ns] = s_re
        z_ref[pl.ds(r, 1), ns:2 * ns] = s_im
        return (a_re * s_re - a_im * s_im + z_re, a_re * s_im + a_im * s_re + z_im)

    s_re, s_im = lax.fori_loop(0, rt, body, (carry_ref[0:1, 0:ns], carry_ref[0:1, ns:2 * ns]))
    carry_ref[0:1, 0:ns] = s_re
    carry_ref[0:1, ns:2 * ns] = s_im

    s_re = z_ref[:, 0:ns]
    s_im = z_ref[:, ns:2 * ns]
    for t in range(CHUNK):
        y = jnp.dot(u_tile(t), klag_ref[0], preferred_element_type=F32)
        for s in range(t):
            y = y + jnp.dot(u_tile(s), klag_ref[t - s], preferred_element_type=F32)
        p_re, p_im = pw(t + 1)
        rot = jnp.concatenate([s_re * p_re - s_im * p_im, s_re * p_im + s_im * p_re], axis=-1)
        y = y + jnp.dot(rot.astype(BF16), cc_ref[...], preferred_element_type=F32)
        y_ref[:, t * tw:(t + 1) * tw] = y.astype(BF16)


def _s5_tail_kernel(y_ref, gs_ref, wg_ref, bg_ref, wp_ref, o_ref):
    y = jnp.concatenate([y_ref[j] for j in range(y_ref.shape[0])], axis=-1).astype(F32)
    z = 0.5 * y * (1.0 + jnp.tanh(math.sqrt(2.0 / math.pi) * (y + 0.044715 * (y * y * y))))
    gl = jnp.dot(z.astype(BF16), wg_ref[...], preferred_element_type=F32) + bg_ref[...]
    zz = (z * _sigmoid(gl)).astype(BF16)
    ys = jnp.dot(zz, wp_ref[...], preferred_element_type=F32)
    o_ref[...] = (gs_ref[...].astype(F32) * ys).astype(BF16)


def _split3(x):
    hi = x.astype(BF16)
    r = x - hi.astype(F32)
    mid = r.astype(BF16)
    lo = (r - mid.astype(F32)).astype(BF16)
    return hi, mid, lo


def _hgrn_kernel(q_ref, k_ref, v_ref, og_ref, lf_ref, gh_ref, ms_ref, x_ref,
                 wph_ref, wout_ref, ng_ref, g2_ref, x1_ref, h2_ref,
                 st_ref, qt_ref, kh_ref, b_ref, gt_ref, o_ref, *, heads):
    tl = q_ref.shape[0]
    hd = HEAD_DIM

    @pl.when(pl.program_id(1) == 0)
    def _():
        st_ref[...] = jnp.zeros_like(st_ref)

    r = lax.broadcasted_iota(jnp.int32, (tl, tl), 0)
    c = lax.broadcasted_iota(jnp.int32, (tl, tl), 1)
    pieces = _split3(lf_ref[...])

    def chunk_sums(chunk):
        same = (r // chunk) == (c // chunk)
        causal = jnp.logical_and(same, c <= r)
        tri = jnp.where(causal, 1.0, 0.0).astype(BF16)
        blk = jnp.where(same, 1.0, 0.0).astype(BF16)
        b = jnp.zeros((tl, heads * hd), F32)
        bt = jnp.zeros((tl, heads * hd), F32)
        for piece in pieces:
            b = b + jnp.dot(tri, piece, preferred_element_type=F32)
            bt = bt + jnp.dot(blk, piece, preferred_element_type=F32)
        return causal, b, bt

    def store_decayed(b, bt):
        b_ref[...] = b
        gt_ref[...] = jnp.exp(bt)
        qt_ref[...] = (q_ref[...].astype(F32) * jnp.exp(b)).astype(BF16)
        kh_ref[...] = (k_ref[...].astype(F32) * jnp.exp(bt - b)).astype(BF16)

    half = CHUNK // 2
    row = lax.broadcasted_iota(jnp.int32, (half, 1), 0)

    def pairs_exact(rows, lanes):
        q16 = q_ref[rows, lanes].astype(F32)
        k16 = k_ref[rows, lanes].astype(F32)
        v16 = v_ref[rows, lanes].astype(F32)
        b16 = b_ref[rows, lanes]
        q_top, q_bot = q16[:half], q16[half:]
        b_top, b_bot = b16[:half], b16[half:]
        acc_top = jnp.zeros((half, hd), F32)
        acc_bot = jnp.zeros((half, hd), F32)
        for s in range(CHUNK):
            bs = b16[s:s + 1]
            ks = k16[s:s + 1]
            vs = v16[s:s + 1]
            if s < half:
                w = jnp.sum(q_top * ks * jnp.exp(jnp.minimum(b_top - bs, 0.0)), axis=-1, keepdims=True)
                acc_top = acc_top + jnp.where(row >= s, w, 0.0) * vs
                w = jnp.sum(q_bot * ks * jnp.exp(b_bot - bs), axis=-1, keepdims=True)
                acc_bot = acc_bot + w * vs
            else:
                w = jnp.sum(q_bot * ks * jnp.exp(jnp.minimum(b_bot - bs, 0.0)), axis=-1, keepdims=True)
                acc_bot = acc_bot + jnp.where(row + half >= s, w, 0.0) * vs
        return jnp.concatenate([acc_top, acc_bot], axis=0)

    def carry_state(rows, first_row, h):
        lanes = slice(h * hd, (h + 1) * hd)
        st = st_ref[h]
        inter = lax.dot_general(qt_ref[rows, lanes], st.astype(BF16),
                                (((1,), (1,)), ((), ())), preferred_element_type=F32)
        upd = lax.dot_general(v_ref[rows, lanes], kh_ref[rows, lanes],
                              (((0,), (0,)), ((), ())), preferred_element_type=F32)
        st_ref[h] = gt_ref[first_row, lanes] * st + upd
        return inter

    def exact_step(n, _):
        r0 = pl.multiple_of(n * CHUNK, CHUNK)
        rows = pl.ds(r0, CHUNK)
        for h in range(heads):
            lanes = slice(h * hd, (h + 1) * hd)
            o_ref[rows, lanes] = carry_state(rows, pl.ds(r0, 1), h) + pairs_exact(rows, lanes)
        return 0

    causal, b, bt = chunk_sums(HGRN_WIDE_CHUNK)
    worst = jnp.max(jnp.max(-bt, axis=0, keepdims=True), axis=1, keepdims=True)[0, 0]
    factorable = worst <= HGRN_FACTOR_LIMIT

    @pl.when(factorable)
    def _():
        store_decayed(b, bt)
        kx = (k_ref[...].astype(F32) * jnp.exp(-b)).astype(BF16)
        for h in range(heads):
            lanes = slice(h * hd, (h + 1) * hd)
            sc = lax.dot_general(qt_ref[:, lanes], kx[:, lanes], (((1,), (1,)), ((), ())),
                                 preferred_element_type=F32)
            o_ref[:, lanes] = jnp.dot(jnp.where(causal, sc, 0.0).astype(BF16), v_ref[:, lanes],
                                      preferred_element_type=F32)
        for n in range(tl // HGRN_WIDE_CHUNK):
            rows = slice(n * HGRN_WIDE_CHUNK, (n + 1) * HGRN_WIDE_CHUNK)
            for h in range(heads):
                lanes = slice(h * hd, (h + 1) * hd)
                o_ref[rows, lanes] += carry_state(rows, slice(rows.start, rows.start + 1), h)

    @pl.when(jnp.logical_not(factorable))
    def _():
        store_decayed(*chunk_sums(CHUNK)[1:])
        lax.fori_loop(0, tl // CHUNK, exact_step, 0)

    parts = []
    for h in range(heads):
        lanes = slice(h * hd, (h + 1) * hd)
        o = o_ref[:, lanes]
        o = o * lax.rsqrt(jnp.mean(o * o, axis=-1, keepdims=True) + RMS_EPS)
        parts.append((o * ng_ref[:, lanes] * og_ref[:, lanes].astype(F32)).astype(BF16))
    on = jnp.concatenate(parts, axis=-1)
    yh = jnp.dot(on, wph_ref[...], preferred_element_type=F32)
    merged = ms_ref[...].astype(F32) + gh_ref[...].astype(F32) * yh
    x1 = x_ref[...] + jnp.dot(merged.astype(BF16), wout_ref[...], preferred_element_type=F32)
    x1_ref[...] = x1
    h2_ref[...] = _rmsnorm(x1, g2_ref[...]).astype(BF16)


def _ffn_kernel(h_ref, halo_ref, wug_ref, wuv_ref, cwg_ref, cwv_ref, cbg_ref, cbv_ref, wd_ref,
                x1_ref, gf_ref, o_ref, hcat_ref, *, tiles_per_seq):
    i = pl.program_id(0)
    j = pl.program_id(1)
    tm = h_ref.shape[0]
    nh = halo_ref.shape[0]

    @pl.when(j == 0)
    def _():
        keep = jnp.where(i % tiles_per_seq == 0, 0.0, 1.0).astype(BF16)
        hcat_ref[0:nh] = halo_ref[...] * keep
        hcat_ref[nh:nh + tm] = h_ref[...]
        o_ref[...] = jnp.zeros_like(o_ref)

    def conv_up(w_ref, cw_ref, cb_ref):
        ext = jnp.dot(hcat_ref[...], w_ref[...], preferred_element_type=F32)
        out = cb_ref[...]
        for tap in range(CONV_WIDTH):
            back = CONV_WIDTH - 1 - tap
            out = out + cw_ref[tap:tap + 1, :] * ext[nh - back:nh - back + tm]
        return out

    gate = conv_up(wug_ref, cwg_ref, cbg_ref)
    val = conv_up(wuv_ref, cwv_ref, cbv_ref)
    act = (0.5 * gate * (1.0 + jnp.tanh(0.5 * gate)) * val).astype(BF16)
    o_ref[...] += jnp.dot(act, wd_ref[...], preferred_element_type=F32)

    @pl.when(j == pl.num_programs(1) - 1)
    def _():
        o_ref[...] = _rmsnorm(x1_ref[...] + o_ref[...], gf_ref[...])


def _tile(n, want):
    t = min(n, want)
    assert n % t == 0, (n, want)
    return t


def _row(v):
    return v.reshape(1, -1).astype(F32)


def _inproj(x2, g, w_in, lb_logits, *, wh):
    tok, dm = x2.shape
    tm = _tile(tok, TOKEN_TILE)
    nblk = wh // MXU_DIM_V7X
    n_a = 5 * wh
    assert w_in.shape[1] == n_a + 2 * dm and dm % wh == 0
    tok_spec = lambda w: pl.BlockSpec((tm, w), lambda i: (i, 0))
    cols = lambda n, width: pl.BlockSpec((dm, width), lambda i: (0, n), pipeline_mode=pl.Buffered(1))
    u, q, k, v, og, lf = pl.pallas_call(
        functools.partial(_inproj_a_kernel, width=wh),
        grid=(tok // tm,),
        in_specs=[tok_spec(dm), _resident((1, dm)), cols(0, n_a), _resident(lb_logits.shape)],
        out_specs=[pl.BlockSpec((nblk, tm, MXU_DIM_V7X), lambda i: (0, i, 0))] + [tok_spec(wh)] * 5,
        out_shape=[jax.ShapeDtypeStruct((nblk, tok, MXU_DIM_V7X), BF16)]
                  + [jax.ShapeDtypeStruct((tok, wh), BF16)] * 4 + [jax.ShapeDtypeStruct((tok, wh), F32)],
        compiler_params=_params("parallel"),
        name="inproj_a",
    )(x2, g, w_in, lb_logits)
    gate_blocks = 2 * dm // wh
    gs, gh = pl.pallas_call(
        _inproj_b_kernel,
        grid=(tok // tm,),
        in_specs=[tok_spec(dm), _resident((1, dm))] + [cols(n_a // wh + n, wh) for n in range(gate_blocks)],
        out_specs=[tok_spec(dm)] * 2,
        out_shape=[jax.ShapeDtypeStruct((tok, dm), BF16)] * 2,
        compiler_params=_params("parallel"),
        name="inproj_b",
    )(x2, g, *([w_in] * gate_blocks))
    return u, q, k, v, og, lf, gs, gh


def _s5_prep(a_re, a_im, log_dt, b_re, b_im, c_re, c_im, d):
    groups = a_re.shape[0]
    nblk = groups // S5_BLOCK_GROUPS
    ns = S5_BLOCK_STATE
    tile_c = S5_BLOCK_GROUPS * S5_GROUP
    per_state = lambda t: t.astype(F32).reshape(nblk, 1, ns)
    blk3 = lambda r, c: pl.BlockSpec((1, r, c), lambda j: (j, 0, 0))
    pw_rows = 24
    return pl.pallas_call(
        _s5_prep_kernel,
        grid=(nblk,),
        in_specs=[blk3(tile_c, ns), blk3(tile_c, ns), blk3(ns, tile_c), blk3(ns, tile_c),
                  blk3(1, ns), blk3(1, ns), blk3(1, ns), blk3(1, tile_c)],
        out_specs=[pl.BlockSpec((1, S5_SUB, tile_c, 2 * ns), lambda j: (j, 0, 0, 0)), blk3(2 * ns, tile_c),
                   pl.BlockSpec((1, CHUNK, tile_c, tile_c), lambda j: (j, 0, 0, 0)), blk3(pw_rows, 2 * ns)],
        out_shape=[jax.ShapeDtypeStruct((nblk, S5_SUB, tile_c, 2 * ns), BF16),
                   jax.ShapeDtypeStruct((nblk, 2 * ns, tile_c), BF16),
                   jax.ShapeDtypeStruct((nblk, CHUNK, tile_c, tile_c), BF16),
                   jax.ShapeDtypeStruct((nblk, pw_rows, 2 * ns), F32)],
        compiler_params=_params("parallel"),
        name="s5_prep",
    )(_blockdiag(jnp.swapaxes(b_re, 1, 2).astype(F32), nblk),
      _blockdiag(jnp.swapaxes(b_im, 1, 2).astype(F32), nblk),
      _blockdiag(jnp.swapaxes(c_re, 1, 2).astype(F32), nblk),
      _blockdiag(jnp.swapaxes(c_im, 1, 2).astype(F32), nblk),
      per_state(a_re), per_state(a_im),
      per_state(jnp.broadcast_to(log_dt[:, None], (groups, S5_STATE))),
      d.astype(F32).reshape(nblk, 1, tile_c))


def _s5_scan(u_rows, bc, cc, klag, pw, *, seq):
    nblk, rows_total, lanes = u_rows.shape
    ns = S5_BLOCK_STATE
    rows_seq = seq // CHUNK
    rt = _tile(rows_seq, S5_ROW_TILE)
    per_tile = lambda a: pl.BlockSpec((None,) + a.shape[1:], lambda j, i: (j,) + (0,) * (a.ndim - 1))
    rows_spec = pl.BlockSpec((None, rt, lanes), lambda j, i: (j, i, 0))
    return pl.pallas_call(
        functools.partial(_s5_scan_kernel, tiles_per_seq=rows_seq // rt),
        grid=(nblk, rows_total // rt),
        in_specs=[rows_spec, per_tile(bc), per_tile(cc), per_tile(klag), per_tile(pw)],
        out_specs=rows_spec,
        out_shape=jax.ShapeDtypeStruct(u_rows.shape, BF16),
        scratch_shapes=[pltpu.VMEM((rt, 2 * ns), F32), pltpu.VMEM((8, 2 * ns), F32)],
        compiler_params=_params("parallel", "arbitrary"),
        name="s5_scan",
    )(u_rows, bc, cc, klag, pw)


def _s5_tail(y, gs, w_glu, b_glu, w_proj):
    nblk, tok, tw = y.shape
    dm = gs.shape[1]
    tm = _tile(tok, TOKEN_TILE)
    tok_spec = lambda w: pl.BlockSpec((tm, w), lambda i: (i, 0))
    return pl.pallas_call(
        _s5_tail_kernel,
        grid=(tok // tm,),
        in_specs=[pl.BlockSpec((nblk, tm, tw), lambda i: (0, i, 0)), tok_spec(dm), _resident(w_glu.shape),
                  _resident(b_glu.shape), _resident(w_proj.shape)],
        out_specs=tok_spec(dm),
        out_shape=jax.ShapeDtypeStruct((tok, dm), BF16),
        compiler_params=_params("parallel"),
        name="s5_tail",
    )(y, gs, w_glu, b_glu, w_proj)


def _hgrn_merge(q, k, v, og, lf, gh, ms, x2, w_proj, w_out, norm_g, ffn_g, *, seq):
    tok, wh = q.shape
    dm = x2.shape[1]
    heads = wh // HEAD_DIM
    tl = _tile(seq, HGRN_TOKEN_TILE)
    lt = seq // tl
    seq_spec = lambda w: pl.BlockSpec((tl, w), lambda b, l: (b * lt + l, 0))
    return pl.pallas_call(
        functools.partial(_hgrn_kernel, heads=heads),
        grid=(tok // seq, lt),
        in_specs=[seq_spec(wh)] * 5 + [seq_spec(dm)] * 3
                 + [_resident(w_proj.shape), _resident(w_out.shape), _resident((1, wh)), _resident((1, dm))],
        out_specs=[seq_spec(dm)] * 2,
        out_shape=[jax.ShapeDtypeStruct((tok, dm), F32), jax.ShapeDtypeStruct((tok, dm), BF16)],
        scratch_shapes=[pltpu.VMEM((heads, HEAD_DIM, HEAD_DIM), F32),
                        pltpu.VMEM((tl, wh), BF16), pltpu.VMEM((tl, wh), BF16),
                        pltpu.VMEM((tl, wh), F32), pltpu.VMEM((tl, wh), F32), pltpu.VMEM((tl, wh), F32)],
        compiler_params=_params("arbitrary", "arbitrary"),
        name="hgrn_merge",
    )(q, k, v, og, lf, gh, ms, x2, w_proj, w_out, norm_g, ffn_g)


def _ffn(h2, x1, w_up, cw, cb, w_down, final_g, *, seq):
    tok, dm = h2.shape
    dff = w_down.shape[0]
    tf = _tile(dff, FFN_FF_TILE)
    nf = dff // tf
    tmf = _tile(seq, FFN_TOKEN_TILE)
    halo = 16
    return pl.pallas_call(
        functools.partial(_ffn_kernel, tiles_per_seq=seq // tmf),
        grid=(tok // tmf, nf),
        in_specs=[pl.BlockSpec((tmf, dm), lambda i, j: (i, 0)),
                  pl.BlockSpec((halo, dm), lambda i, j: (jnp.maximum(i * (tmf // halo) - 1, 0), 0)),
                  pl.BlockSpec((dm, tf), lambda i, j: (0, j)),
                  pl.BlockSpec((dm, tf), lambda i, j: (0, nf + j)),
                  pl.BlockSpec((CONV_WIDTH, tf), lambda i, j: (0, j)),
                  pl.BlockSpec((CONV_WIDTH, tf), lambda i, j: (0, nf + j)),
                  pl.BlockSpec((1, tf), lambda i, j: (0, j)),
                  pl.BlockSpec((1, tf), lambda i, j: (0, nf + j)),
                  pl.BlockSpec((tf, dm), lambda i, j: (j, 0)),
                  pl.BlockSpec((tmf, dm), lambda i, j: (i, 0)),
                  pl.BlockSpec((1, dm), lambda i, j: (0, 0))],
        out_specs=pl.BlockSpec((tmf, dm), lambda i, j: (i, 0)),
        out_shape=jax.ShapeDtypeStruct((tok, dm), F32),
        scratch_shapes=[pltpu.VMEM((halo + tmf, dm), BF16)],
        compiler_params=_params("parallel", "arbitrary"),
        name="ffn",
    )(h2, h2, w_up, w_up, cw, cw, cb, cb, w_down, x1, final_g)


def kernel(x, ln_mix_g, w_in, s5_a_re, s5_a_im, s5_log_dt, s5_b_re, s5_b_im, s5_c_re, s5_c_im,
           s5_d, s5_w_glu, s5_b_glu, w_proj_s5, hgrn_lb_logits, hgrn_norm_g, w_proj_hgrn,
           w_out, ln_ffn_g, w_up, conv_w, conv_b, w_down, ln_final_g):
    bsz, seq, dm = x.shape
    assert ln_mix_g.shape[0] == 1, "single-layer block"
    tok = bsz * seq
    ws5 = s5_w_glu.shape[1]
    wh = hgrn_norm_g.shape[1]
    assert ws5 == wh and ws5 % MXU_DIM_V7X == 0 and seq % CHUNK == 0
    assert w_in.shape[2] == ws5 + 4 * wh + 2 * dm

    x2 = x.reshape(tok, dm)
    u, q, k, v, og, lf, gs, gh = _inproj(x2, _row(ln_mix_g[0]), w_in[0].astype(BF16),
                                         hgrn_lb_logits.astype(F32), wh=wh)

    bc, cc, klag, pw = _s5_prep(s5_a_re[0], s5_a_im[0], s5_log_dt[0], s5_b_re[0], s5_b_im[0],
                                s5_c_re[0], s5_c_im[0], s5_d[0])
    nblk, _, tw = u.shape
    y = _s5_scan(u.reshape(nblk, tok // CHUNK, CHUNK * tw), bc, cc, klag, pw, seq=seq).reshape(u.shape)
    ms = _s5_tail(y, gs, s5_w_glu[0].astype(BF16), _row(s5_b_glu[0]), w_proj_s5[0].astype(BF16))

    x1, h2 = _hgrn_merge(q, k, v, og, lf, gh, ms, x2, w_proj_hgrn[0].astype(BF16), w_out[0].astype(BF16),
                         _row(hgrn_norm_g[0]), _row(ln_ffn_g[0]), seq=seq)

    out = _ffn(h2, x1, w_up[0].astype(BF16), conv_w[0].astype(F32), conv_b[0].astype(F32).reshape(1, -1),
               w_down[0].astype(BF16), _row(ln_final_g), seq=seq)
    return out.reshape(bsz, seq, dm)
```

```python
import functools
import math

import jax
import jax.numpy as jnp
from jax import lax
from jax.experimental import pallas as pl
from jax.experimental.pallas import tpu as pltpu

F32 = jnp.float32
BF16 = jnp.bfloat16

RMS_EPS = 1e-6
S5_MAX_RE = -1e-4
S5_GROUP = 16
S5_STATE = 64
HEAD_DIM = 128
CONV_WIDTH = 3

LANES_V7X = 128
MXU_DIM_V7X = 256
CHUNK = 16
S5_SUB = 4
S5_BLOCK_GROUPS = MXU_DIM_V7X // S5_GROUP
S5_BLOCK_STATE = S5_BLOCK_GROUPS * S5_STATE
VMEM_LIMIT_BYTES = 56 * 1024 * 1024
HGRN_FACTOR_LIMIT = 60.0
HGRN_WIDE_CHUNK = 64

TOKEN_TILE = 512
S5_ROW_TILE = 256
HGRN_TOKEN_TILE = 256
FFN_TOKEN_TILE = 512
FFN_FF_TILE = 512


def _sigmoid(y):
    return 1.0 / (1.0 + jnp.exp(-y))


def _rmsnorm(x, g):
    return x * lax.rsqrt(jnp.mean(x * x, axis=-1, keepdims=True) + RMS_EPS) * g


def _params(*sem):
    return pltpu.CompilerParams(dimension_semantics=sem, vmem_limit_bytes=VMEM_LIMIT_BYTES)


def _resident(shape):
    return pl.BlockSpec(shape, lambda *_: (0,) * len(shape), pipeline_mode=pl.Buffered(1))


def _inproj_a_kernel(x_ref, g_ref, w_ref, lbl_ref, u_ref, q_ref, k_ref, v_ref, og_ref, lf_ref, *, width):
    h = _rmsnorm(x_ref[...], g_ref[...]).astype(BF16)

    def proj(i):
        return jnp.dot(h, w_ref[:, i * width:(i + 1) * width], preferred_element_type=F32)

    a = proj(0).astype(BF16)
    for j in range(u_ref.shape[0]):
        u_ref[j] = a[:, j * MXU_DIM_V7X:(j + 1) * MXU_DIM_V7X]
    a = proj(1)
    q_ref[...] = (a * _sigmoid(a)).astype(BF16)
    lbl = lbl_ref[...]
    e = jnp.exp(lbl - jnp.max(lbl, axis=0, keepdims=True))
    lb = e[0:1, :] / jnp.sum(e, axis=0, keepdims=True)
    sg = _sigmoid(proj(2))
    lf_ref[...] = jnp.log(lb + (1.0 - lb) * sg)
    k_ref[...] = ((1.0 - lb) * (1.0 - sg)).astype(BF16)
    v_ref[...] = proj(3).astype(BF16)
    a = proj(4)
    og_ref[...] = (a * _sigmoid(a)).astype(BF16)


def _inproj_b_kernel(x_ref, g_ref, *refs):
    w_refs, o_refs = refs[:-2], refs[-2:]
    per_out = len(w_refs) // len(o_refs)
    h = _rmsnorm(x_ref[...], g_ref[...]).astype(BF16)
    for n, w_ref in enumerate(w_refs):
        cols = w_ref.shape[1]
        lo = (n % per_out) * cols
        o_refs[n // per_out][:, lo:lo + cols] = _sigmoid(
            jnp.dot(h, w_ref[...], preferred_element_type=F32)).astype(BF16)


def _s5_prep_kernel(b_ref, c_ref, lam_ref, d_ref, bc_ref, cc_ref, klag_ref, pw_ref):
    ns = S5_BLOCK_STATE
    lam_re = jnp.minimum(lam_ref[0, 0], S5_MAX_RE)
    lam_im = lam_ref[1, 0]
    dt = jnp.exp(lam_ref[2, 0])
    mag = jnp.exp(lam_re * dt)
    ab_re = mag * jnp.cos(lam_im * dt)
    ab_im = mag * jnp.sin(lam_im * dt)
    den = lam_re * lam_re + lam_im * lam_im
    nr = ab_re - 1.0
    ni = ab_im
    cf_re = (nr * lam_re + ni * lam_im) / den
    cf_im = (ni * lam_re - nr * lam_im) / den
    bre = b_ref[0, 0]
    bim = b_ref[1, 0]
    b_re = cf_re * bre - cf_im * bim
    b_im = cf_re * bim + cf_im * bre
    cre = c_ref[0, 0].astype(BF16)
    cim = c_ref[1, 0].astype(BF16)
    cc_ref[0] = jnp.concatenate([cre, -cim], axis=0)

    p_re = jnp.ones_like(ab_re)
    p_im = jnp.zeros_like(ab_re)
    pw_ref[...] = jnp.zeros_like(pw_ref)
    for n in range(CHUNK + 1):
        pw_ref[0, n:n + 1, :] = jnp.concatenate([p_re, p_im], axis=-1)
        if n < S5_SUB:
            bc_ref[0, S5_SUB - 1 - n] = jnp.concatenate(
                [b_re * p_re - b_im * p_im, b_re * p_im + b_im * p_re], axis=-1).astype(BF16)
        if n < CHUNK:
            t_re = (b_re * p_re - b_im * p_im).astype(BF16)
            t_im = (b_re * p_im + b_im * p_re).astype(BF16)
            tap = (jnp.dot(t_re, cre, preferred_element_type=F32)
                   - jnp.dot(t_im, cim, preferred_element_type=F32))
            if n == 0:
                r = lax.broadcasted_iota(jnp.int32, tap.shape, 0)
                c = lax.broadcasted_iota(jnp.int32, tap.shape, 1)
                tap = tap + jnp.where(r == c, d_ref[0], 0.0)
            klag_ref[0, n] = tap.astype(BF16)
        p_re, p_im = p_re * ab_re - p_im * ab_im, p_re * ab_im + p_im * ab_re


def _blockdiag(t, nb):
    n, g, r, c = t.shape
    gb = g // nb
    t = t.reshape(n, nb, gb, r, c)
    eye = jnp.eye(gb, dtype=jnp.bool_)[None, None, :, None, :, None]
    out = jnp.where(eye, t[:, :, :, :, None, :], jnp.zeros((), t.dtype))
    return out.reshape(n, nb, gb * r, gb * c)


def _s5_scan_kernel(u_ref, bc_ref, cc_ref, klag_ref, pw_ref, y_ref, z_ref, carry_ref, *, tiles_per_seq):
    ns = S5_BLOCK_STATE
    tw = MXU_DIM_V7X
    rt = u_ref.shape[0] // CHUNK
    by_pos = jnp.swapaxes(u_ref[...].reshape(rt, CHUNK, tw), 0, 1)

    def u_tile(s):
        return by_pos[s]

    def pw(n):
        return pw_ref[n:n + 1, 0:ns], pw_ref[n:n + 1, ns:2 * ns]

    @pl.when(pl.program_id(1) % tiles_per_seq == 0)
    def _():
        carry_ref[...] = jnp.zeros_like(carry_ref)

    acc_re = jnp.zeros((rt, ns), F32)
    acc_im = jnp.zeros((rt, ns), F32)
    for sub in range(CHUNK // S5_SUB):
        bb = jnp.dot(u_tile(sub * S5_SUB), bc_ref[0], preferred_element_type=F32)
        for r in range(1, S5_SUB):
            bb = bb + jnp.dot(u_tile(sub * S5_SUB + r), bc_ref[r], preferred_element_type=F32)
        p_re, p_im = pw(CHUNK - S5_SUB * (sub + 1))
        acc_re = acc_re + bb[:, :ns] * p_re - bb[:, ns:] * p_im
        acc_im = acc_im + bb[:, :ns] * p_im + bb[:, ns:] * p_re
    z_ref[:, 0:ns] = acc_re
    z_ref[:, ns:2 * ns] = acc_im

    a_re, a_im = pw(CHUNK)

    def body(r, carry):
        s_re, s_im = carry
        z_re = z_ref[pl.ds(r, 1), 0:ns]
        z_im = z_ref[pl.ds(r, 1), ns:2 * ns]
        z_ref[pl.ds(r, 1), 0:ns] = s_re
        z_ref[pl.ds(r, 1), ns:2 * ns] = s_im
        return (a_re * s_re - a_im * s_im + z_re, a_re * s_im + a_im * s_re + z_im)

    s_re, s_im = lax.fori_loop(0, rt, body, (carry_ref[0:1, 0:ns], carry_ref[0:1, ns:2 * ns]))
    carry_ref[0:1, 0:ns] = s_re
    carry_ref[0:1, ns:2 * ns] = s_im

    s_re = z_ref[:, 0:ns]
    s_im = z_ref[:, ns:2 * ns]
    ys = []
    for t in range(CHUNK):
        y = jnp.dot(u_tile(t), klag_ref[0], preferred_element_type=F32)
        for s in range(t):
            y = y + jnp.dot(u_tile(s), klag_ref[t - s], preferred_element_type=F32)
        p_re, p_im = pw(t + 1)
        rot = jnp.concatenate([s_re * p_re - s_im * p_im, s_re * p_im + s_im * p_re], axis=-1)
        y = y + jnp.dot(rot.astype(BF16), cc_ref[...], preferred_element_type=F32)
        ys.append(y.astype(BF16))
    y_ref[...] = jnp.swapaxes(jnp.stack(ys, axis=0), 0, 1).reshape(rt * CHUNK, tw)


def _s5_tail_kernel(y_ref, gs_ref, wg_ref, bg_ref, wp_ref, o_ref):
    y = jnp.concatenate([y_ref[j] for j in range(y_ref.shape[0])], axis=-1).astype(F32)
    z = 0.5 * y * (1.0 + jnp.tanh(math.sqrt(2.0 / math.pi) * (y + 0.044715 * (y * y * y))))
    gl = jnp.dot(z.astype(BF16), wg_ref[...], preferred_element_type=F32) + bg_ref[...]
    zz = (z * _sigmoid(gl)).astype(BF16)
    ys = jnp.dot(zz, wp_ref[...], preferred_element_type=F32)
    o_ref[...] = (gs_ref[...].astype(F32) * ys).astype(BF16)


def _split3(x):
    hi = x.astype(BF16)
    r = x - hi.astype(F32)
    mid = r.astype(BF16)
    lo = (r - mid.astype(F32)).astype(BF16)
    return hi, mid, lo


def _hgrn_kernel(q_ref, k_ref, v_ref, og_ref, lf_ref, gh_ref, ms_ref, x_ref,
                 wph_ref, wout_ref, ng_ref, g2_ref, x1_ref, h2_ref,
                 st_ref, qt_ref, kh_ref, b_ref, gt_ref, o_ref, *, heads):
    tl = q_ref.shape[0]
    hd = HEAD_DIM

    @pl.when(pl.program_id(1) == 0)
    def _():
        st_ref[...] = jnp.zeros_like(st_ref)

    r = lax.broadcasted_iota(jnp.int32, (tl, tl), 0)
    c = lax.broadcasted_iota(jnp.int32, (tl, tl), 1)
    pieces = _split3(lf_ref[...])

    def chunk_sums(chunk):
        same = (r // chunk) == (c // chunk)
        causal = jnp.logical_and(same, c <= r)
        tri = jnp.where(causal, 1.0, 0.0).astype(BF16)
        blk = jnp.where(same, 1.0, 0.0).astype(BF16)
        b = jnp.zeros((tl, heads * hd), F32)
        bt = jnp.zeros((tl, heads * hd), F32)
        for piece in pieces:
            b = b + jnp.dot(tri, piece, preferred_element_type=F32)
            bt = bt + jnp.dot(blk, piece, preferred_element_type=F32)
        return causal, b, bt

    def store_decayed(b, bt):
        b_ref[...] = b
        gt_ref[...] = jnp.exp(bt)
        qt_ref[...] = (q_ref[...].astype(F32) * jnp.exp(b)).astype(BF16)
        kh_ref[...] = (k_ref[...].astype(F32) * jnp.exp(bt - b)).astype(BF16)

    half = CHUNK // 2
    row = lax.broadcasted_iota(jnp.int32, (half, 1), 0)

    def pairs_exact(rows, lanes):
        q16 = q_ref[rows, lanes].astype(F32)
        k16 = k_ref[rows, lanes].astype(F32)
        v16 = v_ref[rows, lanes].astype(F32)
        b16 = b_ref[rows, lanes]
        q_top, q_bot = q16[:half], q16[half:]
        b_top, b_bot = b16[:half], b16[half:]
        acc_top = jnp.zeros((half, hd), F32)
        acc_bot = jnp.zeros((half, hd), F32)
        for s in range(CHUNK):
            bs = b16[s:s + 1]
            ks = k16[s:s + 1]
            vs = v16[s:s + 1]
            if s < half:
                w = jnp.sum(q_top * ks * jnp.exp(jnp.minimum(b_top - bs, 0.0)), axis=-1, keepdims=True)
                acc_top = acc_top + jnp.where(row >= s, w, 0.0) * vs
                w = jnp.sum(q_bot * ks * jnp.exp(b_bot - bs), axis=-1, keepdims=True)
                acc_bot = acc_bot + w * vs
            else:
                w = jnp.sum(q_bot * ks * jnp.exp(jnp.minimum(b_bot - bs, 0.0)), axis=-1, keepdims=True)
                acc_bot = acc_bot + jnp.where(row + half >= s, w, 0.0) * vs
        return jnp.concatenate([acc_top, acc_bot], axis=0)

    def carry_state(rows, first_row, h):
        lanes = slice(h * hd, (h + 1) * hd)
        st = st_ref[h]
        inter = lax.dot_general(qt_ref[rows, lanes], st.astype(BF16),
                                (((1,), (1,)), ((), ())), preferred_element_type=F32)
        upd = lax.dot_general(v_ref[rows, lanes], kh_ref[rows, lanes],
                              (((0,), (0,)), ((), ())), preferred_element_type=F32)
        st_ref[h] = gt_ref[first_row, lanes] * st + upd
        return inter

    def exact_step(n, _):
        r0 = pl.multiple_of(n * CHUNK, CHUNK)
        rows = pl.ds(r0, CHUNK)
        for h in range(heads):
            lanes = slice(h * hd, (h + 1) * hd)
            o_ref[rows, lanes] = carry_state(rows, pl.ds(r0, 1), h) + pairs_exact(rows, lanes)
        return 0

    causal, b, bt = chunk_sums(HGRN_WIDE_CHUNK)
    worst = jnp.max(jnp.max(-bt, axis=0, keepdims=True), axis=1, keepdims=True)[0, 0]
    factorable = worst <= HGRN_FACTOR_LIMIT

    @pl.when(factorable)
    def _():
        store_decayed(b, bt)
        kx = (k_ref[...].astype(F32) * jnp.exp(-b)).astype(BF16)
        for h in range(heads):
            lanes = slice(h * hd, (h + 1) * hd)
            sc = lax.dot_general(qt_ref[:, lanes], kx[:, lanes], (((1,), (1,)), ((), ())),
                                 preferred_element_type=F32)
            o_ref[:, lanes] = jnp.dot(jnp.where(causal, sc, 0.0).astype(BF16), v_ref[:, lanes],
                                      preferred_element_type=F32)
        for n in range(tl // HGRN_WIDE_CHUNK):
            rows = slice(n * HGRN_WIDE_CHUNK, (n + 1) * HGRN_WIDE_CHUNK)
            for h in range(heads):
                lanes = slice(h * hd, (h + 1) * hd)
                o_ref[rows, lanes] += carry_state(rows, slice(rows.start, rows.start + 1), h)

    @pl.when(jnp.logical_not(factorable))
    def _():
        store_decayed(*chunk_sums(CHUNK)[1:])
        lax.fori_loop(0, tl // CHUNK, exact_step, 0)

    parts = []
    for h in range(heads):
        lanes = slice(h * hd, (h + 1) * hd)
        o = o_ref[:, lanes]
        o = o * lax.rsqrt(jnp.mean(o * o, axis=-1, keepdims=True) + RMS_EPS)
        parts.append((o * ng_ref[:, lanes] * og_ref[:, lanes].astype(F32)).astype(BF16))
    on = jnp.concatenate(parts, axis=-1)
    yh = jnp.dot(on, wph_ref[...], preferred_element_type=F32)
    merged = ms_ref[...].astype(F32) + gh_ref[...].astype(F32) * yh
    x1 = x_ref[...] + jnp.dot(merged.astype(BF16), wout_ref[...], preferred_element_type=F32)
    x1_ref[...] = x1
    h2_ref[...] = _rmsnorm(x1, g2_ref[...]).astype(BF16)


def _ffn_kernel(h_ref, halo_ref, wug_ref, wuv_ref, cwg_ref, cwv_ref, cbg_ref, cbv_ref, wd_ref,
                x1_ref, gf_ref, o_ref, hcat_ref, *, tiles_per_seq):
    i = pl.program_id(0)
    j = pl.program_id(1)
    tm = h_ref.shape[0]
    nh = halo_ref.shape[0]

    @pl.when(j == 0)
    def _():
        keep = jnp.where(i % tiles_per_seq == 0, 0.0, 1.0).astype(BF16)
        hcat_ref[0:nh] = halo_ref[...] * keep
        hcat_ref[nh:nh + tm] = h_ref[...]
        o_ref[...] = jnp.zeros_like(o_ref)

    def conv_up(w_ref, cw_ref, cb_ref):
        ext = jnp.dot(hcat_ref[...], w_ref[...], preferred_element_type=F32)
        out = cb_ref[...]
        for tap in range(CONV_WIDTH):
            back = CONV_WIDTH - 1 - tap
            out = out + cw_ref[tap:tap + 1, :] * ext[nh - back:nh - back + tm]
        return out

    gate = conv_up(wug_ref, cwg_ref, cbg_ref)
    val = conv_up(wuv_ref, cwv_ref, cbv_ref)
    act = (0.5 * gate * (1.0 + jnp.tanh(0.5 * gate)) * val).astype(BF16)
    o_ref[...] += jnp.dot(act, wd_ref[...], preferred_element_type=F32)

    @pl.when(j == pl.num_programs(1) - 1)
    def _():
        o_ref[...] = _rmsnorm(x1_ref[...] + o_ref[...], gf_ref[...])


def _tile(n, want):
    t = min(n, want)
    assert n % t == 0, (n, want)
    return t


def _row(v):
    return v.reshape(1, -1).astype(F32)


def _inproj(x2, g, w_in, lb_logits, *, wh):
    tok, dm = x2.shape
    tm = _tile(tok, TOKEN_TILE)
    nblk = wh // MXU_DIM_V7X
    n_a = 5 * wh
    assert w_in.shape[1] == n_a + 2 * dm and dm % wh == 0
    tok_spec = lambda w: pl.BlockSpec((tm, w), lambda i: (i, 0))
    cols = lambda n, width: pl.BlockSpec((dm, width), lambda i: (0, n), pipeline_mode=pl.Buffered(1))
    u, q, k, v, og, lf = pl.pallas_call(
        functools.partial(_inproj_a_kernel, width=wh),
        grid=(tok // tm,),
        in_specs=[tok_spec(dm), _resident((1, dm)), cols(0, n_a), _resident(lb_logits.shape)],
        out_specs=[pl.BlockSpec((nblk, tm, MXU_DIM_V7X), lambda i: (0, i, 0))] + [tok_spec(wh)] * 5,
        out_shape=[jax.ShapeDtypeStruct((nblk, tok, MXU_DIM_V7X), BF16)]
                  + [jax.ShapeDtypeStruct((tok, wh), BF16)] * 4 + [jax.ShapeDtypeStruct((tok, wh), F32)],
        compiler_params=_params("parallel"),
        name="inproj_a",
    )(x2, g, w_in, lb_logits)
    gate_blocks = 2 * dm // wh
    gs, gh = pl.pallas_call(
        _inproj_b_kernel,
        grid=(tok // tm,),
        in_specs=[tok_spec(dm), _resident((1, dm))] + [cols(n_a // wh + n, wh) for n in range(gate_blocks)],
        out_specs=[tok_spec(dm)] * 2,
        out_shape=[jax.ShapeDtypeStruct((tok, dm), BF16)] * 2,
        compiler_params=_params("parallel"),
        name="inproj_b",
    )(x2, g, *([w_in] * gate_blocks))
    return u, q, k, v, og, lf, gs, gh


def _s5_prep(a_re, a_im, log_dt, b_re, b_im, c_re, c_im, d):
    groups = a_re.shape[0]
    nblk = groups // S5_BLOCK_GROUPS
    ns = S5_BLOCK_STATE
    tile_c = S5_BLOCK_GROUPS * S5_GROUP
    blk3 = lambda r, c: pl.BlockSpec((1, r, c), lambda j: (j, 0, 0))
    stacked = lambda n, r, c: pl.BlockSpec((n, 1, r, c), lambda j: (0, j, 0, 0))
    pw_rows = 24
    per_state = jnp.stack([a_re, a_im, jnp.broadcast_to(log_dt[:, None], a_re.shape)]).astype(F32)
    return pl.pallas_call(
        _s5_prep_kernel,
        grid=(nblk,),
        in_specs=[stacked(2, tile_c, ns), stacked(2, ns, tile_c), stacked(3, 1, ns), blk3(1, tile_c)],
        out_specs=[pl.BlockSpec((1, S5_SUB, tile_c, 2 * ns), lambda j: (j, 0, 0, 0)), blk3(2 * ns, tile_c),
                   pl.BlockSpec((1, CHUNK, tile_c, tile_c), lambda j: (j, 0, 0, 0)), blk3(pw_rows, 2 * ns)],
        out_shape=[jax.ShapeDtypeStruct((nblk, S5_SUB, tile_c, 2 * ns), BF16),
                   jax.ShapeDtypeStruct((nblk, 2 * ns, tile_c), BF16),
                   jax.ShapeDtypeStruct((nblk, CHUNK, tile_c, tile_c), BF16),
                   jax.ShapeDtypeStruct((nblk, pw_rows, 2 * ns), F32)],
        compiler_params=_params("parallel"),
        name="s5_prep",
    )(_blockdiag(jnp.swapaxes(jnp.stack([b_re, b_im]), 2, 3).astype(F32), nblk),
      _blockdiag(jnp.swapaxes(jnp.stack([c_re, c_im]), 2, 3).astype(F32), nblk),
      per_state.reshape(3, nblk, 1, ns),
      d.astype(F32).reshape(nblk, 1, tile_c))


def _s5_scan(u, bc, cc, klag, pw, *, seq):
    nblk, tok, tw = u.shape
    ns = S5_BLOCK_STATE
    rows_seq = seq // CHUNK
    rt = _tile(rows_seq, S5_ROW_TILE)
    per_tile = lambda a: pl.BlockSpec((None,) + a.shape[1:], lambda j, i: (j,) + (0,) * (a.ndim - 1))
    rows_spec = pl.BlockSpec((None, rt * CHUNK, tw), lambda j, i: (j, i, 0))
    return pl.pallas_call(
        functools.partial(_s5_scan_kernel, tiles_per_seq=rows_seq // rt),
        grid=(nblk, tok // (rt * CHUNK)),
        in_specs=[rows_spec, per_tile(bc), per_tile(cc), per_tile(klag), per_tile(pw)],
        out_specs=rows_spec,
        out_shape=jax.ShapeDtypeStruct(u.shape, BF16),
        scratch_shapes=[pltpu.VMEM((rt, 2 * ns), F32), pltpu.VMEM((8, 2 * ns), F32)],
        compiler_params=_params("parallel", "arbitrary"),
        name="s5_scan",
    )(u, bc, cc, klag, pw)


def _s5_tail(y, gs, w_glu, b_glu, w_proj):
    nblk, tok, tw = y.shape
    dm = gs.shape[1]
    tm = _tile(tok, TOKEN_TILE)
    tok_spec = lambda w: pl.BlockSpec((tm, w), lambda i: (i, 0))
    return pl.pallas_call(
        _s5_tail_kernel,
        grid=(tok // tm,),
        in_specs=[pl.BlockSpec((nblk, tm, tw), lambda i: (0, i, 0)), tok_spec(dm), _resident(w_glu.shape),
                  _resident(b_glu.shape), _resident(w_proj.shape)],
        out_specs=tok_spec(dm),
        out_shape=jax.ShapeDtypeStruct((tok, dm), BF16),
        compiler_params=_params("parallel"),
        name="s5_tail",
    )(y, gs, w_glu, b_glu, w_proj)


def _hgrn_merge(q, k, v, og, lf, gh, ms, x2, w_proj, w_out, norm_g, ffn_g, *, seq):
    tok, wh = q.shape
    dm = x2.shape[1]
    heads = wh // HEAD_DIM
    tl = _tile(seq, HGRN_TOKEN_TILE)
    lt = seq // tl
    seq_spec = lambda w: pl.BlockSpec((tl, w), lambda b, l: (b * lt + l, 0))
    return pl.pallas_call(
        functools.partial(_hgrn_kernel, heads=heads),
        grid=(tok // seq, lt),
        in_specs=[seq_spec(wh)] * 5 + [seq_spec(dm)] * 3
                 + [_resident(w_proj.shape), _resident(w_out.shape), _resident((1, wh)), _resident((1, dm))],
        out_specs=[seq_spec(dm)] * 2,
        out_shape=[jax.ShapeDtypeStruct((tok, dm), F32), jax.ShapeDtypeStruct((tok, dm), BF16)],
        scratch_shapes=[pltpu.VMEM((heads, HEAD_DIM, HEAD_DIM), F32),
                        pltpu.VMEM((tl, wh), BF16), pltpu.VMEM((tl, wh), BF16),
                        pltpu.VMEM((tl, wh), F32), pltpu.VMEM((tl, wh), F32), pltpu.VMEM((tl, wh), F32)],
        compiler_params=_params("arbitrary", "arbitrary"),
        name="hgrn_merge",
    )(q, k, v, og, lf, gh, ms, x2, w_proj, w_out, norm_g, ffn_g)


def _ffn(h2, x1, w_up, cw, cb, w_down, final_g, *, seq):
    tok, dm = h2.shape
    dff = w_down.shape[0]
    tf = _tile(dff, FFN_FF_TILE)
    nf = dff // tf
    tmf = _tile(seq, FFN_TOKEN_TILE)
    halo = 16
    return pl.pallas_call(
        functools.partial(_ffn_kernel, tiles_per_seq=seq // tmf),
        grid=(tok // tmf, nf),
        in_specs=[pl.BlockSpec((tmf, dm), lambda i, j: (i, 0)),
                  pl.BlockSpec((halo, dm), lambda i, j: (jnp.maximum(i * (tmf // halo) - 1, 0), 0)),
                  pl.BlockSpec((dm, tf), lambda i, j: (0, j)),
                  pl.BlockSpec((dm, tf), lambda i, j: (0, nf + j)),
                  pl.BlockSpec((CONV_WIDTH, tf), lambda i, j: (0, j)),
                  pl.BlockSpec((CONV_WIDTH, tf), lambda i, j: (0, nf + j)),
                  pl.BlockSpec((1, tf), lambda i, j: (0, j)),
                  pl.BlockSpec((1, tf), lambda i, j: (0, nf + j)),
                  pl.BlockSpec((tf, dm), lambda i, j: (j, 0)),
                  pl.BlockSpec((tmf, dm), lambda i, j: (i, 0)),
                  pl.BlockSpec((1, dm), lambda i, j: (0, 0))],
        out_specs=pl.BlockSpec((tmf, dm), lambda i, j: (i, 0)),
        out_shape=jax.ShapeDtypeStruct((tok, dm), F32),
        scratch_shapes=[pltpu.VMEM((halo + tmf, dm), BF16)],
        compiler_params=_params("parallel", "arbitrary"),
        name="ffn",
    )(h2, h2, w_up, w_up, cw, cw, cb, cb, w_down, x1, final_g)


def kernel(x, ln_mix_g, w_in, s5_a_re, s5_a_im, s5_log_dt, s5_b_re, s5_b_im, s5_c_re, s5_c_im,
           s5_d, s5_w_glu, s5_b_glu, w_proj_s5, hgrn_lb_logits, hgrn_norm_g, w_proj_hgrn,
           w_out, ln_ffn_g, w_up, conv_w, conv_b, w_down, ln_final_g):
    bsz, seq, dm = x.shape
    assert ln_mix_g.shape[0] == 1, "single-layer block"
    tok = bsz * seq
    ws5 = s5_w_glu.shape[1]
    wh = hgrn_norm_g.shape[1]
    assert ws5 == wh and ws5 % MXU_DIM_V7X == 0 and seq % CHUNK == 0
    assert w_in.shape[2] == ws5 + 4 * wh + 2 * dm

    x2 = x.reshape(tok, dm)
    u, q, k, v, og, lf, gs, gh = _inproj(x2, _row(ln_mix_g[0]), w_in[0].astype(BF16),
                                         hgrn_lb_logits.astype(F32), wh=wh)

    bc, cc, klag, pw = _s5_prep(s5_a_re[0], s5_a_im[0], s5_log_dt[0], s5_b_re[0], s5_b_im[0],
                                s5_c_re[0], s5_c_im[0], s5_d[0])
    y = _s5_scan(u, bc, cc, klag, pw, seq=seq)
    ms = _s5_tail(y, gs, s5_w_glu[0].astype(BF16), _row(s5_b_glu[0]), w_proj_s5[0].astype(BF16))

    x1, h2 = _hgrn_merge(q, k, v, og, lf, gh, ms, x2, w_proj_hgrn[0].astype(BF16), w_out[0].astype(BF16),
                         _row(hgrn_norm_g[0]), _row(ln_ffn_g[0]), seq=seq)

    out = _ffn(h2, x1, w_up[0].astype(BF16), conv_w[0].astype(F32), conv_b[0].astype(F32).reshape(1, -1),
               w_down[0].astype(BF16), _row(ln_final_g), seq=seq)
    return out.reshape(bsz, seq, dm)
```

```python
import functools
import math

import jax
import jax.numpy as jnp
from jax import lax
from jax.experimental import pallas as pl
from jax.experimental.pallas import tpu as pltpu

F32 = jnp.float32
BF16 = jnp.bfloat16

RMS_EPS = 1e-6
S5_MAX_RE = -1e-4
S5_GROUP = 16
S5_STATE = 64
HEAD_DIM = 128
CONV_WIDTH = 3

LANES_V7X = 128
MXU_DIM_V7X = 256
CHUNK = 16
S5_SUB = 4
S5_BLOCK_GROUPS = MXU_DIM_V7X // S5_GROUP
S5_BLOCK_STATE = S5_BLOCK_GROUPS * S5_STATE
VMEM_LIMIT_BYTES = 56 * 1024 * 1024
HGRN_FACTOR_LIMIT = 60.0
HGRN_WIDE_CHUNK = 64

TOKEN_TILE = 512
S5_ROW_TILE = 256
HGRN_TOKEN_TILE = 256
FFN_TOKEN_TILE = 512
FFN_FF_TILE = 512


def _sigmoid(y):
    return 1.0 / (1.0 + jnp.exp(-y))


def _rmsnorm(x, g):
    return x * lax.rsqrt(jnp.mean(x * x, axis=-1, keepdims=True) + RMS_EPS) * g


def _params(*sem):
    return pltpu.CompilerParams(dimension_semantics=sem, vmem_limit_bytes=VMEM_LIMIT_BYTES)


def _resident(shape):
    return pl.BlockSpec(shape, lambda *_: (0,) * len(shape), pipeline_mode=pl.Buffered(1))


def _inproj_a_kernel(x_ref, g_ref, w_ref, lbl_ref, u_ref, q_ref, k_ref, v_ref, og_ref, lf_ref, *, width):
    h = _rmsnorm(x_ref[...], g_ref[...]).astype(BF16)

    def proj(i):
        return jnp.dot(h, w_ref[:, i * width:(i + 1) * width], preferred_element_type=F32)

    a = proj(0).astype(BF16)
    for j in range(u_ref.shape[0]):
        u_ref[j] = a[:, j * MXU_DIM_V7X:(j + 1) * MXU_DIM_V7X]
    a = proj(1)
    q_ref[...] = (a * _sigmoid(a)).astype(BF16)
    lbl = lbl_ref[...]
    e = jnp.exp(lbl - jnp.max(lbl, axis=0, keepdims=True))
    lb = e[0:1, :] / jnp.sum(e, axis=0, keepdims=True)
    sg = _sigmoid(proj(2))
    lf_ref[...] = jnp.log(lb + (1.0 - lb) * sg)
    k_ref[...] = ((1.0 - lb) * (1.0 - sg)).astype(BF16)
    v_ref[...] = proj(3).astype(BF16)
    a = proj(4)
    og_ref[...] = (a * _sigmoid(a)).astype(BF16)


def _inproj_b_kernel(x_ref, g_ref, *refs):
    w_refs, o_refs = refs[:-2], refs[-2:]
    per_out = len(w_refs) // len(o_refs)
    h = _rmsnorm(x_ref[...], g_ref[...]).astype(BF16)
    for n, w_ref in enumerate(w_refs):
        cols = w_ref.shape[1]
        lo = (n % per_out) * cols
        o_refs[n // per_out][:, lo:lo + cols] = _sigmoid(
            jnp.dot(h, w_ref[...], preferred_element_type=F32)).astype(BF16)


def _spread(x, reps):
    w = x.shape[1]
    r = lax.broadcasted_iota(jnp.int32, (w, reps * w), 0)
    c = lax.broadcasted_iota(jnp.int32, (w, reps * w), 1)
    sel = jnp.where(c % w == r, 1.0, 0.0).astype(BF16)
    out = jnp.zeros((x.shape[0], reps * w), F32)
    for piece in _split3(x):
        out = out + jnp.dot(piece, sel, preferred_element_type=F32)
    return out


def _own_group(x, row_div, lane_div):
    r = lax.broadcasted_iota(jnp.int32, x.shape, 0)
    c = lax.broadcasted_iota(jnp.int32, x.shape, 1)
    return jnp.where(r // row_div == c // lane_div, x, 0.0)


def _s5_prep_kernel(bre_ref, bim_ref, cre_ref, cim_ref, are_ref, aim_ref, ldt_ref, d_ref,
                    bc_ref, cc_ref, klag_ref, pw_ref):
    ns = S5_BLOCK_STATE
    gb, p, c = S5_BLOCK_GROUPS, S5_STATE, S5_GROUP
    b_tile = lambda ref: _own_group(_spread(ref[0], gb), p, c).T
    c_tile = lambda ref: _own_group(_spread(ref[0], gb), c, p).T
    per_lane = lambda x, w: jnp.sum(_own_group(x, 1, w), axis=0, keepdims=True)
    lam_re = jnp.minimum(per_lane(_spread(are_ref[0], gb), p), S5_MAX_RE)
    lam_im = per_lane(_spread(aim_ref[0], gb), p)
    dt = jnp.exp(per_lane(jnp.broadcast_to(ldt_ref[0], (gb, ns)), p))
    d_row = per_lane(_spread(d_ref[0], gb), c)
    mag = jnp.exp(lam_re * dt)
    ab_re = mag * jnp.cos(lam_im * dt)
    ab_im = mag * jnp.sin(lam_im * dt)
    den = lam_re * lam_re + lam_im * lam_im
    nr = ab_re - 1.0
    ni = ab_im
    cf_re = (nr * lam_re + ni * lam_im) / den
    cf_im = (ni * lam_re - nr * lam_im) / den
    bre = b_tile(bre_ref)
    bim = b_tile(bim_ref)
    b_re = cf_re * bre - cf_im * bim
    b_im = cf_re * bim + cf_im * bre
    cre = c_tile(cre_ref).astype(BF16)
    cim = c_tile(cim_ref).astype(BF16)
    cc_ref[0] = jnp.concatenate([cre, -cim], axis=0)

    p_re = jnp.ones_like(ab_re)
    p_im = jnp.zeros_like(ab_re)
    pw_ref[...] = jnp.zeros_like(pw_ref)
    for n in range(CHUNK + 1):
        pw_ref[0, n:n + 1, :] = jnp.concatenate([p_re, p_im], axis=-1)
        if n < S5_SUB:
            bc_ref[0, S5_SUB - 1 - n] = jnp.concatenate(
                [b_re * p_re - b_im * p_im, b_re * p_im + b_im * p_re], axis=-1).astype(BF16)
        if n < CHUNK:
            t_re = (b_re * p_re - b_im * p_im).astype(BF16)
            t_im = (b_re * p_im + b_im * p_re).astype(BF16)
            tap = (jnp.dot(t_re, cre, preferred_element_type=F32)
                   - jnp.dot(t_im, cim, preferred_element_type=F32))
            if n == 0:
                r = lax.broadcasted_iota(jnp.int32, tap.shape, 0)
                c = lax.broadcasted_iota(jnp.int32, tap.shape, 1)
                tap = tap + jnp.where(r == c, d_row, 0.0)
            klag_ref[0, n] = tap.astype(BF16)
        p_re, p_im = p_re * ab_re - p_im * ab_im, p_re * ab_im + p_im * ab_re


def _s5_scan_kernel(u_ref, bc_ref, cc_ref, klag_ref, pw_ref, y_ref, z_ref, carry_ref, *, tiles_per_seq):
    ns = S5_BLOCK_STATE
    tw = MXU_DIM_V7X
    rt = u_ref.shape[0] // CHUNK
    by_pos = jnp.swapaxes(u_ref[...].reshape(rt, CHUNK, tw), 0, 1)

    def u_tile(s):
        return by_pos[s]

    def pw(n):
        return pw_ref[n:n + 1, 0:ns], pw_ref[n:n + 1, ns:2 * ns]

    @pl.when(pl.program_id(1) % tiles_per_seq == 0)
    def _():
        carry_ref[...] = jnp.zeros_like(carry_ref)

    acc_re = jnp.zeros((rt, ns), F32)
    acc_im = jnp.zeros((rt, ns), F32)
    for sub in range(CHUNK // S5_SUB):
        bb = jnp.dot(u_tile(sub * S5_SUB), bc_ref[0], preferred_element_type=F32)
        for r in range(1, S5_SUB):
            bb = bb + jnp.dot(u_tile(sub * S5_SUB + r), bc_ref[r], preferred_element_type=F32)
        p_re, p_im = pw(CHUNK - S5_SUB * (sub + 1))
        acc_re = acc_re + bb[:, :ns] * p_re - bb[:, ns:] * p_im
        acc_im = acc_im + bb[:, :ns] * p_im + bb[:, ns:] * p_re
    z_ref[:, 0:ns] = acc_re
    z_ref[:, ns:2 * ns] = acc_im

    a_re, a_im = pw(CHUNK)

    def body(r, carry):
        s_re, s_im = carry
        z_re = z_ref[pl.ds(r, 1), 0:ns]
        z_im = z_ref[pl.ds(r, 1), ns:2 * ns]
        z_ref[pl.ds(r, 1), 0:ns] = s_re
        z_ref[pl.ds(r, 1), ns:2 * ns] = s_im
        return (a_re * s_re - a_im * s_im + z_re, a_re * s_im + a_im * s_re + z_im)

    s_re, s_im = lax.fori_loop(0, rt, body, (carry_ref[0:1, 0:ns], carry_ref[0:1, ns:2 * ns]))
    carry_ref[0:1, 0:ns] = s_re
    carry_ref[0:1, ns:2 * ns] = s_im

    s_re = z_ref[:, 0:ns]
    s_im = z_ref[:, ns:2 * ns]
    ys = []
    for t in range(CHUNK):
        y = jnp.dot(u_tile(t), klag_ref[0], preferred_element_type=F32)
        for s in range(t):
            y = y + jnp.dot(u_tile(s), klag_ref[t - s], preferred_element_type=F32)
        p_re, p_im = pw(t + 1)
        rot = jnp.concatenate([s_re * p_re - s_im * p_im, s_re * p_im + s_im * p_re], axis=-1)
        y = y + jnp.dot(rot.astype(BF16), cc_ref[...], preferred_element_type=F32)
        ys.append(y.astype(BF16))
    y_ref[...] = jnp.swapaxes(jnp.stack(ys, axis=0), 0, 1).reshape(rt * CHUNK, tw)


def _s5_tail_kernel(y_ref, gs_ref, wg_ref, bg_ref, wp_ref, o_ref):
    y = jnp.concatenate([y_ref[j] for j in range(y_ref.shape[0])], axis=-1).astype(F32)
    z = 0.5 * y * (1.0 + jnp.tanh(math.sqrt(2.0 / math.pi) * (y + 0.044715 * (y * y * y))))
    gl = jnp.dot(z.astype(BF16), wg_ref[...], preferred_element_type=F32) + bg_ref[...]
    zz = (z * _sigmoid(gl)).astype(BF16)
    ys = jnp.dot(zz, wp_ref[...], preferred_element_type=F32)
    o_ref[...] = (gs_ref[...].astype(F32) * ys).astype(BF16)


def _split3(x):
    hi = x.astype(BF16)
    r = x - hi.astype(F32)
    mid = r.astype(BF16)
    lo = (r - mid.astype(F32)).astype(BF16)
    return hi, mid, lo


def _hgrn_kernel(q_ref, k_ref, v_ref, og_ref, lf_ref, gh_ref, ms_ref, x_ref,
                 wph_ref, wout_ref, ng_ref, g2_ref, x1_ref, h2_ref,
                 st_ref, qt_ref, kh_ref, b_ref, gt_ref, o_ref, *, heads):
    tl = q_ref.shape[0]
    hd = HEAD_DIM

    @pl.when(pl.program_id(1) == 0)
    def _():
        st_ref[...] = jnp.zeros_like(st_ref)

    r = lax.broadcasted_iota(jnp.int32, (tl, tl), 0)
    c = lax.broadcasted_iota(jnp.int32, (tl, tl), 1)
    pieces = _split3(lf_ref[...])

    def chunk_sums(chunk):
        causal = jnp.logical_and((r // chunk) == (c // chunk), c <= r)
        tri = jnp.where(causal, 1.0, 0.0).astype(BF16)
        b = jnp.zeros((tl, heads * hd), F32)
        for piece in pieces:
            b = b + jnp.dot(tri, piece, preferred_element_type=F32)
        last = b.reshape(tl // chunk, chunk, heads * hd)[:, chunk - 1:chunk, :]
        bt = jnp.broadcast_to(last, (tl // chunk, chunk, heads * hd)).reshape(tl, heads * hd)
        return causal, b, bt

    def store_decayed(b, bt):
        b_ref[...] = b
        gt_ref[...] = jnp.exp(bt)
        qt_ref[...] = (q_ref[...].astype(F32) * jnp.exp(b)).astype(BF16)
        kh_ref[...] = (k_ref[...].astype(F32) * jnp.exp(bt - b)).astype(BF16)

    half = CHUNK // 2
    row = lax.broadcasted_iota(jnp.int32, (half, 1), 0)

    def pairs_exact(rows, lanes):
        q16 = q_ref[rows, lanes].astype(F32)
        k16 = k_ref[rows, lanes].astype(F32)
        v16 = v_ref[rows, lanes].astype(F32)
        b16 = b_ref[rows, lanes]
        q_top, q_bot = q16[:half], q16[half:]
        b_top, b_bot = b16[:half], b16[half:]
        acc_top = jnp.zeros((half, hd), F32)
        acc_bot = jnp.zeros((half, hd), F32)
        for s in range(CHUNK):
            bs = b16[s:s + 1]
            ks = k16[s:s + 1]
            vs = v16[s:s + 1]
            if s < half:
                w = jnp.sum(q_top * ks * jnp.exp(jnp.minimum(b_top - bs, 0.0)), axis=-1, keepdims=True)
                acc_top = acc_top + jnp.where(row >= s, w, 0.0) * vs
                w = jnp.sum(q_bot * ks * jnp.exp(b_bot - bs), axis=-1, keepdims=True)
                acc_bot = acc_bot + w * vs
            else:
                w = jnp.sum(q_bot * ks * jnp.exp(jnp.minimum(b_bot - bs, 0.0)), axis=-1, keepdims=True)
                acc_bot = acc_bot + jnp.where(row + half >= s, w, 0.0) * vs
        return jnp.concatenate([acc_top, acc_bot], axis=0)

    def carry_state(rows, first_row, h):
        lanes = slice(h * hd, (h + 1) * hd)
        st = st_ref[h]
        inter = lax.dot_general(qt_ref[rows, lanes], st.astype(BF16),
                                (((1,), (1,)), ((), ())), preferred_element_type=F32)
        upd = lax.dot_general(v_ref[rows, lanes], kh_ref[rows, lanes],
                              (((0,), (0,)), ((), ())), preferred_element_type=F32)
        st_ref[h] = gt_ref[first_row, lanes] * st + upd
        return inter

    def exact_step(n, _):
        r0 = pl.multiple_of(n * CHUNK, CHUNK)
        rows = pl.ds(r0, CHUNK)
        for h in range(heads):
            lanes = slice(h * hd, (h + 1) * hd)
            o_ref[rows, lanes] = carry_state(rows, pl.ds(r0, 1), h) + pairs_exact(rows, lanes)
        return 0

    causal, b, bt = chunk_sums(HGRN_WIDE_CHUNK)
    worst = jnp.max(jnp.max(-bt, axis=0, keepdims=True), axis=1, keepdims=True)[0, 0]
    factorable = worst <= HGRN_FACTOR_LIMIT

    @pl.when(factorable)
    def _():
        store_decayed(b, bt)
        kx = (k_ref[...].astype(F32) * jnp.exp(-b)).astype(BF16)
        for h in range(heads):
            lanes = slice(h * hd, (h + 1) * hd)
            sc = lax.dot_general(qt_ref[:, lanes], kx[:, lanes], (((1,), (1,)), ((), ())),
                                 preferred_element_type=F32)
            o_ref[:, lanes] = jnp.dot(jnp.where(causal, sc, 0.0).astype(BF16), v_ref[:, lanes],
                                      preferred_element_type=F32)
        for n in range(tl // HGRN_WIDE_CHUNK):
            rows = slice(n * HGRN_WIDE_CHUNK, (n + 1) * HGRN_WIDE_CHUNK)
            for h in range(heads):
                lanes = slice(h * hd, (h + 1) * hd)
                o_ref[rows, lanes] += carry_state(rows, slice(rows.start, rows.start + 1), h)

    @pl.when(jnp.logical_not(factorable))
    def _():
        store_decayed(*chunk_sums(CHUNK)[1:])
        lax.fori_loop(0, tl // CHUNK, exact_step, 0)

    parts = []
    for h in range(heads):
        lanes = slice(h * hd, (h + 1) * hd)
        o = o_ref[:, lanes]
        o = o * lax.rsqrt(jnp.mean(o * o, axis=-1, keepdims=True) + RMS_EPS)
        parts.append((o * ng_ref[:, lanes] * og_ref[:, lanes].astype(F32)).astype(BF16))
    on = jnp.concatenate(parts, axis=-1)
    yh = jnp.dot(on, wph_ref[...], preferred_element_type=F32)
    merged = ms_ref[...].astype(F32) + gh_ref[...].astype(F32) * yh
    x1 = x_ref[...] + jnp.dot(merged.astype(BF16), wout_ref[...], preferred_element_type=F32)
    x1_ref[...] = x1
    h2_ref[...] = _rmsnorm(x1, g2_ref[...]).astype(BF16)


def _ffn_kernel(h_ref, halo_ref, wug_ref, wuv_ref, cwg_ref, cwv_ref, cbg_ref, cbv_ref, wd_ref,
                x1_ref, gf_ref, o_ref, hcat_ref, *, tiles_per_seq):
    i = pl.program_id(0)
    j = pl.program_id(1)
    tm = h_ref.shape[0]
    nh = halo_ref.shape[0]

    @pl.when(j == 0)
    def _():
        keep = jnp.where(i % tiles_per_seq == 0, 0.0, 1.0).astype(BF16)
        hcat_ref[0:nh] = halo_ref[...] * keep
        hcat_ref[nh:nh + tm] = h_ref[...]
        o_ref[...] = jnp.zeros_like(o_ref)

    def conv_up(w_ref, cw_ref, cb_ref):
        ext = jnp.dot(hcat_ref[...], w_ref[...], preferred_element_type=F32)
        out = cb_ref[...]
        for tap in range(CONV_WIDTH):
            back = CONV_WIDTH - 1 - tap
            out = out + cw_ref[tap:tap + 1, :] * ext[nh - back:nh - back + tm]
        return out

    gate = conv_up(wug_ref, cwg_ref, cbg_ref)
    val = conv_up(wuv_ref, cwv_ref, cbv_ref)
    act = (0.5 * gate * (1.0 + jnp.tanh(0.5 * gate)) * val).astype(BF16)
    o_ref[...] += jnp.dot(act, wd_ref[...], preferred_element_type=F32)

    @pl.when(j == pl.num_programs(1) - 1)
    def _():
        o_ref[...] = _rmsnorm(x1_ref[...] + o_ref[...], gf_ref[...])


def _tile(n, want):
    t = min(n, want)
    assert n % t == 0, (n, want)
    return t


def _row(v):
    return v.reshape(1, -1).astype(F32)


def _inproj(x2, g, w_in, lb_logits, *, wh):
    tok, dm = x2.shape
    tm = _tile(tok, TOKEN_TILE)
    nblk = wh // MXU_DIM_V7X
    n_a = 5 * wh
    assert w_in.shape[1] == n_a + 2 * dm and dm % wh == 0
    tok_spec = lambda w: pl.BlockSpec((tm, w), lambda i: (i, 0))
    cols = lambda n, width: pl.BlockSpec((dm, width), lambda i: (0, n), pipeline_mode=pl.Buffered(1))
    u, q, k, v, og, lf = pl.pallas_call(
        functools.partial(_inproj_a_kernel, width=wh),
        grid=(tok // tm,),
        in_specs=[tok_spec(dm), _resident((1, dm)), cols(0, n_a), _resident(lb_logits.shape)],
        out_specs=[pl.BlockSpec((nblk, tm, MXU_DIM_V7X), lambda i: (0, i, 0))] + [tok_spec(wh)] * 5,
        out_shape=[jax.ShapeDtypeStruct((nblk, tok, MXU_DIM_V7X), BF16)]
                  + [jax.ShapeDtypeStruct((tok, wh), BF16)] * 4 + [jax.ShapeDtypeStruct((tok, wh), F32)],
        compiler_params=_params("parallel"),
        name="inproj_a",
    )(x2, g, w_in, lb_logits)
    gate_blocks = 2 * dm // wh
    gs, gh = pl.pallas_call(
        _inproj_b_kernel,
        grid=(tok // tm,),
        in_specs=[tok_spec(dm), _resident((1, dm))] + [cols(n_a // wh + n, wh) for n in range(gate_blocks)],
        out_specs=[tok_spec(dm)] * 2,
        out_shape=[jax.ShapeDtypeStruct((tok, dm), BF16)] * 2,
        compiler_params=_params("parallel"),
        name="inproj_b",
    )(x2, g, *([w_in] * gate_blocks))
    return u, q, k, v, og, lf, gs, gh


def _s5_prep(a_re, a_im, log_dt, b_re, b_im, c_re, c_im, d):
    groups = a_re.shape[0]
    nblk = groups // S5_BLOCK_GROUPS
    ns = S5_BLOCK_STATE
    tile_c = S5_BLOCK_GROUPS * S5_GROUP
    blk3 = lambda r, c: pl.BlockSpec((1, r, c), lambda j: (j, 0, 0))
    pw_rows = 24
    gb, p, c = S5_BLOCK_GROUPS, S5_STATE, S5_GROUP
    tiled = lambda t, r, w: t.astype(F32).reshape(nblk, r, w)
    return pl.pallas_call(
        _s5_prep_kernel,
        grid=(nblk,),
        in_specs=[blk3(gb * p, c), blk3(gb * p, c), blk3(gb * c, p), blk3(gb * c, p),
                  blk3(gb, p), blk3(gb, p), blk3(gb, 1), blk3(gb, c)],
        out_specs=[pl.BlockSpec((1, S5_SUB, tile_c, 2 * ns), lambda j: (j, 0, 0, 0)), blk3(2 * ns, tile_c),
                   pl.BlockSpec((1, CHUNK, tile_c, tile_c), lambda j: (j, 0, 0, 0)), blk3(pw_rows, 2 * ns)],
        out_shape=[jax.ShapeDtypeStruct((nblk, S5_SUB, tile_c, 2 * ns), BF16),
                   jax.ShapeDtypeStruct((nblk, 2 * ns, tile_c), BF16),
                   jax.ShapeDtypeStruct((nblk, CHUNK, tile_c, tile_c), BF16),
                   jax.ShapeDtypeStruct((nblk, pw_rows, 2 * ns), F32)],
        compiler_params=_params("parallel"),
        name="s5_prep",
    )(tiled(b_re, gb * p, c), tiled(b_im, gb * p, c), tiled(c_re, gb * c, p), tiled(c_im, gb * c, p),
      tiled(a_re, gb, p), tiled(a_im, gb, p), tiled(log_dt, gb, 1), tiled(d, gb, c))


def _s5_scan(u, bc, cc, klag, pw, *, seq):
    nblk, tok, tw = u.shape
    ns = S5_BLOCK_STATE
    rows_seq = seq // CHUNK
    rt = _tile(rows_seq, S5_ROW_TILE)
    per_tile = lambda a: pl.BlockSpec((None,) + a.shape[1:], lambda j, i: (j,) + (0,) * (a.ndim - 1))
    rows_spec = pl.BlockSpec((None, rt * CHUNK, tw), lambda j, i: (j, i, 0))
    return pl.pallas_call(
        functools.partial(_s5_scan_kernel, tiles_per_seq=rows_seq // rt),
        grid=(nblk, tok // (rt * CHUNK)),
        in_specs=[rows_spec, per_tile(bc), per_tile(cc), per_tile(klag), per_tile(pw)],
        out_specs=rows_spec,
        out_shape=jax.ShapeDtypeStruct(u.shape, BF16),
        scratch_shapes=[pltpu.VMEM((rt, 2 * ns), F32), pltpu.VMEM((8, 2 * ns), F32)],
        compiler_params=_params("parallel", "arbitrary"),
        name="s5_scan",
    )(u, bc, cc, klag, pw)


def _s5_tail(y, gs, w_glu, b_glu, w_proj):
    nblk, tok, tw = y.shape
    dm = gs.shape[1]
    tm = _tile(tok, TOKEN_TILE)
    tok_spec = lambda w: pl.BlockSpec((tm, w), lambda i: (i, 0))
    return pl.pallas_call(
        _s5_tail_kernel,
        grid=(tok // tm,),
        in_specs=[pl.BlockSpec((nblk, tm, tw), lambda i: (0, i, 0)), tok_spec(dm), _resident(w_glu.shape),
                  _resident(b_glu.shape), _resident(w_proj.shape)],
        out_specs=tok_spec(dm),
        out_shape=jax.ShapeDtypeStruct((tok, dm), BF16),
        compiler_params=_params("parallel"),
        name="s5_tail",
    )(y, gs, w_glu, b_glu, w_proj)


def _hgrn_merge(q, k, v, og, lf, gh, ms, x2, w_proj, w_out, norm_g, ffn_g, *, seq):
    tok, wh = q.shape
    dm = x2.shape[1]
    heads = wh // HEAD_DIM
    tl = _tile(seq, HGRN_TOKEN_TILE)
    lt = seq // tl
    seq_spec = lambda w: pl.BlockSpec((tl, w), lambda b, l: (b * lt + l, 0))
    return pl.pallas_call(
        functools.partial(_hgrn_kernel, heads=heads),
        grid=(tok // seq, lt),
        in_specs=[seq_spec(wh)] * 5 + [seq_spec(dm)] * 3
                 + [_resident(w_proj.shape), _resident(w_out.shape), _resident((1, wh)), _resident((1, dm))],
        out_specs=[seq_spec(dm)] * 2,
        out_shape=[jax.ShapeDtypeStruct((tok, dm), F32), jax.ShapeDtypeStruct((tok, dm), BF16)],
        scratch_shapes=[pltpu.VMEM((heads, HEAD_DIM, HEAD_DIM), F32),
                        pltpu.VMEM((tl, wh), BF16), pltpu.VMEM((tl, wh), BF16),
                        pltpu.VMEM((tl, wh), F32), pltpu.VMEM((tl, wh), F32), pltpu.VMEM((tl, wh), F32)],
        compiler_params=_params("arbitrary", "arbitrary"),
        name="hgrn_merge",
    )(q, k, v, og, lf, gh, ms, x2, w_proj, w_out, norm_g, ffn_g)


def _ffn(h2, x1, w_up, cw, cb, w_down, final_g, *, seq):
    tok, dm = h2.shape
    dff = w_down.shape[0]
    tf = _tile(dff, FFN_FF_TILE)
    nf = dff // tf
    tmf = _tile(seq, FFN_TOKEN_TILE)
    halo = 16
    return pl.pallas_call(
        functools.partial(_ffn_kernel, tiles_per_seq=seq // tmf),
        grid=(tok // tmf, nf),
        in_specs=[pl.BlockSpec((tmf, dm), lambda i, j: (i, 0)),
                  pl.BlockSpec((halo, dm), lambda i, j: (jnp.maximum(i * (tmf // halo) - 1, 0), 0)),
                  pl.BlockSpec((dm, tf), lambda i, j: (0, j)),
                  pl.BlockSpec((dm, tf), lambda i, j: (0, nf + j)),
                  pl.BlockSpec((CONV_WIDTH, tf), lambda i, j: (0, j)),
                  pl.BlockSpec((CONV_WIDTH, tf), lambda i, j: (0, nf + j)),
                  pl.BlockSpec((1, tf), lambda i, j: (0, j)),
                  pl.BlockSpec((1, tf), lambda i, j: (0, nf + j)),
                  pl.BlockSpec((tf, dm), lambda i, j: (j, 0)),
                  pl.BlockSpec((tmf, dm), lambda i, j: (i, 0)),
                  pl.BlockSpec((1, dm), lambda i, j: (0, 0))],
        out_specs=pl.BlockSpec((tmf, dm), lambda i, j: (i, 0)),
        out_shape=jax.ShapeDtypeStruct((tok, dm), F32),
        scratch_shapes=[pltpu.VMEM((halo + tmf, dm), BF16)],
        compiler_params=_params("parallel", "arbitrary"),
        name="ffn",
    )(h2, h2, w_up, w_up, cw, cw, cb, cb, w_down, x1, final_g)


def kernel(x, ln_mix_g, w_in, s5_a_re, s5_a_im, s5_log_dt, s5_b_re, s5_b_im, s5_c_re, s5_c_im,
           s5_d, s5_w_glu, s5_b_glu, w_proj_s5, hgrn_lb_logits, hgrn_norm_g, w_proj_hgrn,
           w_out, ln_ffn_g, w_up, conv_w, conv_b, w_down, ln_final_g):
    bsz, seq, dm = x.shape
    assert ln_mix_g.shape[0] == 1, "single-layer block"
    tok = bsz * seq
    ws5 = s5_w_glu.shape[1]
    wh = hgrn_norm_g.shape[1]
    assert ws5 == wh and ws5 % MXU_DIM_V7X == 0 and seq % CHUNK == 0
    assert w_in.shape[2] == ws5 + 4 * wh + 2 * dm

    x2 = x.reshape(tok, dm)
    u, q, k, v, og, lf, gs, gh = _inproj(x2, _row(ln_mix_g[0]), w_in[0].astype(BF16),
                                         hgrn_lb_logits.astype(F32), wh=wh)

    bc, cc, klag, pw = _s5_prep(s5_a_re[0], s5_a_im[0], s5_log_dt[0], s5_b_re[0], s5_b_im[0],
                                s5_c_re[0], s5_c_im[0], s5_d[0])
    y = _s5_scan(u, bc, cc, klag, pw, seq=seq)
    ms = _s5_tail(y, gs, s5_w_glu[0].astype(BF16), _row(s5_b_glu[0]), w_proj_s5[0].astype(BF16))

    x1, h2 = _hgrn_merge(q, k, v, og, lf, gh, ms, x2, w_proj_hgrn[0].astype(BF16), w_out[0].astype(BF16),
                         _row(hgrn_norm_g[0]), _row(ln_ffn_g[0]), seq=seq)

    out = _ffn(h2, x1, w_up[0].astype(BF16), conv_w[0].astype(F32), conv_b[0].astype(F32).reshape(1, -1),
               w_down[0].astype(BF16), _row(ln_final_g), seq=seq)
    return out.reshape(bsz, seq, dm)
```

```python
import functools
import math

import jax
import jax.numpy as jnp
from jax import lax
from jax.experimental import pallas as pl
from jax.experimental.pallas import tpu as pltpu

F32 = jnp.float32
BF16 = jnp.bfloat16

RMS_EPS = 1e-6
S5_MAX_RE = -1e-4
S5_GROUP = 16
S5_STATE = 64
HEAD_DIM = 128
CONV_WIDTH = 3

LANES_V7X = 128
MXU_DIM_V7X = 256
CHUNK = 16
S5_SUB = 4
S5_BLOCK_GROUPS = MXU_DIM_V7X // S5_GROUP
S5_BLOCK_STATE = S5_BLOCK_GROUPS * S5_STATE
VMEM_LIMIT_BYTES = 56 * 1024 * 1024
HGRN_FACTOR_LIMIT = 60.0
HGRN_WIDE_CHUNK = 64

TOKEN_TILE = 512
S5_ROW_TILE = 256
HGRN_TOKEN_TILE = 256
FFN_TOKEN_TILE = 512
FFN_FF_TILE = 512


def _sigmoid(y):
    return 1.0 / (1.0 + jnp.exp(-y))


def _rmsnorm(x, g):
    return x * lax.rsqrt(jnp.mean(x * x, axis=-1, keepdims=True) + RMS_EPS) * g


def _params(*sem):
    return pltpu.CompilerParams(dimension_semantics=sem, vmem_limit_bytes=VMEM_LIMIT_BYTES)


def _resident(shape):
    return pl.BlockSpec(shape, lambda *_: (0,) * len(shape), pipeline_mode=pl.Buffered(1))


def _inproj_a_kernel(x_ref, g_ref, w_ref, lbl_ref, u_ref, q_ref, k_ref, v_ref, og_ref, lf_ref, *, width):
    h = _rmsnorm(x_ref[...], g_ref[...]).astype(BF16)

    def proj(i):
        return jnp.dot(h, w_ref[:, i * width:(i + 1) * width], preferred_element_type=F32)

    a = proj(0).astype(BF16)
    for j in range(u_ref.shape[0]):
        u_ref[j] = a[:, j * MXU_DIM_V7X:(j + 1) * MXU_DIM_V7X]
    a = proj(1)
    q_ref[...] = (a * _sigmoid(a)).astype(BF16)
    lbl = lbl_ref[...]
    e = jnp.exp(lbl - jnp.max(lbl, axis=0, keepdims=True))
    lb = e[0:1, :] / jnp.sum(e, axis=0, keepdims=True)
    sg = _sigmoid(proj(2))
    lf_ref[...] = jnp.log(lb + (1.0 - lb) * sg)
    k_ref[...] = ((1.0 - lb) * (1.0 - sg)).astype(BF16)
    v_ref[...] = proj(3).astype(BF16)
    a = proj(4)
    og_ref[...] = (a * _sigmoid(a)).astype(BF16)


def _inproj_b_kernel(x_ref, g_ref, *refs, n_w, n_cast):
    w_refs, src_refs = refs[:n_w], refs[n_w:n_w + n_cast]
    o_refs, dst_refs = refs[n_w + n_cast:n_w + n_cast + 2], refs[n_w + n_cast + 2:]
    for src, dst in zip(src_refs, dst_refs):
        dst[...] = src[...].astype(BF16)
    per_out = len(w_refs) // len(o_refs)
    h = _rmsnorm(x_ref[...], g_ref[...]).astype(BF16)
    for n, w_ref in enumerate(w_refs):
        cols = w_ref.shape[1]
        lo = (n % per_out) * cols
        o_refs[n // per_out][:, lo:lo + cols] = _sigmoid(
            jnp.dot(h, w_ref[...], preferred_element_type=F32)).astype(BF16)


def _spread(x, reps):
    w = x.shape[1]
    r = lax.broadcasted_iota(jnp.int32, (w, reps * w), 0)
    c = lax.broadcasted_iota(jnp.int32, (w, reps * w), 1)
    sel = jnp.where(c % w == r, 1.0, 0.0).astype(BF16)
    out = jnp.zeros((x.shape[0], reps * w), F32)
    for piece in _split3(x):
        out = out + jnp.dot(piece, sel, preferred_element_type=F32)
    return out


def _own_group(x, row_div, lane_div):
    r = lax.broadcasted_iota(jnp.int32, x.shape, 0)
    c = lax.broadcasted_iota(jnp.int32, x.shape, 1)
    return jnp.where(r // row_div == c // lane_div, x, 0.0)


def _s5_prep_kernel(bre_ref, bim_ref, cre_ref, cim_ref, are_ref, aim_ref, ldt_ref, d_ref,
                    bc_ref, cc_ref, klag_ref, pw_ref):
    ns = S5_BLOCK_STATE
    gb, p, c = S5_BLOCK_GROUPS, S5_STATE, S5_GROUP
    b_tile = lambda ref: _own_group(_spread(ref[0], gb), p, c).T
    c_tile = lambda ref: _own_group(_spread(ref[0], gb), c, p).T
    per_lane = lambda x, w: jnp.sum(_own_group(x, 1, w), axis=0, keepdims=True)
    lam_re = jnp.minimum(per_lane(_spread(are_ref[0], gb), p), S5_MAX_RE)
    lam_im = per_lane(_spread(aim_ref[0], gb), p)
    dt = jnp.exp(per_lane(jnp.broadcast_to(ldt_ref[0], (gb, ns)), p))
    d_row = per_lane(_spread(d_ref[0], gb), c)
    mag = jnp.exp(lam_re * dt)
    ab_re = mag * jnp.cos(lam_im * dt)
    ab_im = mag * jnp.sin(lam_im * dt)
    den = lam_re * lam_re + lam_im * lam_im
    nr = ab_re - 1.0
    ni = ab_im
    cf_re = (nr * lam_re + ni * lam_im) / den
    cf_im = (ni * lam_re - nr * lam_im) / den
    bre = b_tile(bre_ref)
    bim = b_tile(bim_ref)
    b_re = cf_re * bre - cf_im * bim
    b_im = cf_re * bim + cf_im * bre
    cre = c_tile(cre_ref).astype(BF16)
    cim = c_tile(cim_ref).astype(BF16)
    cc_ref[0] = jnp.concatenate([cre, -cim], axis=0)

    p_re = jnp.ones_like(ab_re)
    p_im = jnp.zeros_like(ab_re)
    pw_ref[...] = jnp.zeros_like(pw_ref)
    for n in range(CHUNK + 1):
        pw_ref[0, n:n + 1, :] = jnp.concatenate([p_re, p_im], axis=-1)
        if n < S5_SUB:
            bc_ref[0, S5_SUB - 1 - n] = jnp.concatenate(
                [b_re * p_re - b_im * p_im, b_re * p_im + b_im * p_re], axis=-1).astype(BF16)
        if n < CHUNK:
            t_re = (b_re * p_re - b_im * p_im).astype(BF16)
            t_im = (b_re * p_im + b_im * p_re).astype(BF16)
            tap = (jnp.dot(t_re, cre, preferred_element_type=F32)
                   - jnp.dot(t_im, cim, preferred_element_type=F32))
            if n == 0:
                r = lax.broadcasted_iota(jnp.int32, tap.shape, 0)
                c = lax.broadcasted_iota(jnp.int32, tap.shape, 1)
                tap = tap + jnp.where(r == c, d_row, 0.0)
            klag_ref[0, n] = tap.astype(BF16)
        p_re, p_im = p_re * ab_re - p_im * ab_im, p_re * ab_im + p_im * ab_re


def _s5_scan_kernel(u_ref, bc_ref, cc_ref, klag_ref, pw_ref, y_ref, z_ref, carry_ref, *, tiles_per_seq):
    ns = S5_BLOCK_STATE
    tw = MXU_DIM_V7X
    rt = u_ref.shape[0] // CHUNK
    by_pos = jnp.swapaxes(u_ref[...].reshape(rt, CHUNK, tw), 0, 1)

    def u_tile(s):
        return by_pos[s]

    def pw(n):
        return pw_ref[n:n + 1, 0:ns], pw_ref[n:n + 1, ns:2 * ns]

    @pl.when(pl.program_id(1) % tiles_per_seq == 0)
    def _():
        carry_ref[...] = jnp.zeros_like(carry_ref)

    acc_re = jnp.zeros((rt, ns), F32)
    acc_im = jnp.zeros((rt, ns), F32)
    for sub in range(CHUNK // S5_SUB):
        bb = jnp.dot(u_tile(sub * S5_SUB), bc_ref[0], preferred_element_type=F32)
        for r in range(1, S5_SUB):
            bb = bb + jnp.dot(u_tile(sub * S5_SUB + r), bc_ref[r], preferred_element_type=F32)
        p_re, p_im = pw(CHUNK - S5_SUB * (sub + 1))
        acc_re = acc_re + bb[:, :ns] * p_re - bb[:, ns:] * p_im
        acc_im = acc_im + bb[:, :ns] * p_im + bb[:, ns:] * p_re
    z_ref[:, 0:ns] = acc_re
    z_ref[:, ns:2 * ns] = acc_im

    a_re, a_im = pw(CHUNK)

    def body(r, carry):
        s_re, s_im = carry
        z_re = z_ref[pl.ds(r, 1), 0:ns]
        z_im = z_ref[pl.ds(r, 1), ns:2 * ns]
        z_ref[pl.ds(r, 1), 0:ns] = s_re
        z_ref[pl.ds(r, 1), ns:2 * ns] = s_im
        return (a_re * s_re - a_im * s_im + z_re, a_re * s_im + a_im * s_re + z_im)

    s_re, s_im = lax.fori_loop(0, rt, body, (carry_ref[0:1, 0:ns], carry_ref[0:1, ns:2 * ns]))
    carry_ref[0:1, 0:ns] = s_re
    carry_ref[0:1, ns:2 * ns] = s_im

    s_re = z_ref[:, 0:ns]
    s_im = z_ref[:, ns:2 * ns]
    ys = []
    for t in range(CHUNK):
        y = jnp.dot(u_tile(t), klag_ref[0], preferred_element_type=F32)
        for s in range(t):
            y = y + jnp.dot(u_tile(s), klag_ref[t - s], preferred_element_type=F32)
        p_re, p_im = pw(t + 1)
        rot = jnp.concatenate([s_re * p_re - s_im * p_im, s_re * p_im + s_im * p_re], axis=-1)
        y = y + jnp.dot(rot.astype(BF16), cc_ref[...], preferred_element_type=F32)
        ys.append(y.astype(BF16))
    y_ref[...] = jnp.swapaxes(jnp.stack(ys, axis=0), 0, 1).reshape(rt * CHUNK, tw)


def _s5_tail_kernel(y_ref, gs_ref, wg_ref, bg_ref, wp_ref, o_ref):
    y = jnp.concatenate([y_ref[j] for j in range(y_ref.shape[0])], axis=-1).astype(F32)
    z = 0.5 * y * (1.0 + jnp.tanh(math.sqrt(2.0 / math.pi) * (y + 0.044715 * (y * y * y))))
    gl = jnp.dot(z.astype(BF16), wg_ref[...], preferred_element_type=F32) + bg_ref[...]
    zz = (z * _sigmoid(gl)).astype(BF16)
    ys = jnp.dot(zz, wp_ref[...], preferred_element_type=F32)
    o_ref[...] = (gs_ref[...].astype(F32) * ys).astype(BF16)


def _split3(x):
    hi = x.astype(BF16)
    r = x - hi.astype(F32)
    mid = r.astype(BF16)
    lo = (r - mid.astype(F32)).astype(BF16)
    return hi, mid, lo


def _hgrn_kernel(q_ref, k_ref, v_ref, og_ref, lf_ref, gh_ref, ms_ref, x_ref,
                 wph_ref, wout_ref, ng_ref, g2_ref, x1_ref, h2_ref,
                 st_ref, qt_ref, kh_ref, b_ref, gt_ref, o_ref, *, heads):
    tl = q_ref.shape[0]
    hd = HEAD_DIM

    @pl.when(pl.program_id(1) == 0)
    def _():
        st_ref[...] = jnp.zeros_like(st_ref)

    r = lax.broadcasted_iota(jnp.int32, (tl, tl), 0)
    c = lax.broadcasted_iota(jnp.int32, (tl, tl), 1)
    pieces = _split3(lf_ref[...])

    def chunk_sums(chunk):
        causal = jnp.logical_and((r // chunk) == (c // chunk), c <= r)
        tri = jnp.where(causal, 1.0, 0.0).astype(BF16)
        b = jnp.zeros((tl, heads * hd), F32)
        for piece in pieces:
            b = b + jnp.dot(tri, piece, preferred_element_type=F32)
        last = b.reshape(tl // chunk, chunk, heads * hd)[:, chunk - 1:chunk, :]
        bt = jnp.broadcast_to(last, (tl // chunk, chunk, heads * hd)).reshape(tl, heads * hd)
        return causal, b, bt

    def store_decayed(b, bt):
        b_ref[...] = b
        gt_ref[...] = jnp.exp(bt)
        qt_ref[...] = (q_ref[...].astype(F32) * jnp.exp(b)).astype(BF16)
        kh_ref[...] = (k_ref[...].astype(F32) * jnp.exp(bt - b)).astype(BF16)

    half = CHUNK // 2
    row = lax.broadcasted_iota(jnp.int32, (half, 1), 0)

    def pairs_exact(rows, lanes):
        q16 = q_ref[rows, lanes].astype(F32)
        k16 = k_ref[rows, lanes].astype(F32)
        v16 = v_ref[rows, lanes].astype(F32)
        b16 = b_ref[rows, lanes]
        q_top, q_bot = q16[:half], q16[half:]
        b_top, b_bot = b16[:half], b16[half:]
        acc_top = jnp.zeros((half, hd), F32)
        acc_bot = jnp.zeros((half, hd), F32)
        for s in range(CHUNK):
            bs = b16[s:s + 1]
            ks = k16[s:s + 1]
            vs = v16[s:s + 1]
            if s < half:
                w = jnp.sum(q_top * ks * jnp.exp(jnp.minimum(b_top - bs, 0.0)), axis=-1, keepdims=True)
                acc_top = acc_top + jnp.where(row >= s, w, 0.0) * vs
                w = jnp.sum(q_bot * ks * jnp.exp(b_bot - bs), axis=-1, keepdims=True)
                acc_bot = acc_bot + w * vs
            else:
                w = jnp.sum(q_bot * ks * jnp.exp(jnp.minimum(b_bot - bs, 0.0)), axis=-1, keepdims=True)
                acc_bot = acc_bot + jnp.where(row + half >= s, w, 0.0) * vs
        return jnp.concatenate([acc_top, acc_bot], axis=0)

    def carry_state(rows, first_row, h):
        lanes = slice(h * hd, (h + 1) * hd)
        st = st_ref[h]
        inter = lax.dot_general(qt_ref[rows, lanes], st.astype(BF16),
                                (((1,), (1,)), ((), ())), preferred_element_type=F32)
        upd = lax.dot_general(v_ref[rows, lanes], kh_ref[rows, lanes],
                              (((0,), (0,)), ((), ())), preferred_element_type=F32)
        st_ref[h] = gt_ref[first_row, lanes] * st + upd
        return inter

    def exact_step(n, _):
        r0 = pl.multiple_of(n * CHUNK, CHUNK)
        rows = pl.ds(r0, CHUNK)
        for h in range(heads):
            lanes = slice(h * hd, (h + 1) * hd)
            o_ref[rows, lanes] = carry_state(rows, pl.ds(r0, 1), h) + pairs_exact(rows, lanes)
        return 0

    causal, b, bt = chunk_sums(HGRN_WIDE_CHUNK)
    worst = jnp.max(jnp.max(-bt, axis=0, keepdims=True), axis=1, keepdims=True)[0, 0]
    factorable = worst <= HGRN_FACTOR_LIMIT

    @pl.when(factorable)
    def _():
        store_decayed(b, bt)
        kx = (k_ref[...].astype(F32) * jnp.exp(-b)).astype(BF16)
        for h in range(heads):
            lanes = slice(h * hd, (h + 1) * hd)
            sc = lax.dot_general(qt_ref[:, lanes], kx[:, lanes], (((1,), (1,)), ((), ())),
                                 preferred_element_type=F32)
            o_ref[:, lanes] = jnp.dot(jnp.where(causal, sc, 0.0).astype(BF16), v_ref[:, lanes],
                                      preferred_element_type=F32)
        for n in range(tl // HGRN_WIDE_CHUNK):
            rows = slice(n * HGRN_WIDE_CHUNK, (n + 1) * HGRN_WIDE_CHUNK)
            for h in range(heads):
                lanes = slice(h * hd, (h + 1) * hd)
                o_ref[rows, lanes] += carry_state(rows, slice(rows.start, rows.start + 1), h)

    @pl.when(jnp.logical_not(factorable))
    def _():
        store_decayed(*chunk_sums(CHUNK)[1:])
        lax.fori_loop(0, tl // CHUNK, exact_step, 0)

    parts = []
    for h in range(heads):
        lanes = slice(h * hd, (h + 1) * hd)
        o = o_ref[:, lanes]
        o = o * lax.rsqrt(jnp.mean(o * o, axis=-1, keepdims=True) + RMS_EPS)
        parts.append((o * ng_ref[:, lanes] * og_ref[:, lanes].astype(F32)).astype(BF16))
    on = jnp.concatenate(parts, axis=-1)
    yh = jnp.dot(on, wph_ref[...], preferred_element_type=F32)
    merged = ms_ref[...].astype(F32) + gh_ref[...].astype(F32) * yh
    x1 = x_ref[...] + jnp.dot(merged.astype(BF16), wout_ref[...], preferred_element_type=F32)
    x1_ref[...] = x1
    h2_ref[...] = _rmsnorm(x1, g2_ref[...]).astype(BF16)


def _ffn_kernel(h_ref, halo_ref, wug_ref, wuv_ref, cwg_ref, cwv_ref, cbg_ref, cbv_ref, wd_ref,
                x1_ref, gf_ref, o_ref, hcat_ref, *, tiles_per_seq):
    i = pl.program_id(0)
    j = pl.program_id(1)
    tm = h_ref.shape[0]
    nh = halo_ref.shape[0]

    @pl.when(j == 0)
    def _():
        keep = jnp.where(i % tiles_per_seq == 0, 0.0, 1.0).astype(BF16)
        hcat_ref[0:nh] = halo_ref[...] * keep
        hcat_ref[nh:nh + tm] = h_ref[...]
        o_ref[...] = jnp.zeros_like(o_ref)

    def conv_up(w_ref, cw_ref, cb_ref):
        ext = jnp.dot(hcat_ref[...], w_ref[...], preferred_element_type=F32)
        out = cb_ref[...]
        for tap in range(CONV_WIDTH):
            back = CONV_WIDTH - 1 - tap
            out = out + cw_ref[tap:tap + 1, :] * ext[nh - back:nh - back + tm]
        return out

    gate = conv_up(wug_ref, cwg_ref, cbg_ref)
    val = conv_up(wuv_ref, cwv_ref, cbv_ref)
    act = (0.5 * gate * (1.0 + jnp.tanh(0.5 * gate)) * val).astype(BF16)
    o_ref[...] += jnp.dot(act, wd_ref[...], preferred_element_type=F32)

    @pl.when(j == pl.num_programs(1) - 1)
    def _():
        o_ref[...] = _rmsnorm(x1_ref[...] + o_ref[...], gf_ref[...])


def _tile(n, want):
    t = min(n, want)
    assert n % t == 0, (n, want)
    return t


def _row(v):
    return v.reshape(1, -1).astype(F32)


def _inproj(x2, g, w_in, lb_logits, to_cast, *, wh):
    tok, dm = x2.shape
    tm = _tile(tok, TOKEN_TILE)
    nblk = wh // MXU_DIM_V7X
    n_a = 5 * wh
    assert w_in.shape[1] == n_a + 2 * dm and dm % wh == 0
    tok_spec = lambda w: pl.BlockSpec((tm, w), lambda i: (i, 0))
    cols = lambda n, width: pl.BlockSpec((dm, width), lambda i: (0, n), pipeline_mode=pl.Buffered(1))
    u, q, k, v, og, lf = pl.pallas_call(
        functools.partial(_inproj_a_kernel, width=wh),
        grid=(tok // tm,),
        in_specs=[tok_spec(dm), _resident((1, dm)), cols(0, n_a), _resident(lb_logits.shape)],
        out_specs=[pl.BlockSpec((nblk, tm, MXU_DIM_V7X), lambda i: (0, i, 0))] + [tok_spec(wh)] * 5,
        out_shape=[jax.ShapeDtypeStruct((nblk, tok, MXU_DIM_V7X), BF16)]
                  + [jax.ShapeDtypeStruct((tok, wh), BF16)] * 4 + [jax.ShapeDtypeStruct((tok, wh), F32)],
        compiler_params=_params("parallel"),
        name="inproj_a",
    )(x2, g, w_in, lb_logits)
    gate_blocks = 2 * dm // wh
    steps = tok // tm
    bf16_rows = 16
    assert all(w.shape[0] % (steps * bf16_rows) == 0 for w in to_cast)
    row_blocks = [pl.BlockSpec((w.shape[0] // steps, w.shape[1]), lambda i: (i, 0)) for w in to_cast]
    gs, gh, *casted = pl.pallas_call(
        functools.partial(_inproj_b_kernel, n_w=gate_blocks, n_cast=len(to_cast)),
        grid=(steps,),
        in_specs=[tok_spec(dm), _resident((1, dm))] + [cols(n_a // wh + n, wh) for n in range(gate_blocks)]
                 + row_blocks,
        out_specs=[tok_spec(dm)] * 2 + row_blocks,
        out_shape=[jax.ShapeDtypeStruct((tok, dm), BF16)] * 2
                  + [jax.ShapeDtypeStruct(w.shape, BF16) for w in to_cast],
        compiler_params=_params("parallel"),
        name="inproj_b",
    )(x2, g, *([w_in] * gate_blocks), *to_cast)
    return u, q, k, v, og, lf, gs, gh, casted


def _s5_prep(a_re, a_im, log_dt, b_re, b_im, c_re, c_im, d):
    groups = a_re.shape[0]
    nblk = groups // S5_BLOCK_GROUPS
    ns = S5_BLOCK_STATE
    tile_c = S5_BLOCK_GROUPS * S5_GROUP
    blk3 = lambda r, c: pl.BlockSpec((1, r, c), lambda j: (j, 0, 0))
    pw_rows = 24
    gb, p, c = S5_BLOCK_GROUPS, S5_STATE, S5_GROUP
    tiled = lambda t, r, w: t.astype(F32).reshape(nblk, r, w)
    return pl.pallas_call(
        _s5_prep_kernel,
        grid=(nblk,),
        in_specs=[blk3(gb * p, c), blk3(gb * p, c), blk3(gb * c, p), blk3(gb * c, p),
                  blk3(gb, p), blk3(gb, p), blk3(gb, 1), blk3(gb, c)],
        out_specs=[pl.BlockSpec((1, S5_SUB, tile_c, 2 * ns), lambda j: (j, 0, 0, 0)), blk3(2 * ns, tile_c),
                   pl.BlockSpec((1, CHUNK, tile_c, tile_c), lambda j: (j, 0, 0, 0)), blk3(pw_rows, 2 * ns)],
        out_shape=[jax.ShapeDtypeStruct((nblk, S5_SUB, tile_c, 2 * ns), BF16),
                   jax.ShapeDtypeStruct((nblk, 2 * ns, tile_c), BF16),
                   jax.ShapeDtypeStruct((nblk, CHUNK, tile_c, tile_c), BF16),
                   jax.ShapeDtypeStruct((nblk, pw_rows, 2 * ns), F32)],
        compiler_params=_params("parallel"),
        name="s5_prep",
    )(tiled(b_re, gb * p, c), tiled(b_im, gb * p, c), tiled(c_re, gb * c, p), tiled(c_im, gb * c, p),
      tiled(a_re, gb, p), tiled(a_im, gb, p), tiled(log_dt, gb, 1), tiled(d, gb, c))


def _s5_scan(u, bc, cc, klag, pw, *, seq):
    nblk, tok, tw = u.shape
    ns = S5_BLOCK_STATE
    rows_seq = seq // CHUNK
    rt = _tile(rows_seq, S5_ROW_TILE)
    per_tile = lambda a: pl.BlockSpec((None,) + a.shape[1:], lambda j, i: (j,) + (0,) * (a.ndim - 1))
    rows_spec = pl.BlockSpec((None, rt * CHUNK, tw), lambda j, i: (j, i, 0))
    return pl.pallas_call(
        functools.partial(_s5_scan_kernel, tiles_per_seq=rows_seq // rt),
        grid=(nblk, tok // (rt * CHUNK)),
        in_specs=[rows_spec, per_tile(bc), per_tile(cc), per_tile(klag), per_tile(pw)],
        out_specs=rows_spec,
        out_shape=jax.ShapeDtypeStruct(u.shape, BF16),
        scratch_shapes=[pltpu.VMEM((rt, 2 * ns), F32), pltpu.VMEM((8, 2 * ns), F32)],
        compiler_params=_params("parallel", "arbitrary"),
        name="s5_scan",
    )(u, bc, cc, klag, pw)


def _s5_tail(y, gs, w_glu, b_glu, w_proj):
    nblk, tok, tw = y.shape
    dm = gs.shape[1]
    tm = _tile(tok, TOKEN_TILE)
    tok_spec = lambda w: pl.BlockSpec((tm, w), lambda i: (i, 0))
    return pl.pallas_call(
        _s5_tail_kernel,
        grid=(tok // tm,),
        in_specs=[pl.BlockSpec((nblk, tm, tw), lambda i: (0, i, 0)), tok_spec(dm), _resident(w_glu.shape),
                  _resident(b_glu.shape), _resident(w_proj.shape)],
        out_specs=tok_spec(dm),
        out_shape=jax.ShapeDtypeStruct((tok, dm), BF16),
        compiler_params=_params("parallel"),
        name="s5_tail",
    )(y, gs, w_glu, b_glu, w_proj)


def _hgrn_merge(q, k, v, og, lf, gh, ms, x2, w_proj, w_out, norm_g, ffn_g, *, seq):
    tok, wh = q.shape
    dm = x2.shape[1]
    heads = wh // HEAD_DIM
    tl = _tile(seq, HGRN_TOKEN_TILE)
    lt = seq // tl
    seq_spec = lambda w: pl.BlockSpec((tl, w), lambda b, l: (b * lt + l, 0))
    return pl.pallas_call(
        functools.partial(_hgrn_kernel, heads=heads),
        grid=(tok // seq, lt),
        in_specs=[seq_spec(wh)] * 5 + [seq_spec(dm)] * 3
                 + [_resident(w_proj.shape), _resident(w_out.shape), _resident((1, wh)), _resident((1, dm))],
        out_specs=[seq_spec(dm)] * 2,
        out_shape=[jax.ShapeDtypeStruct((tok, dm), F32), jax.ShapeDtypeStruct((tok, dm), BF16)],
        scratch_shapes=[pltpu.VMEM((heads, HEAD_DIM, HEAD_DIM), F32),
                        pltpu.VMEM((tl, wh), BF16), pltpu.VMEM((tl, wh), BF16),
                        pltpu.VMEM((tl, wh), F32), pltpu.VMEM((tl, wh), F32), pltpu.VMEM((tl, wh), F32)],
        compiler_params=_params("arbitrary", "arbitrary"),
        name="hgrn_merge",
    )(q, k, v, og, lf, gh, ms, x2, w_proj, w_out, norm_g, ffn_g)


def _ffn(h2, x1, w_up, cw, cb, w_down, final_g, *, seq):
    tok, dm = h2.shape
    dff = w_down.shape[0]
    tf = _tile(dff, FFN_FF_TILE)
    nf = dff // tf
    tmf = _tile(seq, FFN_TOKEN_TILE)
    halo = 16
    return pl.pallas_call(
        functools.partial(_ffn_kernel, tiles_per_seq=seq // tmf),
        grid=(tok // tmf, nf),
        in_specs=[pl.BlockSpec((tmf, dm), lambda i, j: (i, 0)),
                  pl.BlockSpec((halo, dm), lambda i, j: (jnp.maximum(i * (tmf // halo) - 1, 0), 0)),
                  pl.BlockSpec((dm, tf), lambda i, j: (0, j)),
                  pl.BlockSpec((dm, tf), lambda i, j: (0, nf + j)),
                  pl.BlockSpec((CONV_WIDTH, tf), lambda i, j: (0, j)),
                  pl.BlockSpec((CONV_WIDTH, tf), lambda i, j: (0, nf + j)),
                  pl.BlockSpec((1, tf), lambda i, j: (0, j)),
                  pl.BlockSpec((1, tf), lambda i, j: (0, nf + j)),
                  pl.BlockSpec((tf, dm), lambda i, j: (j, 0)),
                  pl.BlockSpec((tmf, dm), lambda i, j: (i, 0)),
                  pl.BlockSpec((1, dm), lambda i, j: (0, 0))],
        out_specs=pl.BlockSpec((tmf, dm), lambda i, j: (i, 0)),
        out_shape=jax.ShapeDtypeStruct((tok, dm), F32),
        scratch_shapes=[pltpu.VMEM((halo + tmf, dm), BF16)],
        compiler_params=_params("parallel", "arbitrary"),
        name="ffn",
    )(h2, h2, w_up, w_up, cw, cw, cb, cb, w_down, x1, final_g)


def kernel(x, ln_mix_g, w_in, s5_a_re, s5_a_im, s5_log_dt, s5_b_re, s5_b_im, s5_c_re, s5_c_im,
           s5_d, s5_w_glu, s5_b_glu, w_proj_s5, hgrn_lb_logits, hgrn_norm_g, w_proj_hgrn,
           w_out, ln_ffn_g, w_up, conv_w, conv_b, w_down, ln_final_g):
    bsz, seq, dm = x.shape
    assert ln_mix_g.shape[0] == 1, "single-layer block"
    tok = bsz * seq
    ws5 = s5_w_glu.shape[1]
    wh = hgrn_norm_g.shape[1]
    assert ws5 == wh and ws5 % MXU_DIM_V7X == 0 and seq % CHUNK == 0
    assert w_in.shape[2] == ws5 + 4 * wh + 2 * dm

    x2 = x.reshape(tok, dm)
    later_weights = [w.astype(F32) for w in (s5_w_glu[0], w_proj_s5[0], w_proj_hgrn[0], w_out[0], w_up[0], w_down[0])]
    u, q, k, v, og, lf, gs, gh, (w_glu_b, w_ps5_b, w_ph_b, w_out_b, w_up_b, w_down_b) = _inproj(
        x2, _row(ln_mix_g[0]), w_in[0].astype(BF16), hgrn_lb_logits.astype(F32), later_weights, wh=wh)

    bc, cc, klag, pw = _s5_prep(s5_a_re[0], s5_a_im[0], s5_log_dt[0], s5_b_re[0], s5_b_im[0],
                                s5_c_re[0], s5_c_im[0], s5_d[0])
    y = _s5_scan(u, bc, cc, klag, pw, seq=seq)
    ms = _s5_tail(y, gs, w_glu_b, _row(s5_b_glu[0]), w_ps5_b)

    x1, h2 = _hgrn_merge(q, k, v, og, lf, gh, ms, x2, w_ph_b, w_out_b,
                         _row(hgrn_norm_g[0]), _row(ln_ffn_g[0]), seq=seq)

    out = _ffn(h2, x1, w_up_b, conv_w[0].astype(F32), conv_b[0].astype(F32).reshape(1, -1),
               w_down_b, _row(ln_final_g), seq=seq)
    return out.reshape(bsz, seq, dm)
```

```python
import functools
import math

import jax
import jax.numpy as jnp
from jax import lax
from jax.experimental import pallas as pl
from jax.experimental.pallas import tpu as pltpu

F32 = jnp.float32
BF16 = jnp.bfloat16

RMS_EPS = 1e-6
S5_MAX_RE = -1e-4
S5_GROUP = 16
S5_STATE = 64
HEAD_DIM = 128
CONV_WIDTH = 3

LANES_V7X = 128
MXU_DIM_V7X = 256
CHUNK = 16
S5_SUB = 4
S5_BLOCK_GROUPS = MXU_DIM_V7X // S5_GROUP
S5_BLOCK_STATE = S5_BLOCK_GROUPS * S5_STATE
VMEM_LIMIT_BYTES = 56 * 1024 * 1024
FFN_VMEM_LIMIT_BYTES = 62 * 1024 * 1024
HGRN_FACTOR_LIMIT = 60.0
HGRN_WIDE_CHUNK = 64

TOKEN_TILE = 512
S5_ROW_TILE = 256
HGRN_TOKEN_TILE = 256
FFN_TOKEN_TILE = 1024
FFN_FF_TILE = 512


def _sigmoid(y):
    return 1.0 / (1.0 + jnp.exp(-y))


def _rmsnorm(x, g):
    return x * lax.rsqrt(jnp.mean(x * x, axis=-1, keepdims=True) + RMS_EPS) * g


def _params(*sem):
    return pltpu.CompilerParams(dimension_semantics=sem, vmem_limit_bytes=VMEM_LIMIT_BYTES)


def _resident(shape):
    return pl.BlockSpec(shape, lambda *_: (0,) * len(shape), pipeline_mode=pl.Buffered(1))


def _inproj_a_kernel(x_ref, g_ref, w_ref, lbl_ref, u_ref, q_ref, k_ref, v_ref, og_ref, lf_ref, *, width):
    h = _rmsnorm(x_ref[...], g_ref[...]).astype(BF16)

    def proj(i):
        return jnp.dot(h, w_ref[:, i * width:(i + 1) * width], preferred_element_type=F32)

    a = proj(0).astype(BF16)
    for j in range(u_ref.shape[0]):
        u_ref[j] = a[:, j * MXU_DIM_V7X:(j + 1) * MXU_DIM_V7X]
    a = proj(1)
    q_ref[...] = (a * _sigmoid(a)).astype(BF16)
    lbl = lbl_ref[...]
    e = jnp.exp(lbl - jnp.max(lbl, axis=0, keepdims=True))
    lb = e[0:1, :] / jnp.sum(e, axis=0, keepdims=True)
    sg = _sigmoid(proj(2))
    lf_ref[...] = jnp.log(lb + (1.0 - lb) * sg)
    k_ref[...] = ((1.0 - lb) * (1.0 - sg)).astype(BF16)
    v_ref[...] = proj(3).astype(BF16)
    a = proj(4)
    og_ref[...] = (a * _sigmoid(a)).astype(BF16)


def _inproj_b_kernel(x_ref, g_ref, *refs, n_w, n_cast):
    w_refs, src_refs = refs[:n_w], refs[n_w:n_w + n_cast]
    o_refs, dst_refs = refs[n_w + n_cast:n_w + n_cast + 2], refs[n_w + n_cast + 2:]
    for src, dst in zip(src_refs, dst_refs):
        dst[...] = src[...].astype(BF16)
    per_out = len(w_refs) // len(o_refs)
    h = _rmsnorm(x_ref[...], g_ref[...]).astype(BF16)
    for n, w_ref in enumerate(w_refs):
        cols = w_ref.shape[1]
        lo = (n % per_out) * cols
        o_refs[n // per_out][:, lo:lo + cols] = _sigmoid(
            jnp.dot(h, w_ref[...], preferred_element_type=F32)).astype(BF16)


def _spread(x, reps):
    w = x.shape[1]
    r = lax.broadcasted_iota(jnp.int32, (w, reps * w), 0)
    c = lax.broadcasted_iota(jnp.int32, (w, reps * w), 1)
    sel = jnp.where(c % w == r, 1.0, 0.0).astype(BF16)
    out = jnp.zeros((x.shape[0], reps * w), F32)
    for piece in _split3(x):
        out = out + jnp.dot(piece, sel, preferred_element_type=F32)
    return out


def _own_group(x, row_div, lane_div):
    r = lax.broadcasted_iota(jnp.int32, x.shape, 0)
    c = lax.broadcasted_iota(jnp.int32, x.shape, 1)
    return jnp.where(r // row_div == c // lane_div, x, 0.0)


def _s5_prep_kernel(bre_ref, bim_ref, cre_ref, cim_ref, are_ref, aim_ref, ldt_ref, d_ref,
                    bc_ref, cc_ref, klag_ref, pw_ref):
    ns = S5_BLOCK_STATE
    gb, p, c = S5_BLOCK_GROUPS, S5_STATE, S5_GROUP
    b_tile = lambda ref: _own_group(_spread(ref[0], gb), p, c).T
    c_tile = lambda ref: _own_group(_spread(ref[0], gb), c, p).T
    per_lane = lambda x, w: jnp.sum(_own_group(x, 1, w), axis=0, keepdims=True)
    lam_re = jnp.minimum(per_lane(_spread(are_ref[0], gb), p), S5_MAX_RE)
    lam_im = per_lane(_spread(aim_ref[0], gb), p)
    dt = jnp.exp(per_lane(jnp.broadcast_to(ldt_ref[0], (gb, ns)), p))
    d_row = per_lane(_spread(d_ref[0], gb), c)
    mag = jnp.exp(lam_re * dt)
    ab_re = mag * jnp.cos(lam_im * dt)
    ab_im = mag * jnp.sin(lam_im * dt)
    den = lam_re * lam_re + lam_im * lam_im
    nr = ab_re - 1.0
    ni = ab_im
    cf_re = (nr * lam_re + ni * lam_im) / den
    cf_im = (ni * lam_re - nr * lam_im) / den
    bre = b_tile(bre_ref)
    bim = b_tile(bim_ref)
    b_re = cf_re * bre - cf_im * bim
    b_im = cf_re * bim + cf_im * bre
    cre = c_tile(cre_ref).astype(BF16)
    cim = c_tile(cim_ref).astype(BF16)
    cc_ref[0] = jnp.concatenate([cre, -cim], axis=0)

    p_re = jnp.ones_like(ab_re)
    p_im = jnp.zeros_like(ab_re)
    pw_ref[...] = jnp.zeros_like(pw_ref)
    for n in range(CHUNK + 1):
        pw_ref[0, n:n + 1, :] = jnp.concatenate([p_re, p_im], axis=-1)
        if n < S5_SUB:
            bc_ref[0, S5_SUB - 1 - n] = jnp.concatenate(
                [b_re * p_re - b_im * p_im, b_re * p_im + b_im * p_re], axis=-1).astype(BF16)
        if n < CHUNK:
            t_re = (b_re * p_re - b_im * p_im).astype(BF16)
            t_im = (b_re * p_im + b_im * p_re).astype(BF16)
            tap = (jnp.dot(t_re, cre, preferred_element_type=F32)
                   - jnp.dot(t_im, cim, preferred_element_type=F32))
            if n == 0:
                r = lax.broadcasted_iota(jnp.int32, tap.shape, 0)
                c = lax.broadcasted_iota(jnp.int32, tap.shape, 1)
                tap = tap + jnp.where(r == c, d_row, 0.0)
            klag_ref[0, n] = tap.astype(BF16)
        p_re, p_im = p_re * ab_re - p_im * ab_im, p_re * ab_im + p_im * ab_re


def _s5_scan_kernel(u_ref, bc_ref, cc_ref, klag_ref, pw_ref, y_ref, z_ref, carry_ref, *, tiles_per_seq):
    ns = S5_BLOCK_STATE
    tw = MXU_DIM_V7X
    rt = u_ref.shape[0] // CHUNK
    by_pos = jnp.swapaxes(u_ref[...].reshape(rt, CHUNK, tw), 0, 1)

    def u_tile(s):
        return by_pos[s]

    def pw(n):
        return pw_ref[n:n + 1, 0:ns], pw_ref[n:n + 1, ns:2 * ns]

    @pl.when(pl.program_id(1) % tiles_per_seq == 0)
    def _():
        carry_ref[...] = jnp.zeros_like(carry_ref)

    acc_re = jnp.zeros((rt, ns), F32)
    acc_im = jnp.zeros((rt, ns), F32)
    for sub in range(CHUNK // S5_SUB):
        bb = jnp.dot(u_tile(sub * S5_SUB), bc_ref[0], preferred_element_type=F32)
        for r in range(1, S5_SUB):
            bb = bb + jnp.dot(u_tile(sub * S5_SUB + r), bc_ref[r], preferred_element_type=F32)
        p_re, p_im = pw(CHUNK - S5_SUB * (sub + 1))
        acc_re = acc_re + bb[:, :ns] * p_re - bb[:, ns:] * p_im
        acc_im = acc_im + bb[:, :ns] * p_im + bb[:, ns:] * p_re
    z_ref[:, 0:ns] = acc_re
    z_ref[:, ns:2 * ns] = acc_im

    a_re, a_im = pw(CHUNK)

    def body(r, carry):
        s_re, s_im = carry
        z_re = z_ref[pl.ds(r, 1), 0:ns]
        z_im = z_ref[pl.ds(r, 1), ns:2 * ns]
        z_ref[pl.ds(r, 1), 0:ns] = s_re
        z_ref[pl.ds(r, 1), ns:2 * ns] = s_im
        return (a_re * s_re - a_im * s_im + z_re, a_re * s_im + a_im * s_re + z_im)

    s_re, s_im = lax.fori_loop(0, rt, body, (carry_ref[0:1, 0:ns], carry_ref[0:1, ns:2 * ns]))
    carry_ref[0:1, 0:ns] = s_re
    carry_ref[0:1, ns:2 * ns] = s_im

    s_re = z_ref[:, 0:ns]
    s_im = z_ref[:, ns:2 * ns]
    ys = []
    for t in range(CHUNK):
        y = jnp.dot(u_tile(t), klag_ref[0], preferred_element_type=F32)
        for s in range(t):
            y = y + jnp.dot(u_tile(s), klag_ref[t - s], preferred_element_type=F32)
        p_re, p_im = pw(t + 1)
        rot = jnp.concatenate([s_re * p_re - s_im * p_im, s_re * p_im + s_im * p_re], axis=-1)
        y = y + jnp.dot(rot.astype(BF16), cc_ref[...], preferred_element_type=F32)
        ys.append(y.astype(BF16))
    y_ref[...] = jnp.swapaxes(jnp.stack(ys, axis=0), 0, 1).reshape(rt * CHUNK, tw)


def _s5_tail_kernel(y_ref, gs_ref, wg_ref, bg_ref, wp_ref, o_ref):
    y = jnp.concatenate([y_ref[j] for j in range(y_ref.shape[0])], axis=-1).astype(F32)
    z = 0.5 * y * (1.0 + jnp.tanh(math.sqrt(2.0 / math.pi) * (y + 0.044715 * (y * y * y))))
    gl = jnp.dot(z.astype(BF16), wg_ref[...], preferred_element_type=F32) + bg_ref[...]
    zz = (z * _sigmoid(gl)).astype(BF16)
    ys = jnp.dot(zz, wp_ref[...], preferred_element_type=F32)
    o_ref[...] = (gs_ref[...].astype(F32) * ys).astype(BF16)


def _split3(x):
    hi = x.astype(BF16)
    r = x - hi.astype(F32)
    mid = r.astype(BF16)
    lo = (r - mid.astype(F32)).astype(BF16)
    return hi, mid, lo


def _hgrn_kernel(q_ref, k_ref, v_ref, og_ref, lf_ref, gh_ref, ms_ref, x_ref,
                 wph_ref, wout_ref, ng_ref, g2_ref, x1_ref, h2_ref,
                 st_ref, qt_ref, kh_ref, b_ref, gt_ref, o_ref, *, heads):
    tl = q_ref.shape[0]
    hd = HEAD_DIM

    @pl.when(pl.program_id(1) == 0)
    def _():
        st_ref[...] = jnp.zeros_like(st_ref)

    r = lax.broadcasted_iota(jnp.int32, (tl, tl), 0)
    c = lax.broadcasted_iota(jnp.int32, (tl, tl), 1)
    pieces = _split3(lf_ref[...])

    def chunk_sums(chunk):
        causal = jnp.logical_and((r // chunk) == (c // chunk), c <= r)
        tri = jnp.where(causal, 1.0, 0.0).astype(BF16)
        b = jnp.zeros((tl, heads * hd), F32)
        for piece in pieces:
            b = b + jnp.dot(tri, piece, preferred_element_type=F32)
        last = b.reshape(tl // chunk, chunk, heads * hd)[:, chunk - 1:chunk, :]
        bt = jnp.broadcast_to(last, (tl // chunk, chunk, heads * hd)).reshape(tl, heads * hd)
        return causal, b, bt

    def store_decayed(b, bt):
        b_ref[...] = b
        gt_ref[...] = jnp.exp(bt)
        qt_ref[...] = (q_ref[...].astype(F32) * jnp.exp(b)).astype(BF16)
        kh_ref[...] = (k_ref[...].astype(F32) * jnp.exp(bt - b)).astype(BF16)

    half = CHUNK // 2
    row = lax.broadcasted_iota(jnp.int32, (half, 1), 0)

    def pairs_exact(rows, lanes):
        q16 = q_ref[rows, lanes].astype(F32)
        k16 = k_ref[rows, lanes].astype(F32)
        v16 = v_ref[rows, lanes].astype(F32)
        b16 = b_ref[rows, lanes]
        q_top, q_bot = q16[:half], q16[half:]
        b_top, b_bot = b16[:half], b16[half:]
        acc_top = jnp.zeros((half, hd), F32)
        acc_bot = jnp.zeros((half, hd), F32)
        for s in range(CHUNK):
            bs = b16[s:s + 1]
            ks = k16[s:s + 1]
            vs = v16[s:s + 1]
            if s < half:
                w = jnp.sum(q_top * ks * jnp.exp(jnp.minimum(b_top - bs, 0.0)), axis=-1, keepdims=True)
                acc_top = acc_top + jnp.where(row >= s, w, 0.0) * vs
                w = jnp.sum(q_bot * ks * jnp.exp(b_bot - bs), axis=-1, keepdims=True)
                acc_bot = acc_bot + w * vs
            else:
                w = jnp.sum(q_bot * ks * jnp.exp(jnp.minimum(b_bot - bs, 0.0)), axis=-1, keepdims=True)
                acc_bot = acc_bot + jnp.where(row + half >= s, w, 0.0) * vs
        return jnp.concatenate([acc_top, acc_bot], axis=0)

    def carry_state(rows, first_row, h):
        lanes = slice(h * hd, (h + 1) * hd)
        st = st_ref[h]
        inter = lax.dot_general(qt_ref[rows, lanes], st.astype(BF16),
                                (((1,), (1,)), ((), ())), preferred_element_type=F32)
        upd = lax.dot_general(v_ref[rows, lanes], kh_ref[rows, lanes],
                              (((0,), (0,)), ((), ())), preferred_element_type=F32)
        st_ref[h] = gt_ref[first_row, lanes] * st + upd
        return inter

    def exact_step(n, _):
        r0 = pl.multiple_of(n * CHUNK, CHUNK)
        rows = pl.ds(r0, CHUNK)
        for h in range(heads):
            lanes = slice(h * hd, (h + 1) * hd)
            o_ref[rows, lanes] = carry_state(rows, pl.ds(r0, 1), h) + pairs_exact(rows, lanes)
        return 0

    causal, b, bt = chunk_sums(HGRN_WIDE_CHUNK)
    worst = jnp.max(jnp.max(-bt, axis=0, keepdims=True), axis=1, keepdims=True)[0, 0]
    factorable = worst <= HGRN_FACTOR_LIMIT

    @pl.when(factorable)
    def _():
        store_decayed(b, bt)
        kx = (k_ref[...].astype(F32) * jnp.exp(-b)).astype(BF16)
        for h in range(heads):
            lanes = slice(h * hd, (h + 1) * hd)
            sc = lax.dot_general(qt_ref[:, lanes], kx[:, lanes], (((1,), (1,)), ((), ())),
                                 preferred_element_type=F32)
            o_ref[:, lanes] = jnp.dot(jnp.where(causal, sc, 0.0).astype(BF16), v_ref[:, lanes],
                                      preferred_element_type=F32)
        for n in range(tl // HGRN_WIDE_CHUNK):
            rows = slice(n * HGRN_WIDE_CHUNK, (n + 1) * HGRN_WIDE_CHUNK)
            for h in range(heads):
                lanes = slice(h * hd, (h + 1) * hd)
                o_ref[rows, lanes] += carry_state(rows, slice(rows.start, rows.start + 1), h)

    @pl.when(jnp.logical_not(factorable))
    def _():
        store_decayed(*chunk_sums(CHUNK)[1:])
        lax.fori_loop(0, tl // CHUNK, exact_step, 0)

    parts = []
    for h in range(heads):
        lanes = slice(h * hd, (h + 1) * hd)
        o = o_ref[:, lanes]
        o = o * lax.rsqrt(jnp.mean(o * o, axis=-1, keepdims=True) + RMS_EPS)
        parts.append((o * ng_ref[:, lanes] * og_ref[:, lanes].astype(F32)).astype(BF16))
    on = jnp.concatenate(parts, axis=-1)
    yh = jnp.dot(on, wph_ref[...], preferred_element_type=F32)
    merged = ms_ref[...].astype(F32) + gh_ref[...].astype(F32) * yh
    x1 = x_ref[...] + jnp.dot(merged.astype(BF16), wout_ref[...], preferred_element_type=F32)
    x1_ref[...] = x1
    h2_ref[...] = _rmsnorm(x1, g2_ref[...]).astype(BF16)


def _ffn_kernel(h_ref, halo_ref, wug_ref, wuv_ref, cwg_ref, cwv_ref, cbg_ref, cbv_ref, wd_ref,
                x1_ref, gf_ref, o_ref, hcat_ref, *, tiles_per_seq):
    i = pl.program_id(0)
    j = pl.program_id(1)
    tm = h_ref.shape[0]
    nh = halo_ref.shape[0]

    @pl.when(j == 0)
    def _():
        keep = jnp.where(i % tiles_per_seq == 0, 0.0, 1.0).astype(BF16)
        hcat_ref[0:nh] = halo_ref[...] * keep
        hcat_ref[nh:nh + tm] = h_ref[...]
        o_ref[...] = jnp.zeros_like(o_ref)

    def conv_up(w_ref, cw_ref, cb_ref):
        ext = jnp.dot(hcat_ref[...], w_ref[...], preferred_element_type=F32)
        out = cb_ref[...]
        for tap in range(CONV_WIDTH):
            back = CONV_WIDTH - 1 - tap
            out = out + cw_ref[tap:tap + 1, :] * ext[nh - back:nh - back + tm]
        return out

    gate = conv_up(wug_ref, cwg_ref, cbg_ref)
    val = conv_up(wuv_ref, cwv_ref, cbv_ref)
    act = (0.5 * gate * (1.0 + jnp.tanh(0.5 * gate)) * val).astype(BF16)
    o_ref[...] += jnp.dot(act, wd_ref[...], preferred_element_type=F32)

    @pl.when(j == pl.num_programs(1) - 1)
    def _():
        o_ref[...] = _rmsnorm(x1_ref[...] + o_ref[...], gf_ref[...])


def _tile(n, want):
    t = min(n, want)
    assert n % t == 0, (n, want)
    return t


def _row(v):
    return v.reshape(1, -1).astype(F32)


def _inproj(x2, g, w_in, lb_logits, to_cast, *, wh):
    tok, dm = x2.shape
    tm = _tile(tok, TOKEN_TILE)
    nblk = wh // MXU_DIM_V7X
    n_a = 5 * wh
    assert w_in.shape[1] == n_a + 2 * dm and dm % wh == 0
    tok_spec = lambda w: pl.BlockSpec((tm, w), lambda i: (i, 0))
    cols = lambda n, width: pl.BlockSpec((dm, width), lambda i: (0, n), pipeline_mode=pl.Buffered(1))
    u, q, k, v, og, lf = pl.pallas_call(
        functools.partial(_inproj_a_kernel, width=wh),
        grid=(tok // tm,),
        in_specs=[tok_spec(dm), _resident((1, dm)), cols(0, n_a), _resident(lb_logits.shape)],
        out_specs=[pl.BlockSpec((nblk, tm, MXU_DIM_V7X), lambda i: (0, i, 0))] + [tok_spec(wh)] * 5,
        out_shape=[jax.ShapeDtypeStruct((nblk, tok, MXU_DIM_V7X), BF16)]
                  + [jax.ShapeDtypeStruct((tok, wh), BF16)] * 4 + [jax.ShapeDtypeStruct((tok, wh), F32)],
        compiler_params=_params("parallel"),
        name="inproj_a",
    )(x2, g, w_in, lb_logits)
    gate_blocks = 2 * dm // wh
    steps = tok // tm
    bf16_rows = 16
    assert all(w.shape[0] % (steps * bf16_rows) == 0 for w in to_cast)
    row_blocks = [pl.BlockSpec((w.shape[0] // steps, w.shape[1]), lambda i: (i, 0)) for w in to_cast]
    gs, gh, *casted = pl.pallas_call(
        functools.partial(_inproj_b_kernel, n_w=gate_blocks, n_cast=len(to_cast)),
        grid=(steps,),
        in_specs=[tok_spec(dm), _resident((1, dm))] + [cols(n_a // wh + n, wh) for n in range(gate_blocks)]
                 + row_blocks,
        out_specs=[tok_spec(dm)] * 2 + row_blocks,
        out_shape=[jax.ShapeDtypeStruct((tok, dm), BF16)] * 2
                  + [jax.ShapeDtypeStruct(w.shape, BF16) for w in to_cast],
        compiler_params=_params("parallel"),
        name="inproj_b",
    )(x2, g, *([w_in] * gate_blocks), *to_cast)
    return u, q, k, v, og, lf, gs, gh, casted


def _s5_prep(a_re, a_im, log_dt, b_re, b_im, c_re, c_im, d):
    groups = a_re.shape[0]
    nblk = groups // S5_BLOCK_GROUPS
    ns = S5_BLOCK_STATE
    tile_c = S5_BLOCK_GROUPS * S5_GROUP
    blk3 = lambda r, c: pl.BlockSpec((1, r, c), lambda j: (j, 0, 0))
    pw_rows = 24
    gb, p, c = S5_BLOCK_GROUPS, S5_STATE, S5_GROUP
    tiled = lambda t, r, w: t.astype(F32).reshape(nblk, r, w)
    return pl.pallas_call(
        _s5_prep_kernel,
        grid=(nblk,),
        in_specs=[blk3(gb * p, c), blk3(gb * p, c), blk3(gb * c, p), blk3(gb * c, p),
                  blk3(gb, p), blk3(gb, p), blk3(gb, 1), blk3(gb, c)],
        out_specs=[pl.BlockSpec((1, S5_SUB, tile_c, 2 * ns), lambda j: (j, 0, 0, 0)), blk3(2 * ns, tile_c),
                   pl.BlockSpec((1, CHUNK, tile_c, tile_c), lambda j: (j, 0, 0, 0)), blk3(pw_rows, 2 * ns)],
        out_shape=[jax.ShapeDtypeStruct((nblk, S5_SUB, tile_c, 2 * ns), BF16),
                   jax.ShapeDtypeStruct((nblk, 2 * ns, tile_c), BF16),
                   jax.ShapeDtypeStruct((nblk, CHUNK, tile_c, tile_c), BF16),
                   jax.ShapeDtypeStruct((nblk, pw_rows, 2 * ns), F32)],
        compiler_params=_params("parallel"),
        name="s5_prep",
    )(tiled(b_re, gb * p, c), tiled(b_im, gb * p, c), tiled(c_re, gb * c, p), tiled(c_im, gb * c, p),
      tiled(a_re, gb, p), tiled(a_im, gb, p), tiled(log_dt, gb, 1), tiled(d, gb, c))


def _s5_scan(u, bc, cc, klag, pw, *, seq):
    nblk, tok, tw = u.shape
    ns = S5_BLOCK_STATE
    rows_seq = seq // CHUNK
    rt = _tile(rows_seq, S5_ROW_TILE)
    per_tile = lambda a: pl.BlockSpec((None,) + a.shape[1:], lambda j, i: (j,) + (0,) * (a.ndim - 1))
    rows_spec = pl.BlockSpec((None, rt * CHUNK, tw), lambda j, i: (j, i, 0))
    return pl.pallas_call(
        functools.partial(_s5_scan_kernel, tiles_per_seq=rows_seq // rt),
        grid=(nblk, tok // (rt * CHUNK)),
        in_specs=[rows_spec, per_tile(bc), per_tile(cc), per_tile(klag), per_tile(pw)],
        out_specs=rows_spec,
        out_shape=jax.ShapeDtypeStruct(u.shape, BF16),
        scratch_shapes=[pltpu.VMEM((rt, 2 * ns), F32), pltpu.VMEM((8, 2 * ns), F32)],
        compiler_params=_params("parallel", "arbitrary"),
        name="s5_scan",
    )(u, bc, cc, klag, pw)


def _s5_tail(y, gs, w_glu, b_glu, w_proj):
    nblk, tok, tw = y.shape
    dm = gs.shape[1]
    tm = _tile(tok, TOKEN_TILE)
    tok_spec = lambda w: pl.BlockSpec((tm, w), lambda i: (i, 0))
    return pl.pallas_call(
        _s5_tail_kernel,
        grid=(tok // tm,),
        in_specs=[pl.BlockSpec((nblk, tm, tw), lambda i: (0, i, 0)), tok_spec(dm), _resident(w_glu.shape),
                  _resident(b_glu.shape), _resident(w_proj.shape)],
        out_specs=tok_spec(dm),
        out_shape=jax.ShapeDtypeStruct((tok, dm), BF16),
        compiler_params=_params("parallel"),
        name="s5_tail",
    )(y, gs, w_glu, b_glu, w_proj)


def _hgrn_merge(q, k, v, og, lf, gh, ms, x2, w_proj, w_out, norm_g, ffn_g, *, seq):
    tok, wh = q.shape
    dm = x2.shape[1]
    heads = wh // HEAD_DIM
    tl = _tile(seq, HGRN_TOKEN_TILE)
    lt = seq // tl
    seq_spec = lambda w: pl.BlockSpec((tl, w), lambda b, l: (b * lt + l, 0))
    return pl.pallas_call(
        functools.partial(_hgrn_kernel, heads=heads),
        grid=(tok // seq, lt),
        in_specs=[seq_spec(wh)] * 5 + [seq_spec(dm)] * 3
                 + [_resident(w_proj.shape), _resident(w_out.shape), _resident((1, wh)), _resident((1, dm))],
        out_specs=[seq_spec(dm)] * 2,
        out_shape=[jax.ShapeDtypeStruct((tok, dm), F32), jax.ShapeDtypeStruct((tok, dm), BF16)],
        scratch_shapes=[pltpu.VMEM((heads, HEAD_DIM, HEAD_DIM), F32),
                        pltpu.VMEM((tl, wh), BF16), pltpu.VMEM((tl, wh), BF16),
                        pltpu.VMEM((tl, wh), F32), pltpu.VMEM((tl, wh), F32), pltpu.VMEM((tl, wh), F32)],
        compiler_params=_params("arbitrary", "arbitrary"),
        name="hgrn_merge",
    )(q, k, v, og, lf, gh, ms, x2, w_proj, w_out, norm_g, ffn_g)


def _ffn(h2, x1, w_up, cw, cb, w_down, final_g, *, seq):
    tok, dm = h2.shape
    dff = w_down.shape[0]
    tf = _tile(dff, FFN_FF_TILE)
    nf = dff // tf
    tmf = _tile(seq, FFN_TOKEN_TILE)
    halo = 16
    return pl.pallas_call(
        functools.partial(_ffn_kernel, tiles_per_seq=seq // tmf),
        grid=(tok // tmf, nf),
        in_specs=[pl.BlockSpec((tmf, dm), lambda i, j: (i, 0)),
                  pl.BlockSpec((halo, dm), lambda i, j: (jnp.maximum(i * (tmf // halo) - 1, 0), 0)),
                  pl.BlockSpec((dm, tf), lambda i, j: (0, j)),
                  pl.BlockSpec((dm, tf), lambda i, j: (0, nf + j)),
                  pl.BlockSpec((CONV_WIDTH, tf), lambda i, j: (0, j)),
                  pl.BlockSpec((CONV_WIDTH, tf), lambda i, j: (0, nf + j)),
                  pl.BlockSpec((1, tf), lambda i, j: (0, j)),
                  pl.BlockSpec((1, tf), lambda i, j: (0, nf + j)),
                  pl.BlockSpec((tf, dm), lambda i, j: (j, 0)),
                  pl.BlockSpec((tmf, dm), lambda i, j: (i, 0), pipeline_mode=pl.Buffered(1)),
                  pl.BlockSpec((1, dm), lambda i, j: (0, 0))],
        out_specs=pl.BlockSpec((tmf, dm), lambda i, j: (i, 0), pipeline_mode=pl.Buffered(1)),
        out_shape=jax.ShapeDtypeStruct((tok, dm), F32),
        scratch_shapes=[pltpu.VMEM((halo + tmf, dm), BF16)],
        compiler_params=pltpu.CompilerParams(dimension_semantics=("parallel", "arbitrary"),
                                             vmem_limit_bytes=FFN_VMEM_LIMIT_BYTES),
        name="ffn",
    )(h2, h2, w_up, w_up, cw, cw, cb, cb, w_down, x1, final_g)


def kernel(x, ln_mix_g, w_in, s5_a_re, s5_a_im, s5_log_dt, s5_b_re, s5_b_im, s5_c_re, s5_c_im,
           s5_d, s5_w_glu, s5_b_glu, w_proj_s5, hgrn_lb_logits, hgrn_norm_g, w_proj_hgrn,
           w_out, ln_ffn_g, w_up, conv_w, conv_b, w_down, ln_final_g):
    bsz, seq, dm = x.shape
    assert ln_mix_g.shape[0] == 1, "single-layer block"
    tok = bsz * seq
    ws5 = s5_w_glu.shape[1]
    wh = hgrn_norm_g.shape[1]
    assert ws5 == wh and ws5 % MXU_DIM_V7X == 0 and seq % CHUNK == 0
    assert w_in.shape[2] == ws5 + 4 * wh + 2 * dm

    x2 = x.reshape(tok, dm)
    later_weights = [w.astype(F32) for w in (s5_w_glu[0], w_proj_s5[0], w_proj_hgrn[0], w_out[0], w_up[0], w_down[0])]
    u, q, k, v, og, lf, gs, gh, (w_glu_b, w_ps5_b, w_ph_b, w_out_b, w_up_b, w_down_b) = _inproj(
        x2, _row(ln_mix_g[0]), w_in[0].astype(BF16), hgrn_lb_logits.astype(F32), later_weights, wh=wh)

    bc, cc, klag, pw = _s5_prep(s5_a_re[0], s5_a_im[0], s5_log_dt[0], s5_b_re[0], s5_b_im[0],
                                s5_c_re[0], s5_c_im[0], s5_d[0])
    y = _s5_scan(u, bc, cc, klag, pw, seq=seq)
    ms = _s5_tail(y, gs, w_glu_b, _row(s5_b_glu[0]), w_ps5_b)

    x1, h2 = _hgrn_merge(q, k, v, og, lf, gh, ms, x2, w_ph_b, w_out_b,
                         _row(hgrn_norm_g[0]), _row(ln_ffn_g[0]), seq=seq)

    out = _ffn(h2, x1, w_up_b, conv_w[0].astype(F32), conv_b[0].astype(F32).reshape(1, -1),
               w_down_b, _row(ln_final_g), seq=seq)
    return out.reshape(bsz, seq, dm)
```

```python
import functools
import math

import jax
import jax.numpy as jnp
from jax import lax
from jax.experimental import pallas as pl
from jax.experimental.pallas import tpu as pltpu

F32 = jnp.float32
BF16 = jnp.bfloat16

RMS_EPS = 1e-6
S5_MAX_RE = -1e-4
S5_GROUP = 16
S5_STATE = 64
HEAD_DIM = 128
CONV_WIDTH = 3

LANES_V7X = 128
MXU_DIM_V7X = 256
CHUNK = 16
S5_SUB = 4
S5_BLOCK_GROUPS = MXU_DIM_V7X // S5_GROUP
S5_BLOCK_STATE = S5_BLOCK_GROUPS * S5_STATE
VMEM_LIMIT_BYTES = 56 * 1024 * 1024
FFN_VMEM_LIMIT_BYTES = 62 * 1024 * 1024
HGRN_FACTOR_LIMIT = 60.0
HGRN_WIDE_CHUNK = 64

TOKEN_TILE = 512
S5_ROW_TILE = 256
HGRN_TOKEN_TILE = 256
FFN_TOKEN_TILE = 1024
FFN_FF_TILE = 512


def _sigmoid(y):
    return 0.5 + 0.5 * jnp.tanh(0.5 * y)


def _rmsnorm(x, g):
    return x * lax.rsqrt(jnp.mean(x * x, axis=-1, keepdims=True) + RMS_EPS) * g


def _params(*sem):
    return pltpu.CompilerParams(dimension_semantics=sem, vmem_limit_bytes=VMEM_LIMIT_BYTES)


def _resident(shape):
    return pl.BlockSpec(shape, lambda *_: (0,) * len(shape), pipeline_mode=pl.Buffered(1))


def _inproj_a_kernel(x_ref, g_ref, w_ref, lbl_ref, u_ref, q_ref, k_ref, v_ref, og_ref, lf_ref, *, width):
    h = _rmsnorm(x_ref[...], g_ref[...]).astype(BF16)

    def proj(i):
        return jnp.dot(h, w_ref[:, i * width:(i + 1) * width], preferred_element_type=F32)

    a = proj(1)
    q_ref[...] = (a * _sigmoid(a)).astype(BF16)
    lbl = lbl_ref[...]
    e = jnp.exp(lbl - jnp.max(lbl, axis=0, keepdims=True))
    lb = e[0:1, :] / jnp.sum(e, axis=0, keepdims=True)
    sg = _sigmoid(proj(2))
    lf_ref[...] = jnp.log(lb + (1.0 - lb) * sg)
    k_ref[...] = ((1.0 - lb) * (1.0 - sg)).astype(BF16)
    a = proj(4)
    og_ref[...] = (a * _sigmoid(a)).astype(BF16)
    v_ref[...] = proj(3).astype(BF16)
    a = proj(0).astype(BF16)
    for j in range(u_ref.shape[0]):
        u_ref[j] = a[:, j * MXU_DIM_V7X:(j + 1) * MXU_DIM_V7X]


def _inproj_b_kernel(x_ref, g_ref, *refs, n_w, n_cast):
    w_refs, src_refs = refs[:n_w], refs[n_w:n_w + n_cast]
    o_refs, dst_refs = refs[n_w + n_cast:n_w + n_cast + 2], refs[n_w + n_cast + 2:]
    for src, dst in zip(src_refs, dst_refs):
        dst[...] = src[...].astype(BF16)
    per_out = len(w_refs) // len(o_refs)
    h = _rmsnorm(x_ref[...], g_ref[...]).astype(BF16)
    for n, w_ref in enumerate(w_refs):
        cols = w_ref.shape[1]
        lo = (n % per_out) * cols
        o_refs[n // per_out][:, lo:lo + cols] = _sigmoid(
            jnp.dot(h, w_ref[...], preferred_element_type=F32)).astype(BF16)


def _spread(x, reps):
    w = x.shape[1]
    r = lax.broadcasted_iota(jnp.int32, (w, reps * w), 0)
    c = lax.broadcasted_iota(jnp.int32, (w, reps * w), 1)
    sel = jnp.where(c % w == r, 1.0, 0.0).astype(BF16)
    out = jnp.zeros((x.shape[0], reps * w), F32)
    for piece in _split3(x):
        out = out + jnp.dot(piece, sel, preferred_element_type=F32)
    return out


def _own_group(x, row_div, lane_div):
    r = lax.broadcasted_iota(jnp.int32, x.shape, 0)
    c = lax.broadcasted_iota(jnp.int32, x.shape, 1)
    return jnp.where(r // row_div == c // lane_div, x, 0.0)


def _s5_prep_kernel(bre_ref, bim_ref, cre_ref, cim_ref, are_ref, aim_ref, ldt_ref, d_ref,
                    bc_ref, cc_ref, klag_ref, pw_ref):
    ns = S5_BLOCK_STATE
    gb, p, c = S5_BLOCK_GROUPS, S5_STATE, S5_GROUP
    b_tile = lambda ref: _own_group(_spread(ref[0], gb), p, c).T
    c_tile = lambda ref: _own_group(_spread(ref[0], gb), c, p).T
    per_lane = lambda x, w: jnp.sum(_own_group(x, 1, w), axis=0, keepdims=True)
    lam_re = jnp.minimum(per_lane(_spread(are_ref[0], gb), p), S5_MAX_RE)
    lam_im = per_lane(_spread(aim_ref[0], gb), p)
    dt = jnp.exp(per_lane(jnp.broadcast_to(ldt_ref[0], (gb, ns)), p))
    d_row = per_lane(_spread(d_ref[0], gb), c)
    mag = jnp.exp(lam_re * dt)
    ab_re = mag * jnp.cos(lam_im * dt)
    ab_im = mag * jnp.sin(lam_im * dt)
    den = lam_re * lam_re + lam_im * lam_im
    nr = ab_re - 1.0
    ni = ab_im
    cf_re = (nr * lam_re + ni * lam_im) / den
    cf_im = (ni * lam_re - nr * lam_im) / den
    bre = b_tile(bre_ref)
    bim = b_tile(bim_ref)
    b_re = cf_re * bre - cf_im * bim
    b_im = cf_re * bim + cf_im * bre
    cre = c_tile(cre_ref).astype(BF16)
    cim = c_tile(cim_ref).astype(BF16)
    cc_ref[0] = jnp.concatenate([cre, -cim], axis=0)

    p_re = jnp.ones_like(ab_re)
    p_im = jnp.zeros_like(ab_re)
    pw_ref[...] = jnp.zeros_like(pw_ref)
    for n in range(CHUNK + 1):
        pw_ref[0, n:n + 1, :] = jnp.concatenate([p_re, p_im], axis=-1)
        if n < S5_SUB:
            bc_ref[0, S5_SUB - 1 - n] = jnp.concatenate(
                [b_re * p_re - b_im * p_im, b_re * p_im + b_im * p_re], axis=-1).astype(BF16)
        if n < CHUNK:
            t_re = (b_re * p_re - b_im * p_im).astype(BF16)
            t_im = (b_re * p_im + b_im * p_re).astype(BF16)
            tap = (jnp.dot(t_re, cre, preferred_element_type=F32)
                   - jnp.dot(t_im, cim, preferred_element_type=F32))
            if n == 0:
                r = lax.broadcasted_iota(jnp.int32, tap.shape, 0)
                c = lax.broadcasted_iota(jnp.int32, tap.shape, 1)
                tap = tap + jnp.where(r == c, d_row, 0.0)
            klag_ref[0, n] = tap.astype(BF16)
        p_re, p_im = p_re * ab_re - p_im * ab_im, p_re * ab_im + p_im * ab_re


def _s5_scan_kernel(u_ref, bc_ref, cc_ref, klag_ref, pw_ref, y_ref, z_ref, carry_ref, *, tiles_per_seq):
    ns = S5_BLOCK_STATE
    tw = MXU_DIM_V7X
    rt = u_ref.shape[0] // CHUNK
    by_pos = jnp.swapaxes(u_ref[...].reshape(rt, CHUNK, tw), 0, 1)

    def u_tile(s):
        return by_pos[s]

    def pw(n):
        return pw_ref[n:n + 1, 0:ns], pw_ref[n:n + 1, ns:2 * ns]

    @pl.when(pl.program_id(1) % tiles_per_seq == 0)
    def _():
        carry_ref[...] = jnp.zeros_like(carry_ref)

    acc_re = jnp.zeros((rt, ns), F32)
    acc_im = jnp.zeros((rt, ns), F32)
    for sub in range(CHUNK // S5_SUB):
        bb = jnp.dot(u_tile(sub * S5_SUB), bc_ref[0], preferred_element_type=F32)
        for r in range(1, S5_SUB):
            bb = bb + jnp.dot(u_tile(sub * S5_SUB + r), bc_ref[r], preferred_element_type=F32)
        p_re, p_im = pw(CHUNK - S5_SUB * (sub + 1))
        acc_re = acc_re + bb[:, :ns] * p_re - bb[:, ns:] * p_im
        acc_im = acc_im + bb[:, :ns] * p_im + bb[:, ns:] * p_re
    z_ref[:, 0:ns] = acc_re
    z_ref[:, ns:2 * ns] = acc_im

    a_re, a_im = pw(CHUNK)

    def body(r, carry):
        s_re, s_im = carry
        z_re = z_ref[pl.ds(r, 1), 0:ns]
        z_im = z_ref[pl.ds(r, 1), ns:2 * ns]
        z_ref[pl.ds(r, 1), 0:ns] = s_re
        z_ref[pl.ds(r, 1), ns:2 * ns] = s_im
        return (a_re * s_re - a_im * s_im + z_re, a_re * s_im + a_im * s_re + z_im)

    s_re, s_im = lax.fori_loop(0, rt, body, (carry_ref[0:1, 0:ns], carry_ref[0:1, ns:2 * ns]))
    carry_ref[0:1, 0:ns] = s_re
    carry_ref[0:1, ns:2 * ns] = s_im

    s_re = z_ref[:, 0:ns]
    s_im = z_ref[:, ns:2 * ns]
    ys = []
    for t in range(CHUNK):
        y = jnp.dot(u_tile(t), klag_ref[0], preferred_element_type=F32)
        for s in range(t):
            y = y + jnp.dot(u_tile(s), klag_ref[t - s], preferred_element_type=F32)
        p_re, p_im = pw(t + 1)
        rot = jnp.concatenate([s_re * p_re - s_im * p_im, s_re * p_im + s_im * p_re], axis=-1)
        y = y + jnp.dot(rot.astype(BF16), cc_ref[...], preferred_element_type=F32)
        ys.append(y.astype(BF16))
    y_ref[...] = jnp.swapaxes(jnp.stack(ys, axis=0), 0, 1).reshape(rt * CHUNK, tw)


def _s5_tail_kernel(y_ref, gs_ref, wg_ref, bg_ref, wp_ref, o_ref):
    y = jnp.concatenate([y_ref[j] for j in range(y_ref.shape[0])], axis=-1).astype(F32)
    z = 0.5 * y * (1.0 + jnp.tanh(math.sqrt(2.0 / math.pi) * (y + 0.044715 * (y * y * y))))
    gl = jnp.dot(z.astype(BF16), wg_ref[...], preferred_element_type=F32) + bg_ref[...]
    zz = (z * _sigmoid(gl)).astype(BF16)
    ys = jnp.dot(zz, wp_ref[...], preferred_element_type=F32)
    o_ref[...] = (gs_ref[...].astype(F32) * ys).astype(BF16)


def _split3(x):
    hi = x.astype(BF16)
    r = x - hi.astype(F32)
    mid = r.astype(BF16)
    lo = (r - mid.astype(F32)).astype(BF16)
    return hi, mid, lo


def _hgrn_kernel(q_ref, k_ref, v_ref, og_ref, lf_ref, gh_ref, ms_ref, x_ref,
                 wph_ref, wout_ref, ng_ref, g2_ref, x1_ref, h2_ref,
                 st_ref, qt_ref, kh_ref, b_ref, gt_ref, o_ref, *, heads):
    tl = q_ref.shape[0]
    hd = HEAD_DIM

    @pl.when(pl.program_id(1) == 0)
    def _():
        st_ref[...] = jnp.zeros_like(st_ref)

    r = lax.broadcasted_iota(jnp.int32, (tl, tl), 0)
    c = lax.broadcasted_iota(jnp.int32, (tl, tl), 1)
    pieces = _split3(lf_ref[...])

    def chunk_sums(chunk):
        causal = jnp.logical_and((r // chunk) == (c // chunk), c <= r)
        tri = jnp.where(causal, 1.0, 0.0).astype(BF16)
        b = jnp.zeros((tl, heads * hd), F32)
        for piece in pieces:
            b = b + jnp.dot(tri, piece, preferred_element_type=F32)
        last = b.reshape(tl // chunk, chunk, heads * hd)[:, chunk - 1:chunk, :]
        bt = jnp.broadcast_to(last, (tl // chunk, chunk, heads * hd)).reshape(tl, heads * hd)
        return causal, b, bt

    def store_decayed(b, bt):
        b_ref[...] = b
        gt_ref[...] = jnp.exp(bt)
        qt_ref[...] = (q_ref[...].astype(F32) * jnp.exp(b)).astype(BF16)
        kh_ref[...] = (k_ref[...].astype(F32) * jnp.exp(bt - b)).astype(BF16)

    half = CHUNK // 2
    row = lax.broadcasted_iota(jnp.int32, (half, 1), 0)

    def pairs_exact(rows, lanes):
        q16 = q_ref[rows, lanes].astype(F32)
        k16 = k_ref[rows, lanes].astype(F32)
        v16 = v_ref[rows, lanes].astype(F32)
        b16 = b_ref[rows, lanes]
        q_top, q_bot = q16[:half], q16[half:]
        b_top, b_bot = b16[:half], b16[half:]
        acc_top = jnp.zeros((half, hd), F32)
        acc_bot = jnp.zeros((half, hd), F32)
        for s in range(CHUNK):
            bs = b16[s:s + 1]
            ks = k16[s:s + 1]
            vs = v16[s:s + 1]
            if s < half:
                w = jnp.sum(q_top * ks * jnp.exp(jnp.minimum(b_top - bs, 0.0)), axis=-1, keepdims=True)
                acc_top = acc_top + jnp.where(row >= s, w, 0.0) * vs
                w = jnp.sum(q_bot * ks * jnp.exp(b_bot - bs), axis=-1, keepdims=True)
                acc_bot = acc_bot + w * vs
            else:
                w = jnp.sum(q_bot * ks * jnp.exp(jnp.minimum(b_bot - bs, 0.0)), axis=-1, keepdims=True)
                acc_bot = acc_bot + jnp.where(row + half >= s, w, 0.0) * vs
        return jnp.concatenate([acc_top, acc_bot], axis=0)

    def carry_state(rows, first_row, h):
        lanes = slice(h * hd, (h + 1) * hd)
        st = st_ref[h]
        inter = lax.dot_general(qt_ref[rows, lanes], st.astype(BF16),
                                (((1,), (1,)), ((), ())), preferred_element_type=F32)
        upd = lax.dot_general(v_ref[rows, lanes], kh_ref[rows, lanes],
                              (((0,), (0,)), ((), ())), preferred_element_type=F32)
        st_ref[h] = gt_ref[first_row, lanes] * st + upd
        return inter

    def exact_step(n, _):
        r0 = pl.multiple_of(n * CHUNK, CHUNK)
        rows = pl.ds(r0, CHUNK)
        for h in range(heads):
            lanes = slice(h * hd, (h + 1) * hd)
            o_ref[rows, lanes] = carry_state(rows, pl.ds(r0, 1), h) + pairs_exact(rows, lanes)
        return 0

    causal, b, bt = chunk_sums(HGRN_WIDE_CHUNK)
    worst = jnp.max(jnp.max(-bt, axis=0, keepdims=True), axis=1, keepdims=True)[0, 0]
    factorable = worst <= HGRN_FACTOR_LIMIT

    @pl.when(factorable)
    def _():
        store_decayed(b, bt)
        kx = (k_ref[...].astype(F32) * jnp.exp(-b)).astype(BF16)
        for h in range(heads):
            lanes = slice(h * hd, (h + 1) * hd)
            sc = lax.dot_general(qt_ref[:, lanes], kx[:, lanes], (((1,), (1,)), ((), ())),
                                 preferred_element_type=F32)
            o_ref[:, lanes] = jnp.dot(jnp.where(causal, sc, 0.0).astype(BF16), v_ref[:, lanes],
                                      preferred_element_type=F32)
        for n in range(tl // HGRN_WIDE_CHUNK):
            rows = slice(n * HGRN_WIDE_CHUNK, (n + 1) * HGRN_WIDE_CHUNK)
            for h in range(heads):
                lanes = slice(h * hd, (h + 1) * hd)
                o_ref[rows, lanes] += carry_state(rows, slice(rows.start, rows.start + 1), h)

    @pl.when(jnp.logical_not(factorable))
    def _():
        store_decayed(*chunk_sums(CHUNK)[1:])
        lax.fori_loop(0, tl // CHUNK, exact_step, 0)

    parts = []
    for h in range(heads):
        lanes = slice(h * hd, (h + 1) * hd)
        o = o_ref[:, lanes]
        o = o * lax.rsqrt(jnp.mean(o * o, axis=-1, keepdims=True) + RMS_EPS)
        parts.append((o * ng_ref[:, lanes] * og_ref[:, lanes].astype(F32)).astype(BF16))
    on = jnp.concatenate(parts, axis=-1)
    yh = jnp.dot(on, wph_ref[...], preferred_element_type=F32)
    merged = ms_ref[...].astype(F32) + gh_ref[...].astype(F32) * yh
    x1 = x_ref[...] + jnp.dot(merged.astype(BF16), wout_ref[...], preferred_element_type=F32)
    x1_ref[...] = x1
    h2_ref[...] = _rmsnorm(x1, g2_ref[...]).astype(BF16)


def _ffn_kernel(h_ref, halo_ref, wug_ref, wuv_ref, cwg_ref, cwv_ref, cbg_ref, cbv_ref, wd_ref,
                x1_ref, gf_ref, o_ref, hcat_ref, *, tiles_per_seq):
    i = pl.program_id(0)
    j = pl.program_id(1)
    tm = h_ref.shape[0]
    nh = halo_ref.shape[0]

    @pl.when(j == 0)
    def _():
        keep = jnp.where(i % tiles_per_seq == 0, 0.0, 1.0).astype(BF16)
        hcat_ref[0:nh] = halo_ref[...] * keep
        hcat_ref[nh:nh + tm] = h_ref[...]
        o_ref[...] = jnp.zeros_like(o_ref)

    def conv_up(w_ref, cw_ref, cb_ref):
        ext = jnp.dot(hcat_ref[...], w_ref[...], preferred_element_type=F32)
        out = cb_ref[...]
        for tap in range(CONV_WIDTH):
            back = CONV_WIDTH - 1 - tap
            out = out + cw_ref[tap:tap + 1, :] * ext[nh - back:nh - back + tm]
        return out

    gate = conv_up(wug_ref, cwg_ref, cbg_ref)
    val = conv_up(wuv_ref, cwv_ref, cbv_ref)
    act = (0.5 * gate * (1.0 + jnp.tanh(0.5 * gate)) * val).astype(BF16)
    o_ref[...] += jnp.dot(act, wd_ref[...], preferred_element_type=F32)

    @pl.when(j == pl.num_programs(1) - 1)
    def _():
        o_ref[...] = _rmsnorm(x1_ref[...] + o_ref[...], gf_ref[...])


def _tile(n, want):
    t = min(n, want)
    assert n % t == 0, (n, want)
    return t


def _row(v):
    return v.reshape(1, -1).astype(F32)


def _inproj(x2, g, w_in, lb_logits, to_cast, *, wh):
    tok, dm = x2.shape
    tm = _tile(tok, TOKEN_TILE)
    nblk = wh // MXU_DIM_V7X
    n_a = 5 * wh
    assert w_in.shape[1] == n_a + 2 * dm and dm % wh == 0
    tok_spec = lambda w: pl.BlockSpec((tm, w), lambda i: (i, 0))
    cols = lambda n, width: pl.BlockSpec((dm, width), lambda i: (0, n), pipeline_mode=pl.Buffered(1))
    u, q, k, v, og, lf = pl.pallas_call(
        functools.partial(_inproj_a_kernel, width=wh),
        grid=(tok // tm,),
        in_specs=[tok_spec(dm), _resident((1, dm)), cols(0, n_a), _resident(lb_logits.shape)],
        out_specs=[pl.BlockSpec((nblk, tm, MXU_DIM_V7X), lambda i: (0, i, 0))] + [tok_spec(wh)] * 5,
        out_shape=[jax.ShapeDtypeStruct((nblk, tok, MXU_DIM_V7X), BF16)]
                  + [jax.ShapeDtypeStruct((tok, wh), BF16)] * 4 + [jax.ShapeDtypeStruct((tok, wh), F32)],
        compiler_params=_params("parallel"),
        name="inproj_a",
    )(x2, g, w_in, lb_logits)
    gate_blocks = 2 * dm // wh
    steps = tok // tm
    bf16_rows = 16
    assert all(w.shape[0] % (steps * bf16_rows) == 0 for w in to_cast)
    row_blocks = [pl.BlockSpec((w.shape[0] // steps, w.shape[1]), lambda i: (i, 0)) for w in to_cast]
    gs, gh, *casted = pl.pallas_call(
        functools.partial(_inproj_b_kernel, n_w=gate_blocks, n_cast=len(to_cast)),
        grid=(steps,),
        in_specs=[tok_spec(dm), _resident((1, dm))] + [cols(n_a // wh + n, wh) for n in range(gate_blocks)]
                 + row_blocks,
        out_specs=[tok_spec(dm)] * 2 + row_blocks,
        out_shape=[jax.ShapeDtypeStruct((tok, dm), BF16)] * 2
                  + [jax.ShapeDtypeStruct(w.shape, BF16) for w in to_cast],
        compiler_params=_params("parallel"),
        name="inproj_b",
    )(x2, g, *([w_in] * gate_blocks), *to_cast)
    return u, q, k, v, og, lf, gs, gh, casted


def _s5_prep(a_re, a_im, log_dt, b_re, b_im, c_re, c_im, d):
    groups = a_re.shape[0]
    nblk = groups // S5_BLOCK_GROUPS
    ns = S5_BLOCK_STATE
    tile_c = S5_BLOCK_GROUPS * S5_GROUP
    blk3 = lambda r, c: pl.BlockSpec((1, r, c), lambda j: (j, 0, 0))
    pw_rows = 24
    gb, p, c = S5_BLOCK_GROUPS, S5_STATE, S5_GROUP
    tiled = lambda t, r, w: t.astype(F32).reshape(nblk, r, w)
    return pl.pallas_call(
        _s5_prep_kernel,
        grid=(nblk,),
        in_specs=[blk3(gb * p, c), blk3(gb * p, c), blk3(gb * c, p), blk3(gb * c, p),
                  blk3(gb, p), blk3(gb, p), blk3(gb, 1), blk3(gb, c)],
        out_specs=[pl.BlockSpec((1, S5_SUB, tile_c, 2 * ns), lambda j: (j, 0, 0, 0)), blk3(2 * ns, tile_c),
                   pl.BlockSpec((1, CHUNK, tile_c, tile_c), lambda j: (j, 0, 0, 0)), blk3(pw_rows, 2 * ns)],
        out_shape=[jax.ShapeDtypeStruct((nblk, S5_SUB, tile_c, 2 * ns), BF16),
                   jax.ShapeDtypeStruct((nblk, 2 * ns, tile_c), BF16),
                   jax.ShapeDtypeStruct((nblk, CHUNK, tile_c, tile_c), BF16),
                   jax.ShapeDtypeStruct((nblk, pw_rows, 2 * ns), F32)],
        compiler_params=_params("parallel"),
        name="s5_prep",
    )(tiled(b_re, gb * p, c), tiled(b_im, gb * p, c), tiled(c_re, gb * c, p), tiled(c_im, gb * c, p),
      tiled(a_re, gb, p), tiled(a_im, gb, p), tiled(log_dt, gb, 1), tiled(d, gb, c))


def _s5_scan(u, bc, cc, klag, pw, *, seq):
    nblk, tok, tw = u.shape
    ns = S5_BLOCK_STATE
    rows_seq = seq // CHUNK
    rt = _tile(rows_seq, S5_ROW_TILE)
    per_tile = lambda a: pl.BlockSpec((None,) + a.shape[1:], lambda j, i: (j,) + (0,) * (a.ndim - 1))
    rows_spec = pl.BlockSpec((None, rt * CHUNK, tw), lambda j, i: (j, i, 0))
    return pl.pallas_call(
        functools.partial(_s5_scan_kernel, tiles_per_seq=rows_seq // rt),
        grid=(nblk, tok // (rt * CHUNK)),
        in_specs=[rows_spec, per_tile(bc), per_tile(cc), per_tile(klag), per_tile(pw)],
        out_specs=rows_spec,
        out_shape=jax.ShapeDtypeStruct(u.shape, BF16),
        scratch_shapes=[pltpu.VMEM((rt, 2 * ns), F32), pltpu.VMEM((8, 2 * ns), F32)],
        compiler_params=_params("parallel", "arbitrary"),
        name="s5_scan",
    )(u, bc, cc, klag, pw)


def _s5_tail(y, gs, w_glu, b_glu, w_proj):
    nblk, tok, tw = y.shape
    dm = gs.shape[1]
    tm = _tile(tok, TOKEN_TILE)
    tok_spec = lambda w: pl.BlockSpec((tm, w), lambda i: (i, 0))
    return pl.pallas_call(
        _s5_tail_kernel,
        grid=(tok // tm,),
        in_specs=[pl.BlockSpec((nblk, tm, tw), lambda i: (0, i, 0)), tok_spec(dm), _resident(w_glu.shape),
                  _resident(b_glu.shape), _resident(w_proj.shape)],
        out_specs=tok_spec(dm),
        out_shape=jax.ShapeDtypeStruct((tok, dm), BF16),
        compiler_params=_params("parallel"),
        name="s5_tail",
    )(y, gs, w_glu, b_glu, w_proj)


def _hgrn_merge(q, k, v, og, lf, gh, ms, x2, w_proj, w_out, norm_g, ffn_g, *, seq):
    tok, wh = q.shape
    dm = x2.shape[1]
    heads = wh // HEAD_DIM
    tl = _tile(seq, HGRN_TOKEN_TILE)
    lt = seq // tl
    seq_spec = lambda w: pl.BlockSpec((tl, w), lambda b, l: (b * lt + l, 0))
    return pl.pallas_call(
        functools.partial(_hgrn_kernel, heads=heads),
        grid=(tok // seq, lt),
        in_specs=[seq_spec(wh)] * 5 + [seq_spec(dm)] * 3
                 + [_resident(w_proj.shape), _resident(w_out.shape), _resident((1, wh)), _resident((1, dm))],
        out_specs=[seq_spec(dm)] * 2,
        out_shape=[jax.ShapeDtypeStruct((tok, dm), F32), jax.ShapeDtypeStruct((tok, dm), BF16)],
        scratch_shapes=[pltpu.VMEM((heads, HEAD_DIM, HEAD_DIM), F32),
                        pltpu.VMEM((tl, wh), BF16), pltpu.VMEM((tl, wh), BF16),
                        pltpu.VMEM((tl, wh), F32), pltpu.VMEM((tl, wh), F32), pltpu.VMEM((tl, wh), F32)],
        compiler_params=_params("arbitrary", "arbitrary"),
        name="hgrn_merge",
    )(q, k, v, og, lf, gh, ms, x2, w_proj, w_out, norm_g, ffn_g)


def _ffn(h2, x1, w_up, cw, cb, w_down, final_g, *, seq):
    tok, dm = h2.shape
    dff = w_down.shape[0]
    tf = _tile(dff, FFN_FF_TILE)
    nf = dff // tf
    tmf = _tile(seq, FFN_TOKEN_TILE)
    halo = 16
    return pl.pallas_call(
        functools.partial(_ffn_kernel, tiles_per_seq=seq // tmf),
        grid=(tok // tmf, nf),
        in_specs=[pl.BlockSpec((tmf, dm), lambda i, j: (i, 0)),
                  pl.BlockSpec((halo, dm), lambda i, j: (jnp.maximum(i * (tmf // halo) - 1, 0), 0)),
                  pl.BlockSpec((dm, tf), lambda i, j: (0, j)),
                  pl.BlockSpec((dm, tf), lambda i, j: (0, nf + j)),
                  pl.BlockSpec((CONV_WIDTH, tf), lambda i, j: (0, j)),
                  pl.BlockSpec((CONV_WIDTH, tf), lambda i, j: (0, nf + j)),
                  pl.BlockSpec((1, tf), lambda i, j: (0, j)),
                  pl.BlockSpec((1, tf), lambda i, j: (0, nf + j)),
                  pl.BlockSpec((tf, dm), lambda i, j: (j, 0)),
                  pl.BlockSpec((tmf, dm), lambda i, j: (i, 0), pipeline_mode=pl.Buffered(1)),
                  pl.BlockSpec((1, dm), lambda i, j: (0, 0))],
        out_specs=pl.BlockSpec((tmf, dm), lambda i, j: (i, 0), pipeline_mode=pl.Buffered(1)),
        out_shape=jax.ShapeDtypeStruct((tok, dm), F32),
        scratch_shapes=[pltpu.VMEM((halo + tmf, dm), BF16)],
        compiler_params=pltpu.CompilerParams(dimension_semantics=("parallel", "arbitrary"),
                                             vmem_limit_bytes=FFN_VMEM_LIMIT_BYTES),
        name="ffn",
    )(h2, h2, w_up, w_up, cw, cw, cb, cb, w_down, x1, final_g)


def kernel(x, ln_mix_g, w_in, s5_a_re, s5_a_im, s5_log_dt, s5_b_re, s5_b_im, s5_c_re, s5_c_im,
           s5_d, s5_w_glu, s5_b_glu, w_proj_s5, hgrn_lb_logits, hgrn_norm_g, w_proj_hgrn,
           w_out, ln_ffn_g, w_up, conv_w, conv_b, w_down, ln_final_g):
    bsz, seq, dm = x.shape
    assert ln_mix_g.shape[0] == 1, "single-layer block"
    tok = bsz * seq
    ws5 = s5_w_glu.shape[1]
    wh = hgrn_norm_g.shape[1]
    assert ws5 == wh and ws5 % MXU_DIM_V7X == 0 and seq % CHUNK == 0
    assert w_in.shape[2] == ws5 + 4 * wh + 2 * dm

    x2 = x.reshape(tok, dm)
    later_weights = [w.astype(F32) for w in (s5_w_glu[0], w_proj_s5[0], w_proj_hgrn[0], w_out[0], w_up[0], w_down[0])]
    u, q, k, v, og, lf, gs, gh, (w_glu_b, w_ps5_b, w_ph_b, w_out_b, w_up_b, w_down_b) = _inproj(
        x2, _row(ln_mix_g[0]), w_in[0].astype(BF16), hgrn_lb_logits.astype(F32), later_weights, wh=wh)

    bc, cc, klag, pw = _s5_prep(s5_a_re[0], s5_a_im[0], s5_log_dt[0], s5_b_re[0], s5_b_im[0],
                                s5_c_re[0], s5_c_im[0], s5_d[0])
    y = _s5_scan(u, bc, cc, klag, pw, seq=seq)
    ms = _s5_tail(y, gs, w_glu_b, _row(s5_b_glu[0]), w_ps5_b)

    x1, h2 = _hgrn_merge(q, k, v, og, lf, gh, ms, x2, w_ph_b, w_out_b,
                         _row(hgrn_norm_g[0]), _row(ln_ffn_g[0]), seq=seq)

    out = _ffn(h2, x1, w_up_b, conv_w[0].astype(F32), conv_b[0].astype(F32).reshape(1, -1),
               w_down_b, _row(ln_final_g), seq=seq)
    return out.reshape(bsz, seq, dm)
```

```python
import functools
import math

import jax
import jax.numpy as jnp
from jax import lax
from jax.experimental import pallas as pl
from jax.experimental.pallas import tpu as pltpu

F32 = jnp.float32
BF16 = jnp.bfloat16

RMS_EPS = 1e-6
S5_MAX_RE = -1e-4
S5_GROUP = 16
S5_STATE = 64
HEAD_DIM = 128
CONV_WIDTH = 3

LANES_V7X = 128
MXU_DIM_V7X = 256
CHUNK = 16
S5_SUB = 4
S5_BLOCK_GROUPS = MXU_DIM_V7X // S5_GROUP
S5_BLOCK_STATE = S5_BLOCK_GROUPS * S5_STATE
VMEM_LIMIT_BYTES = 56 * 1024 * 1024
FFN_VMEM_LIMIT_BYTES = 62 * 1024 * 1024
HGRN_FACTOR_LIMIT = 60.0
HGRN_WIDE_CHUNK = 64

TOKEN_TILE = 512
S5_TAIL_TOKEN_TILE = 1024
S5_ROW_TILE = 256
HGRN_TOKEN_TILE = 256
FFN_TOKEN_TILE = 1024
FFN_FF_TILE = 512


def _sigmoid(y):
    return 0.5 + 0.5 * jnp.tanh(0.5 * y)


def _rmsnorm(x, g):
    return x * lax.rsqrt(jnp.mean(x * x, axis=-1, keepdims=True) + RMS_EPS) * g


def _params(*sem):
    return pltpu.CompilerParams(dimension_semantics=sem, vmem_limit_bytes=VMEM_LIMIT_BYTES)


def _resident(shape):
    return pl.BlockSpec(shape, lambda *_: (0,) * len(shape), pipeline_mode=pl.Buffered(1))


def _inproj_a_kernel(x_ref, g_ref, w_ref, lbl_ref, u_ref, q_ref, k_ref, v_ref, og_ref, lf_ref, *, width):
    h = _rmsnorm(x_ref[...], g_ref[...]).astype(BF16)

    def proj(i):
        return jnp.dot(h, w_ref[:, i * width:(i + 1) * width], preferred_element_type=F32)

    a = proj(1)
    q_ref[...] = (a * _sigmoid(a)).astype(BF16)
    lbl = lbl_ref[...]
    e = jnp.exp(lbl - jnp.max(lbl, axis=0, keepdims=True))
    lb = e[0:1, :] / jnp.sum(e, axis=0, keepdims=True)
    sg = _sigmoid(proj(2))
    lf_ref[...] = jnp.log(lb + (1.0 - lb) * sg)
    k_ref[...] = ((1.0 - lb) * (1.0 - sg)).astype(BF16)
    a = proj(4)
    og_ref[...] = (a * _sigmoid(a)).astype(BF16)
    v_ref[...] = proj(3).astype(BF16)
    a = proj(0).astype(BF16)
    for j in range(u_ref.shape[0]):
        u_ref[j] = a[:, j * MXU_DIM_V7X:(j + 1) * MXU_DIM_V7X]


def _inproj_b_kernel(x_ref, g_ref, *refs, n_w, n_cast):
    w_refs, src_refs = refs[:n_w], refs[n_w:n_w + n_cast]
    o_refs, dst_refs = refs[n_w + n_cast:n_w + n_cast + 2], refs[n_w + n_cast + 2:]
    for src, dst in zip(src_refs, dst_refs):
        dst[...] = src[...].astype(BF16)
    per_out = len(w_refs) // len(o_refs)
    h = _rmsnorm(x_ref[...], g_ref[...]).astype(BF16)
    for n, w_ref in enumerate(w_refs):
        cols = w_ref.shape[1]
        lo = (n % per_out) * cols
        o_refs[n // per_out][:, lo:lo + cols] = _sigmoid(
            jnp.dot(h, w_ref[...], preferred_element_type=F32)).astype(BF16)


def _spread(x, reps):
    w = x.shape[1]
    r = lax.broadcasted_iota(jnp.int32, (w, reps * w), 0)
    c = lax.broadcasted_iota(jnp.int32, (w, reps * w), 1)
    sel = jnp.where(c % w == r, 1.0, 0.0).astype(BF16)
    out = jnp.zeros((x.shape[0], reps * w), F32)
    for piece in _split3(x):
        out = out + jnp.dot(piece, sel, preferred_element_type=F32)
    return out


def _own_group(x, row_div, lane_div):
    r = lax.broadcasted_iota(jnp.int32, x.shape, 0)
    c = lax.broadcasted_iota(jnp.int32, x.shape, 1)
    return jnp.where(r // row_div == c // lane_div, x, 0.0)


def _s5_prep_kernel(bre_ref, bim_ref, cre_ref, cim_ref, are_ref, aim_ref, ldt_ref, d_ref,
                    bc_ref, cc_ref, klag_ref, pw_ref):
    ns = S5_BLOCK_STATE
    gb, p, c = S5_BLOCK_GROUPS, S5_STATE, S5_GROUP
    b_tile = lambda ref: _own_group(_spread(ref[0], gb), p, c).T
    c_tile = lambda ref: _own_group(_spread(ref[0], gb), c, p).T
    per_lane = lambda x, w: jnp.sum(_own_group(x, 1, w), axis=0, keepdims=True)
    lam_re = jnp.minimum(per_lane(_spread(are_ref[0], gb), p), S5_MAX_RE)
    lam_im = per_lane(_spread(aim_ref[0], gb), p)
    dt = jnp.exp(per_lane(jnp.broadcast_to(ldt_ref[0], (gb, ns)), p))
    d_row = per_lane(_spread(d_ref[0], gb), c)
    mag = jnp.exp(lam_re * dt)
    ab_re = mag * jnp.cos(lam_im * dt)
    ab_im = mag * jnp.sin(lam_im * dt)
    den = lam_re * lam_re + lam_im * lam_im
    nr = ab_re - 1.0
    ni = ab_im
    cf_re = (nr * lam_re + ni * lam_im) / den
    cf_im = (ni * lam_re - nr * lam_im) / den
    bre = b_tile(bre_ref)
    bim = b_tile(bim_ref)
    b_re = cf_re * bre - cf_im * bim
    b_im = cf_re * bim + cf_im * bre
    cre = c_tile(cre_ref).astype(BF16)
    cim = c_tile(cim_ref).astype(BF16)
    cc_ref[0] = jnp.concatenate([cre, -cim], axis=0)

    p_re = jnp.ones_like(ab_re)
    p_im = jnp.zeros_like(ab_re)
    pw_ref[...] = jnp.zeros_like(pw_ref)
    for n in range(CHUNK + 1):
        pw_ref[0, n:n + 1, :] = jnp.concatenate([p_re, p_im], axis=-1)
        if n < S5_SUB:
            bc_ref[0, S5_SUB - 1 - n] = jnp.concatenate(
                [b_re * p_re - b_im * p_im, b_re * p_im + b_im * p_re], axis=-1).astype(BF16)
        if n < CHUNK:
            t_re = (b_re * p_re - b_im * p_im).astype(BF16)
            t_im = (b_re * p_im + b_im * p_re).astype(BF16)
            tap = (jnp.dot(t_re, cre, preferred_element_type=F32)
                   - jnp.dot(t_im, cim, preferred_element_type=F32))
            if n == 0:
                r = lax.broadcasted_iota(jnp.int32, tap.shape, 0)
                c = lax.broadcasted_iota(jnp.int32, tap.shape, 1)
                tap = tap + jnp.where(r == c, d_row, 0.0)
            klag_ref[0, n] = tap.astype(BF16)
        p_re, p_im = p_re * ab_re - p_im * ab_im, p_re * ab_im + p_im * ab_re


def _s5_scan_kernel(u_ref, bc_ref, cc_ref, klag_ref, pw_ref, y_ref, z_ref, carry_ref, *, tiles_per_seq):
    ns = S5_BLOCK_STATE
    tw = MXU_DIM_V7X
    rt = u_ref.shape[0] // CHUNK
    by_pos = jnp.swapaxes(u_ref[...].reshape(rt, CHUNK, tw), 0, 1)

    def u_tile(s):
        return by_pos[s]

    def pw(n):
        return pw_ref[n:n + 1, 0:ns], pw_ref[n:n + 1, ns:2 * ns]

    @pl.when(pl.program_id(1) % tiles_per_seq == 0)
    def _():
        carry_ref[...] = jnp.zeros_like(carry_ref)

    acc_re = jnp.zeros((rt, ns), F32)
    acc_im = jnp.zeros((rt, ns), F32)
    for sub in range(CHUNK // S5_SUB):
        bb = jnp.dot(u_tile(sub * S5_SUB), bc_ref[0], preferred_element_type=F32)
        for r in range(1, S5_SUB):
            bb = bb + jnp.dot(u_tile(sub * S5_SUB + r), bc_ref[r], preferred_element_type=F32)
        p_re, p_im = pw(CHUNK - S5_SUB * (sub + 1))
        acc_re = acc_re + bb[:, :ns] * p_re - bb[:, ns:] * p_im
        acc_im = acc_im + bb[:, :ns] * p_im + bb[:, ns:] * p_re
    z_ref[:, 0:ns] = acc_re
    z_ref[:, ns:2 * ns] = acc_im

    a_re, a_im = pw(CHUNK)

    def body(r, carry):
        s_re, s_im = carry
        z_re = z_ref[pl.ds(r, 1), 0:ns]
        z_im = z_ref[pl.ds(r, 1), ns:2 * ns]
        z_ref[pl.ds(r, 1), 0:ns] = s_re
        z_ref[pl.ds(r, 1), ns:2 * ns] = s_im
        return (a_re * s_re - a_im * s_im + z_re, a_re * s_im + a_im * s_re + z_im)

    s_re, s_im = lax.fori_loop(0, rt, body, (carry_ref[0:1, 0:ns], carry_ref[0:1, ns:2 * ns]))
    carry_ref[0:1, 0:ns] = s_re
    carry_ref[0:1, ns:2 * ns] = s_im

    s_re = z_ref[:, 0:ns]
    s_im = z_ref[:, ns:2 * ns]
    ys = []
    for t in range(CHUNK):
        y = jnp.dot(u_tile(t), klag_ref[0], preferred_element_type=F32)
        for s in range(t):
            y = y + jnp.dot(u_tile(s), klag_ref[t - s], preferred_element_type=F32)
        p_re, p_im = pw(t + 1)
        rot = jnp.concatenate([s_re * p_re - s_im * p_im, s_re * p_im + s_im * p_re], axis=-1)
        y = y + jnp.dot(rot.astype(BF16), cc_ref[...], preferred_element_type=F32)
        ys.append(y.astype(BF16))
    y_ref[...] = jnp.swapaxes(jnp.stack(ys, axis=0), 0, 1).reshape(rt * CHUNK, tw)


def _s5_tail_kernel(y_ref, gs_ref, wg_ref, bg_ref, wp_ref, o_ref):
    y = jnp.concatenate([y_ref[j] for j in range(y_ref.shape[0])], axis=-1).astype(F32)
    z = 0.5 * y * (1.0 + jnp.tanh(math.sqrt(2.0 / math.pi) * (y + 0.044715 * (y * y * y))))
    gl = jnp.dot(z.astype(BF16), wg_ref[...], preferred_element_type=F32) + bg_ref[...]
    zz = (z * _sigmoid(gl)).astype(BF16)
    ys = jnp.dot(zz, wp_ref[...], preferred_element_type=F32)
    o_ref[...] = (gs_ref[...].astype(F32) * ys).astype(BF16)


def _split3(x):
    hi = x.astype(BF16)
    r = x - hi.astype(F32)
    mid = r.astype(BF16)
    lo = (r - mid.astype(F32)).astype(BF16)
    return hi, mid, lo


def _hgrn_kernel(q_ref, k_ref, v_ref, og_ref, lf_ref, gh_ref, ms_ref, x_ref,
                 wph_ref, wout_ref, ng_ref, g2_ref, x1_ref, h2_ref,
                 st_ref, qt_ref, kh_ref, b_ref, gt_ref, o_ref, *, heads):
    tl = q_ref.shape[0]
    hd = HEAD_DIM

    @pl.when(pl.program_id(1) == 0)
    def _():
        st_ref[...] = jnp.zeros_like(st_ref)

    r = lax.broadcasted_iota(jnp.int32, (tl, tl), 0)
    c = lax.broadcasted_iota(jnp.int32, (tl, tl), 1)
    pieces = _split3(lf_ref[...])

    def chunk_sums(chunk):
        causal = jnp.logical_and((r // chunk) == (c // chunk), c <= r)
        tri = jnp.where(causal, 1.0, 0.0).astype(BF16)
        b = jnp.zeros((tl, heads * hd), F32)
        for piece in pieces:
            b = b + jnp.dot(tri, piece, preferred_element_type=F32)
        last = b.reshape(tl // chunk, chunk, heads * hd)[:, chunk - 1:chunk, :]
        bt = jnp.broadcast_to(last, (tl // chunk, chunk, heads * hd)).reshape(tl, heads * hd)
        return causal, b, bt

    def store_decayed(b, bt):
        b_ref[...] = b
        gt_ref[...] = jnp.exp(bt)
        qt_ref[...] = (q_ref[...].astype(F32) * jnp.exp(b)).astype(BF16)
        kh_ref[...] = (k_ref[...].astype(F32) * jnp.exp(bt - b)).astype(BF16)

    half = CHUNK // 2
    row = lax.broadcasted_iota(jnp.int32, (half, 1), 0)

    def pairs_exact(rows, lanes):
        q16 = q_ref[rows, lanes].astype(F32)
        k16 = k_ref[rows, lanes].astype(F32)
        v16 = v_ref[rows, lanes].astype(F32)
        b16 = b_ref[rows, lanes]
        q_top, q_bot = q16[:half], q16[half:]
        b_top, b_bot = b16[:half], b16[half:]
        acc_top = jnp.zeros((half, hd), F32)
        acc_bot = jnp.zeros((half, hd), F32)
        for s in range(CHUNK):
            bs = b16[s:s + 1]
            ks = k16[s:s + 1]
            vs = v16[s:s + 1]
            if s < half:
                w = jnp.sum(q_top * ks * jnp.exp(jnp.minimum(b_top - bs, 0.0)), axis=-1, keepdims=True)
                acc_top = acc_top + jnp.where(row >= s, w, 0.0) * vs
                w = jnp.sum(q_bot * ks * jnp.exp(b_bot - bs), axis=-1, keepdims=True)
                acc_bot = acc_bot + w * vs
            else:
                w = jnp.sum(q_bot * ks * jnp.exp(jnp.minimum(b_bot - bs, 0.0)), axis=-1, keepdims=True)
                acc_bot = acc_bot + jnp.where(row + half >= s, w, 0.0) * vs
        return jnp.concatenate([acc_top, acc_bot], axis=0)

    def carry_state(rows, first_row, h):
        lanes = slice(h * hd, (h + 1) * hd)
        st = st_ref[h]
        inter = lax.dot_general(qt_ref[rows, lanes], st.astype(BF16),
                                (((1,), (1,)), ((), ())), preferred_element_type=F32)
        upd = lax.dot_general(v_ref[rows, lanes], kh_ref[rows, lanes],
                              (((0,), (0,)), ((), ())), preferred_element_type=F32)
        st_ref[h] = gt_ref[first_row, lanes] * st + upd
        return inter

    def exact_step(n, _):
        r0 = pl.multiple_of(n * CHUNK, CHUNK)
        rows = pl.ds(r0, CHUNK)
        for h in range(heads):
            lanes = slice(h * hd, (h + 1) * hd)
            o_ref[rows, lanes] = carry_state(rows, pl.ds(r0, 1), h) + pairs_exact(rows, lanes)
        return 0

    causal, b, bt = chunk_sums(HGRN_WIDE_CHUNK)
    worst = jnp.max(jnp.max(-bt, axis=0, keepdims=True), axis=1, keepdims=True)[0, 0]
    factorable = worst <= HGRN_FACTOR_LIMIT

    @pl.when(factorable)
    def _():
        store_decayed(b, bt)
        kx = (k_ref[...].astype(F32) * jnp.exp(-b)).astype(BF16)
        for h in range(heads):
            lanes = slice(h * hd, (h + 1) * hd)
            sc = lax.dot_general(qt_ref[:, lanes], kx[:, lanes], (((1,), (1,)), ((), ())),
                                 preferred_element_type=F32)
            o_ref[:, lanes] = jnp.dot(jnp.where(causal, sc, 0.0).astype(BF16), v_ref[:, lanes],
                                      preferred_element_type=F32)
        for n in range(tl // HGRN_WIDE_CHUNK):
            rows = slice(n * HGRN_WIDE_CHUNK, (n + 1) * HGRN_WIDE_CHUNK)
            for h in range(heads):
                lanes = slice(h * hd, (h + 1) * hd)
                o_ref[rows, lanes] += carry_state(rows, slice(rows.start, rows.start + 1), h)

    @pl.when(jnp.logical_not(factorable))
    def _():
        store_decayed(*chunk_sums(CHUNK)[1:])
        lax.fori_loop(0, tl // CHUNK, exact_step, 0)

    parts = []
    for h in range(heads):
        lanes = slice(h * hd, (h + 1) * hd)
        o = o_ref[:, lanes]
        o = o * lax.rsqrt(jnp.mean(o * o, axis=-1, keepdims=True) + RMS_EPS)
        parts.append((o * ng_ref[:, lanes] * og_ref[:, lanes].astype(F32)).astype(BF16))
    on = jnp.concatenate(parts, axis=-1)
    yh = jnp.dot(on, wph_ref[...], preferred_element_type=F32)
    merged = ms_ref[...].astype(F32) + gh_ref[...].astype(F32) * yh
    x1 = x_ref[...] + jnp.dot(merged.astype(BF16), wout_ref[...], preferred_element_type=F32)
    x1_ref[...] = x1
    h2_ref[...] = _rmsnorm(x1, g2_ref[...]).astype(BF16)


def _ffn_kernel(h_ref, halo_ref, wug_ref, wuv_ref, cwg_ref, cwv_ref, cbg_ref, cbv_ref, wd_ref,
                x1_ref, gf_ref, o_ref, hcat_ref, *, tiles_per_seq):
    i = pl.program_id(0)
    j = pl.program_id(1)
    tm = h_ref.shape[0]
    nh = halo_ref.shape[0]

    @pl.when(j == 0)
    def _():
        keep = jnp.where(i % tiles_per_seq == 0, 0.0, 1.0).astype(BF16)
        hcat_ref[0:nh] = halo_ref[...] * keep
        hcat_ref[nh:nh + tm] = h_ref[...]
        o_ref[...] = jnp.zeros_like(o_ref)

    def conv_up(w_ref, cw_ref, cb_ref):
        ext = jnp.dot(hcat_ref[...], w_ref[...], preferred_element_type=F32)
        out = cb_ref[...]
        for tap in range(CONV_WIDTH):
            back = CONV_WIDTH - 1 - tap
            out = out + cw_ref[tap:tap + 1, :] * ext[nh - back:nh - back + tm]
        return out

    gate = conv_up(wug_ref, cwg_ref, cbg_ref)
    val = conv_up(wuv_ref, cwv_ref, cbv_ref)
    act = (0.5 * gate * (1.0 + jnp.tanh(0.5 * gate)) * val).astype(BF16)
    o_ref[...] += jnp.dot(act, wd_ref[...], preferred_element_type=F32)

    @pl.when(j == pl.num_programs(1) - 1)
    def _():
        o_ref[...] = _rmsnorm(x1_ref[...] + o_ref[...], gf_ref[...])


def _tile(n, want):
    t = min(n, want)
    assert n % t == 0, (n, want)
    return t


def _row(v):
    return v.reshape(1, -1).astype(F32)


def _inproj(x2, g, w_in, lb_logits, to_cast, *, wh):
    tok, dm = x2.shape
    tm = _tile(tok, TOKEN_TILE)
    nblk = wh // MXU_DIM_V7X
    n_a = 5 * wh
    assert w_in.shape[1] == n_a + 2 * dm and dm % wh == 0
    tok_spec = lambda w: pl.BlockSpec((tm, w), lambda i: (i, 0))
    cols = lambda n, width: pl.BlockSpec((dm, width), lambda i: (0, n), pipeline_mode=pl.Buffered(1))
    u, q, k, v, og, lf = pl.pallas_call(
        functools.partial(_inproj_a_kernel, width=wh),
        grid=(tok // tm,),
        in_specs=[tok_spec(dm), _resident((1, dm)), cols(0, n_a), _resident(lb_logits.shape)],
        out_specs=[pl.BlockSpec((nblk, tm, MXU_DIM_V7X), lambda i: (0, i, 0))] + [tok_spec(wh)] * 5,
        out_shape=[jax.ShapeDtypeStruct((nblk, tok, MXU_DIM_V7X), BF16)]
                  + [jax.ShapeDtypeStruct((tok, wh), BF16)] * 4 + [jax.ShapeDtypeStruct((tok, wh), F32)],
        compiler_params=_params("parallel"),
        name="inproj_a",
    )(x2, g, w_in, lb_logits)
    gate_blocks = 2 * dm // wh
    steps = tok // tm
    bf16_rows = 16
    assert all(w.shape[0] % (steps * bf16_rows) == 0 for w in to_cast)
    row_blocks = [pl.BlockSpec((w.shape[0] // steps, w.shape[1]), lambda i: (i, 0)) for w in to_cast]
    gs, gh, *casted = pl.pallas_call(
        functools.partial(_inproj_b_kernel, n_w=gate_blocks, n_cast=len(to_cast)),
        grid=(steps,),
        in_specs=[tok_spec(dm), _resident((1, dm))] + [cols(n_a // wh + n, wh) for n in range(gate_blocks)]
                 + row_blocks,
        out_specs=[tok_spec(dm)] * 2 + row_blocks,
        out_shape=[jax.ShapeDtypeStruct((tok, dm), BF16)] * 2
                  + [jax.ShapeDtypeStruct(w.shape, BF16) for w in to_cast],
        compiler_params=_params("parallel"),
        name="inproj_b",
    )(x2, g, *([w_in] * gate_blocks), *to_cast)
    return u, q, k, v, og, lf, gs, gh, casted


def _s5_prep(a_re, a_im, log_dt, b_re, b_im, c_re, c_im, d):
    groups = a_re.shape[0]
    nblk = groups // S5_BLOCK_GROUPS
    ns = S5_BLOCK_STATE
    tile_c = S5_BLOCK_GROUPS * S5_GROUP
    blk3 = lambda r, c: pl.BlockSpec((1, r, c), lambda j: (j, 0, 0))
    pw_rows = 24
    gb, p, c = S5_BLOCK_GROUPS, S5_STATE, S5_GROUP
    tiled = lambda t, r, w: t.astype(F32).reshape(nblk, r, w)
    return pl.pallas_call(
        _s5_prep_kernel,
        grid=(nblk,),
        in_specs=[blk3(gb * p, c), blk3(gb * p, c), blk3(gb * c, p), blk3(gb * c, p),
                  blk3(gb, p), blk3(gb, p), blk3(gb, 1), blk3(gb, c)],
        out_specs=[pl.BlockSpec((1, S5_SUB, tile_c, 2 * ns), lambda j: (j, 0, 0, 0)), blk3(2 * ns, tile_c),
                   pl.BlockSpec((1, CHUNK, tile_c, tile_c), lambda j: (j, 0, 0, 0)), blk3(pw_rows, 2 * ns)],
        out_shape=[jax.ShapeDtypeStruct((nblk, S5_SUB, tile_c, 2 * ns), BF16),
                   jax.ShapeDtypeStruct((nblk, 2 * ns, tile_c), BF16),
                   jax.ShapeDtypeStruct((nblk, CHUNK, tile_c, tile_c), BF16),
                   jax.ShapeDtypeStruct((nblk, pw_rows, 2 * ns), F32)],
        compiler_params=_params("parallel"),
        name="s5_prep",
    )(tiled(b_re, gb * p, c), tiled(b_im, gb * p, c), tiled(c_re, gb * c, p), tiled(c_im, gb * c, p),
      tiled(a_re, gb, p), tiled(a_im, gb, p), tiled(log_dt, gb, 1), tiled(d, gb, c))


def _s5_scan(u, bc, cc, klag, pw, *, seq):
    nblk, tok, tw = u.shape
    ns = S5_BLOCK_STATE
    rows_seq = seq // CHUNK
    rt = _tile(rows_seq, S5_ROW_TILE)
    per_tile = lambda a: pl.BlockSpec((None,) + a.shape[1:], lambda j, i: (j,) + (0,) * (a.ndim - 1))
    rows_spec = pl.BlockSpec((None, rt * CHUNK, tw), lambda j, i: (j, i, 0))
    return pl.pallas_call(
        functools.partial(_s5_scan_kernel, tiles_per_seq=rows_seq // rt),
        grid=(nblk, tok // (rt * CHUNK)),
        in_specs=[rows_spec, per_tile(bc), per_tile(cc), per_tile(klag), per_tile(pw)],
        out_specs=rows_spec,
        out_shape=jax.ShapeDtypeStruct(u.shape, BF16),
        scratch_shapes=[pltpu.VMEM((rt, 2 * ns), F32), pltpu.VMEM((8, 2 * ns), F32)],
        compiler_params=_params("parallel", "arbitrary"),
        name="s5_scan",
    )(u, bc, cc, klag, pw)


def _s5_tail(y, gs, w_glu, b_glu, w_proj):
    nblk, tok, tw = y.shape
    dm = gs.shape[1]
    tm = _tile(tok, S5_TAIL_TOKEN_TILE)
    tok_spec = lambda w: pl.BlockSpec((tm, w), lambda i: (i, 0))
    return pl.pallas_call(
        _s5_tail_kernel,
        grid=(tok // tm,),
        in_specs=[pl.BlockSpec((nblk, tm, tw), lambda i: (0, i, 0)), tok_spec(dm), _resident(w_glu.shape),
                  _resident(b_glu.shape), _resident(w_proj.shape)],
        out_specs=tok_spec(dm),
        out_shape=jax.ShapeDtypeStruct((tok, dm), BF16),
        compiler_params=_params("parallel"),
        name="s5_tail",
    )(y, gs, w_glu, b_glu, w_proj)


def _hgrn_merge(q, k, v, og, lf, gh, ms, x2, w_proj, w_out, norm_g, ffn_g, *, seq):
    tok, wh = q.shape
    dm = x2.shape[1]
    heads = wh // HEAD_DIM
    tl = _tile(seq, HGRN_TOKEN_TILE)
    lt = seq // tl
    seq_spec = lambda w: pl.BlockSpec((tl, w), lambda b, l: (b * lt + l, 0))
    return pl.pallas_call(
        functools.partial(_hgrn_kernel, heads=heads),
        grid=(tok // seq, lt),
        in_specs=[seq_spec(wh)] * 5 + [seq_spec(dm)] * 3
                 + [_resident(w_proj.shape), _resident(w_out.shape), _resident((1, wh)), _resident((1, dm))],
        out_specs=[seq_spec(dm)] * 2,
        out_shape=[jax.ShapeDtypeStruct((tok, dm), F32), jax.ShapeDtypeStruct((tok, dm), BF16)],
        scratch_shapes=[pltpu.VMEM((heads, HEAD_DIM, HEAD_DIM), F32),
                        pltpu.VMEM((tl, wh), BF16), pltpu.VMEM((tl, wh), BF16),
                        pltpu.VMEM((tl, wh), F32), pltpu.VMEM((tl, wh), F32), pltpu.VMEM((tl, wh), F32)],
        compiler_params=_params("arbitrary", "arbitrary"),
        name="hgrn_merge",
    )(q, k, v, og, lf, gh, ms, x2, w_proj, w_out, norm_g, ffn_g)


def _ffn(h2, x1, w_up, cw, cb, w_down, final_g, *, seq):
    tok, dm = h2.shape
    dff = w_down.shape[0]
    tf = _tile(dff, FFN_FF_TILE)
    nf = dff // tf
    tmf = _tile(seq, FFN_TOKEN_TILE)
    halo = 16
    return pl.pallas_call(
        functools.partial(_ffn_kernel, tiles_per_seq=seq // tmf),
        grid=(tok // tmf, nf),
        in_specs=[pl.BlockSpec((tmf, dm), lambda i, j: (i, 0)),
                  pl.BlockSpec((halo, dm), lambda i, j: (jnp.maximum(i * (tmf // halo) - 1, 0), 0)),
                  pl.BlockSpec((dm, tf), lambda i, j: (0, j)),
                  pl.BlockSpec((dm, tf), lambda i, j: (0, nf + j)),
                  pl.BlockSpec((CONV_WIDTH, tf), lambda i, j: (0, j)),
                  pl.BlockSpec((CONV_WIDTH, tf), lambda i, j: (0, nf + j)),
                  pl.BlockSpec((1, tf), lambda i, j: (0, j)),
                  pl.BlockSpec((1, tf), lambda i, j: (0, nf + j)),
                  pl.BlockSpec((tf, dm), lambda i, j: (j, 0)),
                  pl.BlockSpec((tmf, dm), lambda i, j: (i, 0), pipeline_mode=pl.Buffered(1)),
                  pl.BlockSpec((1, dm), lambda i, j: (0, 0))],
        out_specs=pl.BlockSpec((tmf, dm), lambda i, j: (i, 0), pipeline_mode=pl.Buffered(1)),
        out_shape=jax.ShapeDtypeStruct((tok, dm), F32),
        scratch_shapes=[pltpu.VMEM((halo + tmf, dm), BF16)],
        compiler_params=pltpu.CompilerParams(dimension_semantics=("parallel", "arbitrary"),
                                             vmem_limit_bytes=FFN_VMEM_LIMIT_BYTES),
        name="ffn",
    )(h2, h2, w_up, w_up, cw, cw, cb, cb, w_down, x1, final_g)


def kernel(x, ln_mix_g, w_in, s5_a_re, s5_a_im, s5_log_dt, s5_b_re, s5_b_im, s5_c_re, s5_c_im,
           s5_d, s5_w_glu, s5_b_glu, w_proj_s5, hgrn_lb_logits, hgrn_norm_g, w_proj_hgrn,
           w_out, ln_ffn_g, w_up, conv_w, conv_b, w_down, ln_final_g):
    bsz, seq, dm = x.shape
    assert ln_mix_g.shape[0] == 1, "single-layer block"
    tok = bsz * seq
    ws5 = s5_w_glu.shape[1]
    wh = hgrn_norm_g.shape[1]
    assert ws5 == wh and ws5 % MXU_DIM_V7X == 0 and seq % CHUNK == 0
    assert w_in.shape[2] == ws5 + 4 * wh + 2 * dm

    x2 = x.reshape(tok, dm)
    later_weights = [w.astype(F32) for w in (s5_w_glu[0], w_proj_s5[0], w_proj_hgrn[0], w_out[0], w_up[0], w_down[0])]
    u, q, k, v, og, lf, gs, gh, (w_glu_b, w_ps5_b, w_ph_b, w_out_b, w_up_b, w_down_b) = _inproj(
        x2, _row(ln_mix_g[0]), w_in[0].astype(BF16), hgrn_lb_logits.astype(F32), later_weights, wh=wh)

    bc, cc, klag, pw = _s5_prep(s5_a_re[0], s5_a_im[0], s5_log_dt[0], s5_b_re[0], s5_b_im[0],
                                s5_c_re[0], s5_c_im[0], s5_d[0])
    y = _s5_scan(u, bc, cc, klag, pw, seq=seq)
    ms = _s5_tail(y, gs, w_glu_b, _row(s5_b_glu[0]), w_ps5_b)

    x1, h2 = _hgrn_merge(q, k, v, og, lf, gh, ms, x2, w_ph_b, w_out_b,
                         _row(hgrn_norm_g[0]), _row(ln_ffn_g[0]), seq=seq)

    out = _ffn(h2, x1, w_up_b, conv_w[0].astype(F32), conv_b[0].astype(F32).reshape(1, -1),
               w_down_b, _row(ln_final_g), seq=seq)
    return out.reshape(bsz, seq, dm)
```

```python
import functools
import math

import jax
import jax.numpy as jnp
from jax import lax
from jax.experimental import pallas as pl
from jax.experimental.pallas import tpu as pltpu

F32 = jnp.float32
BF16 = jnp.bfloat16

RMS_EPS = 1e-6
S5_MAX_RE = -1e-4
S5_GROUP = 16
S5_STATE = 64
HEAD_DIM = 128
CONV_WIDTH = 3

LANES_V7X = 128
MXU_DIM_V7X = 256
CHUNK = 16
S5_SUB = 4
S5_BLOCK_GROUPS = MXU_DIM_V7X // S5_GROUP
S5_BLOCK_STATE = S5_BLOCK_GROUPS * S5_STATE
VMEM_LIMIT_BYTES = 56 * 1024 * 1024
FFN_VMEM_LIMIT_BYTES = 62 * 1024 * 1024
HGRN_FACTOR_LIMIT = 60.0
HGRN_WIDE_CHUNK = 64

TOKEN_TILE = 512
S5_TAIL_TOKEN_TILE = 1024
S5_ROW_TILE = 256
HGRN_TOKEN_TILE = 256
FFN_TOKEN_TILE = 1024
FFN_FF_TILE = 512


def _sigmoid(y):
    return 0.5 + 0.5 * jnp.tanh(0.5 * y)


def _rmsnorm(x, g):
    return x * lax.rsqrt(jnp.mean(x * x, axis=-1, keepdims=True) + RMS_EPS) * g


def _params(*sem):
    return pltpu.CompilerParams(dimension_semantics=sem, vmem_limit_bytes=VMEM_LIMIT_BYTES)


def _resident(shape):
    return pl.BlockSpec(shape, lambda *_: (0,) * len(shape), pipeline_mode=pl.Buffered(1))


def _inproj_a_kernel(x_ref, g_ref, w_ref, lbl_ref, u_ref, q_ref, k_ref, v_ref, og_ref, lf_ref, *, width):
    h = _rmsnorm(x_ref[...], g_ref[...]).astype(BF16)

    def proj(i):
        return jnp.dot(h, w_ref[:, i * width:(i + 1) * width], preferred_element_type=F32)

    a = proj(1)
    q_ref[...] = (a * _sigmoid(a)).astype(BF16)
    lbl = lbl_ref[...]
    e = jnp.exp(lbl - jnp.max(lbl, axis=0, keepdims=True))
    lb = e[0:1, :] / jnp.sum(e, axis=0, keepdims=True)
    sg = _sigmoid(proj(2))
    lf_ref[...] = jnp.log(lb + (1.0 - lb) * sg)
    k_ref[...] = ((1.0 - lb) * (1.0 - sg)).astype(BF16)
    a = proj(4)
    og_ref[...] = (a * _sigmoid(a)).astype(BF16)
    v_ref[...] = proj(3).astype(BF16)
    a = proj(0).astype(BF16)
    for j in range(u_ref.shape[0]):
        u_ref[j] = a[:, j * MXU_DIM_V7X:(j + 1) * MXU_DIM_V7X]


def _inproj_b_kernel(x_ref, g_ref, *refs, n_w, n_cast):
    w_refs, src_refs = refs[:n_w], refs[n_w:n_w + n_cast]
    o_refs, dst_refs = refs[n_w + n_cast:n_w + n_cast + 2], refs[n_w + n_cast + 2:]
    for src, dst in zip(src_refs, dst_refs):
        dst[...] = src[...].astype(BF16)
    per_out = len(w_refs) // len(o_refs)
    h = _rmsnorm(x_ref[...], g_ref[...]).astype(BF16)
    for n, w_ref in enumerate(w_refs):
        cols = w_ref.shape[1]
        lo = (n % per_out) * cols
        o_refs[n // per_out][:, lo:lo + cols] = _sigmoid(
            jnp.dot(h, w_ref[...], preferred_element_type=F32)).astype(BF16)


def _spread(x, reps):
    w = x.shape[1]
    r = lax.broadcasted_iota(jnp.int32, (w, reps * w), 0)
    c = lax.broadcasted_iota(jnp.int32, (w, reps * w), 1)
    sel = jnp.where(c % w == r, 1.0, 0.0).astype(BF16)
    out = jnp.zeros((x.shape[0], reps * w), F32)
    for piece in _split3(x):
        out = out + jnp.dot(piece, sel, preferred_element_type=F32)
    return out


def _own_group(x, row_div, lane_div):
    r = lax.broadcasted_iota(jnp.int32, x.shape, 0)
    c = lax.broadcasted_iota(jnp.int32, x.shape, 1)
    return jnp.where(r // row_div == c // lane_div, x, 0.0)


def _s5_prep_kernel(bre_ref, bim_ref, cre_ref, cim_ref, are_ref, aim_ref, ldt_ref, d_ref,
                    bc_ref, cc_ref, klag_ref, pw_ref):
    ns = S5_BLOCK_STATE
    gb, p, c = S5_BLOCK_GROUPS, S5_STATE, S5_GROUP
    b_tile = lambda ref: _own_group(_spread(ref[0], gb), p, c).T
    c_tile = lambda ref: _own_group(_spread(ref[0], gb), c, p).T
    per_lane = lambda x, w: jnp.sum(_own_group(x, 1, w), axis=0, keepdims=True)
    lam_re = jnp.minimum(per_lane(_spread(are_ref[0], gb), p), S5_MAX_RE)
    lam_im = per_lane(_spread(aim_ref[0], gb), p)
    dt = jnp.exp(per_lane(jnp.broadcast_to(ldt_ref[0], (gb, ns)), p))
    d_row = per_lane(_spread(d_ref[0], gb), c)
    mag = jnp.exp(lam_re * dt)
    ab_re = mag * jnp.cos(lam_im * dt)
    ab_im = mag * jnp.sin(lam_im * dt)
    den = lam_re * lam_re + lam_im * lam_im
    nr = ab_re - 1.0
    ni = ab_im
    cf_re = (nr * lam_re + ni * lam_im) / den
    cf_im = (ni * lam_re - nr * lam_im) / den
    bre = b_tile(bre_ref)
    bim = b_tile(bim_ref)
    b_re = cf_re * bre - cf_im * bim
    b_im = cf_re * bim + cf_im * bre
    cre = c_tile(cre_ref).astype(BF16)
    cim = c_tile(cim_ref).astype(BF16)
    cc_ref[0] = jnp.concatenate([cre, -cim], axis=0)

    p_re = jnp.ones_like(ab_re)
    p_im = jnp.zeros_like(ab_re)
    pw_ref[...] = jnp.zeros_like(pw_ref)
    for n in range(CHUNK + 1):
        pw_ref[0, n:n + 1, :] = jnp.concatenate([p_re, p_im], axis=-1)
        if n < S5_SUB:
            bc_ref[0, S5_SUB - 1 - n] = jnp.concatenate(
                [b_re * p_re - b_im * p_im, b_re * p_im + b_im * p_re], axis=-1).astype(BF16)
        if n < CHUNK:
            t_re = (b_re * p_re - b_im * p_im).astype(BF16)
            t_im = (b_re * p_im + b_im * p_re).astype(BF16)
            tap = (jnp.dot(t_re, cre, preferred_element_type=F32)
                   - jnp.dot(t_im, cim, preferred_element_type=F32))
            if n == 0:
                r = lax.broadcasted_iota(jnp.int32, tap.shape, 0)
                c = lax.broadcasted_iota(jnp.int32, tap.shape, 1)
                tap = tap + jnp.where(r == c, d_row, 0.0)
            klag_ref[0, n] = tap.astype(BF16)
        p_re, p_im = p_re * ab_re - p_im * ab_im, p_re * ab_im + p_im * ab_re


def _s5_scan_kernel(u_ref, bc_ref, cc_ref, klag_ref, pw_ref, y_ref, z_ref, carry_ref, *, tiles_per_seq):
    ns = S5_BLOCK_STATE
    tw = MXU_DIM_V7X
    rt = u_ref.shape[0] // CHUNK
    by_pos = jnp.swapaxes(u_ref[...].reshape(rt, CHUNK, tw), 0, 1)

    def u_tile(s):
        return by_pos[s]

    def pw(n):
        return pw_ref[n:n + 1, 0:ns], pw_ref[n:n + 1, ns:2 * ns]

    @pl.when(pl.program_id(1) % tiles_per_seq == 0)
    def _():
        carry_ref[...] = jnp.zeros_like(carry_ref)

    acc_re = jnp.zeros((rt, ns), F32)
    acc_im = jnp.zeros((rt, ns), F32)
    for sub in range(CHUNK // S5_SUB):
        bb = jnp.dot(u_tile(sub * S5_SUB), bc_ref[0], preferred_element_type=F32)
        for r in range(1, S5_SUB):
            bb = bb + jnp.dot(u_tile(sub * S5_SUB + r), bc_ref[r], preferred_element_type=F32)
        p_re, p_im = pw(CHUNK - S5_SUB * (sub + 1))
        acc_re = acc_re + bb[:, :ns] * p_re - bb[:, ns:] * p_im
        acc_im = acc_im + bb[:, :ns] * p_im + bb[:, ns:] * p_re
    z_ref[:, 0:ns] = acc_re
    z_ref[:, ns:2 * ns] = acc_im

    a_re, a_im = pw(CHUNK)

    def body(r, carry):
        s_re, s_im = carry
        z_re = z_ref[pl.ds(r, 1), 0:ns]
        z_im = z_ref[pl.ds(r, 1), ns:2 * ns]
        z_ref[pl.ds(r, 1), 0:ns] = s_re
        z_ref[pl.ds(r, 1), ns:2 * ns] = s_im
        return (a_re * s_re - a_im * s_im + z_re, a_re * s_im + a_im * s_re + z_im)

    s_re, s_im = lax.fori_loop(0, rt, body, (carry_ref[0:1, 0:ns], carry_ref[0:1, ns:2 * ns]))
    carry_ref[0:1, 0:ns] = s_re
    carry_ref[0:1, ns:2 * ns] = s_im

    s_re = z_ref[:, 0:ns]
    s_im = z_ref[:, ns:2 * ns]
    ys = []
    for t in range(CHUNK):
        y = jnp.dot(u_tile(t), klag_ref[0], preferred_element_type=F32)
        for s in range(t):
            y = y + jnp.dot(u_tile(s), klag_ref[t - s], preferred_element_type=F32)
        p_re, p_im = pw(t + 1)
        rot = jnp.concatenate([s_re * p_re - s_im * p_im, s_re * p_im + s_im * p_re], axis=-1)
        y = y + jnp.dot(rot.astype(BF16), cc_ref[...], preferred_element_type=F32)
        ys.append(y.astype(BF16))
    y_ref[...] = jnp.swapaxes(jnp.stack(ys, axis=0), 0, 1).reshape(rt * CHUNK, tw)


def _s5_tail_kernel(y_ref, gs_ref, wg_ref, bg_ref, wp_ref, o_ref):
    y = jnp.concatenate([y_ref[j] for j in range(y_ref.shape[0])], axis=-1).astype(F32)
    z = 0.5 * y * (1.0 + jnp.tanh(math.sqrt(2.0 / math.pi) * (y + 0.044715 * (y * y * y))))
    gl = jnp.dot(z.astype(BF16), wg_ref[...], preferred_element_type=F32) + bg_ref[...]
    zz = (z * _sigmoid(gl)).astype(BF16)
    ys = jnp.dot(zz, wp_ref[...], preferred_element_type=F32)
    o_ref[...] = (gs_ref[...].astype(F32) * ys).astype(BF16)


def _split3(x):
    hi = x.astype(BF16)
    r = x - hi.astype(F32)
    mid = r.astype(BF16)
    lo = (r - mid.astype(F32)).astype(BF16)
    return hi, mid, lo


def _hgrn_kernel(q_ref, k_ref, v_ref, og_ref, lf_ref, gh_ref, ms_ref, x_ref,
                 wph_ref, wout_ref, ng_ref, g2_ref, x1_ref, h2_ref,
                 st_ref, qt_ref, kh_ref, b_ref, gt_ref, o_ref, *, heads):
    tl = q_ref.shape[0]
    hd = HEAD_DIM

    @pl.when(pl.program_id(1) == 0)
    def _():
        st_ref[...] = jnp.zeros_like(st_ref)

    r = lax.broadcasted_iota(jnp.int32, (tl, tl), 0)
    c = lax.broadcasted_iota(jnp.int32, (tl, tl), 1)
    pieces = _split3(lf_ref[...])

    def chunk_sums(chunk):
        causal = jnp.logical_and((r // chunk) == (c // chunk), c <= r)
        tri = jnp.where(causal, 1.0, 0.0).astype(BF16)
        b = jnp.zeros((tl, heads * hd), F32)
        for piece in pieces:
            b = b + jnp.dot(tri, piece, preferred_element_type=F32)
        last = b.reshape(tl // chunk, chunk, heads * hd)[:, chunk - 1:chunk, :]
        bt = jnp.broadcast_to(last, (tl // chunk, chunk, heads * hd)).reshape(tl, heads * hd)
        return causal, b, bt

    def store_decayed(b, bt):
        b_ref[...] = b
        gt_ref[...] = jnp.exp(bt)
        qt_ref[...] = (q_ref[...].astype(F32) * jnp.exp(b)).astype(BF16)
        kh_ref[...] = (k_ref[...].astype(F32) * jnp.exp(bt - b)).astype(BF16)

    half = CHUNK // 2
    row = lax.broadcasted_iota(jnp.int32, (half, 1), 0)

    def pairs_exact(rows, lanes):
        q16 = q_ref[rows, lanes].astype(F32)
        k16 = k_ref[rows, lanes].astype(F32)
        v16 = v_ref[rows, lanes].astype(F32)
        b16 = b_ref[rows, lanes]
        q_top, q_bot = q16[:half], q16[half:]
        b_top, b_bot = b16[:half], b16[half:]
        acc_top = jnp.zeros((half, hd), F32)
        acc_bot = jnp.zeros((half, hd), F32)
        for s in range(CHUNK):
            bs = b16[s:s + 1]
            ks = k16[s:s + 1]
            vs = v16[s:s + 1]
            if s < half:
                w = jnp.sum(q_top * ks * jnp.exp(jnp.minimum(b_top - bs, 0.0)), axis=-1, keepdims=True)
                acc_top = acc_top + jnp.where(row >= s, w, 0.0) * vs
                w = jnp.sum(q_bot * ks * jnp.exp(b_bot - bs), axis=-1, keepdims=True)
                acc_bot = acc_bot + w * vs
            else:
                w = jnp.sum(q_bot * ks * jnp.exp(jnp.minimum(b_bot - bs, 0.0)), axis=-1, keepdims=True)
                acc_bot = acc_bot + jnp.where(row + half >= s, w, 0.0) * vs
        return jnp.concatenate([acc_top, acc_bot], axis=0)

    def carry_state(rows, first_row, h):
        lanes = slice(h * hd, (h + 1) * hd)
        st = st_ref[h]
        inter = lax.dot_general(qt_ref[rows, lanes], st.astype(BF16),
                                (((1,), (1,)), ((), ())), preferred_element_type=F32)
        upd = lax.dot_general(v_ref[rows, lanes], kh_ref[rows, lanes],
                              (((0,), (0,)), ((), ())), preferred_element_type=F32)
        st_ref[h] = gt_ref[first_row, lanes] * st + upd
        return inter

    def exact_step(n, _):
        r0 = pl.multiple_of(n * CHUNK, CHUNK)
        rows = pl.ds(r0, CHUNK)
        for h in range(heads):
            lanes = slice(h * hd, (h + 1) * hd)
            o_ref[rows, lanes] = carry_state(rows, pl.ds(r0, 1), h) + pairs_exact(rows, lanes)
        return 0

    causal, b, bt = chunk_sums(HGRN_WIDE_CHUNK)
    worst = jnp.max(jnp.max(-bt, axis=0, keepdims=True), axis=1, keepdims=True)[0, 0]
    factorable = worst <= HGRN_FACTOR_LIMIT

    @pl.when(factorable)
    def _():
        store_decayed(b, bt)
        kx = (k_ref[...].astype(F32) * jnp.exp(-b)).astype(BF16)
        for h in range(heads):
            lanes = slice(h * hd, (h + 1) * hd)
            sc = lax.dot_general(qt_ref[:, lanes], kx[:, lanes], (((1,), (1,)), ((), ())),
                                 preferred_element_type=F32)
            o_ref[:, lanes] = jnp.dot(jnp.where(causal, sc, 0.0).astype(BF16), v_ref[:, lanes],
                                      preferred_element_type=F32)
        for n in range(tl // HGRN_WIDE_CHUNK):
            rows = slice(n * HGRN_WIDE_CHUNK, (n + 1) * HGRN_WIDE_CHUNK)
            for h in range(heads):
                lanes = slice(h * hd, (h + 1) * hd)
                o_ref[rows, lanes] += carry_state(rows, slice(rows.start, rows.start + 1), h)

    @pl.when(jnp.logical_not(factorable))
    def _():
        store_decayed(*chunk_sums(CHUNK)[1:])
        lax.fori_loop(0, tl // CHUNK, exact_step, 0)

    parts = []
    for h in range(heads):
        lanes = slice(h * hd, (h + 1) * hd)
        o = o_ref[:, lanes]
        o = o * lax.rsqrt(jnp.mean(o * o, axis=-1, keepdims=True) + RMS_EPS)
        parts.append((o * ng_ref[:, lanes] * og_ref[:, lanes].astype(F32)).astype(BF16))
    on = jnp.concatenate(parts, axis=-1)
    yh = jnp.dot(on, wph_ref[...], preferred_element_type=F32)
    merged = ms_ref[...].astype(F32) + gh_ref[...].astype(F32) * yh
    x1 = x_ref[...] + jnp.dot(merged.astype(BF16), wout_ref[...], preferred_element_type=F32)
    x1_ref[...] = x1
    h2_ref[...] = _rmsnorm(x1, g2_ref[...]).astype(BF16)


def _ffn_kernel(h_ref, halo_ref, wug_ref, wuv_ref, cwg_ref, cwv_ref, cbg_ref, cbv_ref, wd_ref,
                x1_ref, gf_ref, o_ref, hcat_ref, *, tiles_per_seq):
    i = pl.program_id(0)
    j = pl.program_id(1)
    tm = h_ref.shape[0]
    nh = halo_ref.shape[0]

    @pl.when(j == 0)
    def _():
        keep = jnp.where(i % tiles_per_seq == 0, 0.0, 1.0).astype(BF16)
        hcat_ref[0:nh] = halo_ref[...] * keep
        hcat_ref[nh:nh + tm] = h_ref[...]
        o_ref[...] = x1_ref[...]

    def conv_up(w_ref, cw_ref, cb_ref):
        ext = jnp.dot(hcat_ref[...], w_ref[...], preferred_element_type=F32)
        out = cb_ref[...]
        for tap in range(CONV_WIDTH):
            back = CONV_WIDTH - 1 - tap
            out = out + cw_ref[tap:tap + 1, :] * ext[nh - back:nh - back + tm]
        return out

    gate = conv_up(wug_ref, cwg_ref, cbg_ref)
    val = conv_up(wuv_ref, cwv_ref, cbv_ref)
    act = (0.5 * gate * (1.0 + jnp.tanh(0.5 * gate)) * val).astype(BF16)
    o_ref[...] += jnp.dot(act, wd_ref[...], preferred_element_type=F32)

    @pl.when(j == pl.num_programs(1) - 1)
    def _():
        o_ref[...] = _rmsnorm(o_ref[...], gf_ref[...])


def _tile(n, want):
    t = min(n, want)
    assert n % t == 0, (n, want)
    return t


def _row(v):
    return v.reshape(1, -1).astype(F32)


def _inproj(x2, g, w_in, lb_logits, to_cast, *, wh):
    tok, dm = x2.shape
    tm = _tile(tok, TOKEN_TILE)
    nblk = wh // MXU_DIM_V7X
    n_a = 5 * wh
    assert w_in.shape[1] == n_a + 2 * dm and dm % wh == 0
    tok_spec = lambda w: pl.BlockSpec((tm, w), lambda i: (i, 0))
    cols = lambda n, width: pl.BlockSpec((dm, width), lambda i: (0, n), pipeline_mode=pl.Buffered(1))
    u, q, k, v, og, lf = pl.pallas_call(
        functools.partial(_inproj_a_kernel, width=wh),
        grid=(tok // tm,),
        in_specs=[tok_spec(dm), _resident((1, dm)), cols(0, n_a), _resident(lb_logits.shape)],
        out_specs=[pl.BlockSpec((nblk, tm, MXU_DIM_V7X), lambda i: (0, i, 0))] + [tok_spec(wh)] * 5,
        out_shape=[jax.ShapeDtypeStruct((nblk, tok, MXU_DIM_V7X), BF16)]
                  + [jax.ShapeDtypeStruct((tok, wh), BF16)] * 4 + [jax.ShapeDtypeStruct((tok, wh), F32)],
        compiler_params=_params("parallel"),
        name="inproj_a",
    )(x2, g, w_in, lb_logits)
    gate_blocks = 2 * dm // wh
    steps = tok // tm
    bf16_rows = 16
    assert all(w.shape[0] % (steps * bf16_rows) == 0 for w in to_cast)
    row_blocks = [pl.BlockSpec((w.shape[0] // steps, w.shape[1]), lambda i: (i, 0)) for w in to_cast]
    gs, gh, *casted = pl.pallas_call(
        functools.partial(_inproj_b_kernel, n_w=gate_blocks, n_cast=len(to_cast)),
        grid=(steps,),
        in_specs=[tok_spec(dm), _resident((1, dm))] + [cols(n_a // wh + n, wh) for n in range(gate_blocks)]
                 + row_blocks,
        out_specs=[tok_spec(dm)] * 2 + row_blocks,
        out_shape=[jax.ShapeDtypeStruct((tok, dm), BF16)] * 2
                  + [jax.ShapeDtypeStruct(w.shape, BF16) for w in to_cast],
        compiler_params=_params("parallel"),
        name="inproj_b",
    )(x2, g, *([w_in] * gate_blocks), *to_cast)
    return u, q, k, v, og, lf, gs, gh, casted


def _s5_prep(a_re, a_im, log_dt, b_re, b_im, c_re, c_im, d):
    groups = a_re.shape[0]
    nblk = groups // S5_BLOCK_GROUPS
    ns = S5_BLOCK_STATE
    tile_c = S5_BLOCK_GROUPS * S5_GROUP
    blk3 = lambda r, c: pl.BlockSpec((1, r, c), lambda j: (j, 0, 0))
    pw_rows = 24
    gb, p, c = S5_BLOCK_GROUPS, S5_STATE, S5_GROUP
    tiled = lambda t, r, w: t.astype(F32).reshape(nblk, r, w)
    return pl.pallas_call(
        _s5_prep_kernel,
        grid=(nblk,),
        in_specs=[blk3(gb * p, c), blk3(gb * p, c), blk3(gb * c, p), blk3(gb * c, p),
                  blk3(gb, p), blk3(gb, p), blk3(gb, 1), blk3(gb, c)],
        out_specs=[pl.BlockSpec((1, S5_SUB, tile_c, 2 * ns), lambda j: (j, 0, 0, 0)), blk3(2 * ns, tile_c),
                   pl.BlockSpec((1, CHUNK, tile_c, tile_c), lambda j: (j, 0, 0, 0)), blk3(pw_rows, 2 * ns)],
        out_shape=[jax.ShapeDtypeStruct((nblk, S5_SUB, tile_c, 2 * ns), BF16),
                   jax.ShapeDtypeStruct((nblk, 2 * ns, tile_c), BF16),
                   jax.ShapeDtypeStruct((nblk, CHUNK, tile_c, tile_c), BF16),
                   jax.ShapeDtypeStruct((nblk, pw_rows, 2 * ns), F32)],
        compiler_params=_params("parallel"),
        name="s5_prep",
    )(tiled(b_re, gb * p, c), tiled(b_im, gb * p, c), tiled(c_re, gb * c, p), tiled(c_im, gb * c, p),
      tiled(a_re, gb, p), tiled(a_im, gb, p), tiled(log_dt, gb, 1), tiled(d, gb, c))


def _s5_scan(u, bc, cc, klag, pw, *, seq):
    nblk, tok, tw = u.shape
    ns = S5_BLOCK_STATE
    rows_seq = seq // CHUNK
    rt = _tile(rows_seq, S5_ROW_TILE)
    per_tile = lambda a: pl.BlockSpec((None,) + a.shape[1:], lambda j, i: (j,) + (0,) * (a.ndim - 1))
    rows_spec = pl.BlockSpec((None, rt * CHUNK, tw), lambda j, i: (j, i, 0))
    return pl.pallas_call(
        functools.partial(_s5_scan_kernel, tiles_per_seq=rows_seq // rt),
        grid=(nblk, tok // (rt * CHUNK)),
        in_specs=[rows_spec, per_tile(bc), per_tile(cc), per_tile(klag), per_tile(pw)],
        out_specs=rows_spec,
        out_shape=jax.ShapeDtypeStruct(u.shape, BF16),
        scratch_shapes=[pltpu.VMEM((rt, 2 * ns), F32), pltpu.VMEM((8, 2 * ns), F32)],
        compiler_params=_params("parallel", "arbitrary"),
        name="s5_scan",
    )(u, bc, cc, klag, pw)


def _s5_tail(y, gs, w_glu, b_glu, w_proj):
    nblk, tok, tw = y.shape
    dm = gs.shape[1]
    tm = _tile(tok, S5_TAIL_TOKEN_TILE)
    tok_spec = lambda w: pl.BlockSpec((tm, w), lambda i: (i, 0))
    return pl.pallas_call(
        _s5_tail_kernel,
        grid=(tok // tm,),
        in_specs=[pl.BlockSpec((nblk, tm, tw), lambda i: (0, i, 0)), tok_spec(dm), _resident(w_glu.shape),
                  _resident(b_glu.shape), _resident(w_proj.shape)],
        out_specs=tok_spec(dm),
        out_shape=jax.ShapeDtypeStruct((tok, dm), BF16),
        compiler_params=_params("parallel"),
        name="s5_tail",
    )(y, gs, w_glu, b_glu, w_proj)


def _hgrn_merge(q, k, v, og, lf, gh, ms, x2, w_proj, w_out, norm_g, ffn_g, *, seq):
    tok, wh = q.shape
    dm = x2.shape[1]
    heads = wh // HEAD_DIM
    tl = _tile(seq, HGRN_TOKEN_TILE)
    lt = seq // tl
    seq_spec = lambda w: pl.BlockSpec((tl, w), lambda b, l: (b * lt + l, 0))
    return pl.pallas_call(
        functools.partial(_hgrn_kernel, heads=heads),
        grid=(tok // seq, lt),
        in_specs=[seq_spec(wh)] * 5 + [seq_spec(dm)] * 3
                 + [_resident(w_proj.shape), _resident(w_out.shape), _resident((1, wh)), _resident((1, dm))],
        out_specs=[seq_spec(dm)] * 2,
        out_shape=[jax.ShapeDtypeStruct((tok, dm), F32), jax.ShapeDtypeStruct((tok, dm), BF16)],
        scratch_shapes=[pltpu.VMEM((heads, HEAD_DIM, HEAD_DIM), F32),
                        pltpu.VMEM((tl, wh), BF16), pltpu.VMEM((tl, wh), BF16),
                        pltpu.VMEM((tl, wh), F32), pltpu.VMEM((tl, wh), F32), pltpu.VMEM((tl, wh), F32)],
        compiler_params=_params("arbitrary", "arbitrary"),
        name="hgrn_merge",
    )(q, k, v, og, lf, gh, ms, x2, w_proj, w_out, norm_g, ffn_g)


def _ffn(h2, x1, w_up, cw, cb, w_down, final_g, *, seq):
    tok, dm = h2.shape
    dff = w_down.shape[0]
    tf = _tile(dff, FFN_FF_TILE)
    nf = dff // tf
    tmf = _tile(seq, FFN_TOKEN_TILE)
    halo = 16
    return pl.pallas_call(
        functools.partial(_ffn_kernel, tiles_per_seq=seq // tmf),
        grid=(tok // tmf, nf),
        in_specs=[pl.BlockSpec((tmf, dm), lambda i, j: (i, 0)),
                  pl.BlockSpec((halo, dm), lambda i, j: (jnp.maximum(i * (tmf // halo) - 1, 0), 0)),
                  pl.BlockSpec((dm, tf), lambda i, j: (0, j)),
                  pl.BlockSpec((dm, tf), lambda i, j: (0, nf + j)),
                  pl.BlockSpec((CONV_WIDTH, tf), lambda i, j: (0, j)),
                  pl.BlockSpec((CONV_WIDTH, tf), lambda i, j: (0, nf + j)),
                  pl.BlockSpec((1, tf), lambda i, j: (0, j)),
                  pl.BlockSpec((1, tf), lambda i, j: (0, nf + j)),
                  pl.BlockSpec((tf, dm), lambda i, j: (j, 0)),
                  pl.BlockSpec((tmf, dm), lambda i, j: (i, 0), pipeline_mode=pl.Buffered(1)),
                  pl.BlockSpec((1, dm), lambda i, j: (0, 0))],
        out_specs=pl.BlockSpec((tmf, dm), lambda i, j: (i, 0), pipeline_mode=pl.Buffered(1)),
        out_shape=jax.ShapeDtypeStruct((tok, dm), F32),
        scratch_shapes=[pltpu.VMEM((halo + tmf, dm), BF16)],
        compiler_params=pltpu.CompilerParams(dimension_semantics=("parallel", "arbitrary"),
                                             vmem_limit_bytes=FFN_VMEM_LIMIT_BYTES),
        name="ffn",
    )(h2, h2, w_up, w_up, cw, cw, cb, cb, w_down, x1, final_g)


def kernel(x, ln_mix_g, w_in, s5_a_re, s5_a_im, s5_log_dt, s5_b_re, s5_b_im, s5_c_re, s5_c_im,
           s5_d, s5_w_glu, s5_b_glu, w_proj_s5, hgrn_lb_logits, hgrn_norm_g, w_proj_hgrn,
           w_out, ln_ffn_g, w_up, conv_w, conv_b, w_down, ln_final_g):
    bsz, seq, dm = x.shape
    assert ln_mix_g.shape[0] == 1, "single-layer block"
    tok = bsz * seq
    ws5 = s5_w_glu.shape[1]
    wh = hgrn_norm_g.shape[1]
    assert ws5 == wh and ws5 % MXU_DIM_V7X == 0 and seq % CHUNK == 0
    assert w_in.shape[2] == ws5 + 4 * wh + 2 * dm

    x2 = x.reshape(tok, dm)
    later_weights = [w.astype(F32) for w in (s5_w_glu[0], w_proj_s5[0], w_proj_hgrn[0], w_out[0], w_up[0], w_down[0])]
    u, q, k, v, og, lf, gs, gh, (w_glu_b, w_ps5_b, w_ph_b, w_out_b, w_up_b, w_down_b) = _inproj(
        x2, _row(ln_mix_g[0]), w_in[0].astype(BF16), hgrn_lb_logits.astype(F32), later_weights, wh=wh)

    bc, cc, klag, pw = _s5_prep(s5_a_re[0], s5_a_im[0], s5_log_dt[0], s5_b_re[0], s5_b_im[0],
                                s5_c_re[0], s5_c_im[0], s5_d[0])
    y = _s5_scan(u, bc, cc, klag, pw, seq=seq)
    ms = _s5_tail(y, gs, w_glu_b, _row(s5_b_glu[0]), w_ps5_b)

    x1, h2 = _hgrn_merge(q, k, v, og, lf, gh, ms, x2, w_ph_b, w_out_b,
                         _row(hgrn_norm_g[0]), _row(ln_ffn_g[0]), seq=seq)

    out = _ffn(h2, x1, w_up_b, conv_w[0].astype(F32), conv_b[0].astype(F32).reshape(1, -1),
               w_down_b, _row(ln_final_g), seq=seq)
    return out.reshape(bsz, seq, dm)
```

```python
import functools
import math

import jax
import jax.numpy as jnp
from jax import lax
from jax.experimental import pallas as pl
from jax.experimental.pallas import tpu as pltpu

F32 = jnp.float32
BF16 = jnp.bfloat16

RMS_EPS = 1e-6
S5_MAX_RE = -1e-4
S5_GROUP = 16
S5_STATE = 64
HEAD_DIM = 128
CONV_WIDTH = 3

MXU_DIM_V7X = 256
CHUNK = 16
S5_SUB = 4
S5_BLOCK_GROUPS = MXU_DIM_V7X // S5_GROUP
S5_BLOCK_STATE = S5_BLOCK_GROUPS * S5_STATE
VMEM_LIMIT_BYTES = 56 * 1024 * 1024
FFN_VMEM_LIMIT_BYTES = 62 * 1024 * 1024
HGRN_FACTOR_LIMIT = 60.0
HGRN_WIDE_CHUNK = 64

TOKEN_TILE = 512
S5_TAIL_TOKEN_TILE = 1024
S5_ROW_TILE = 256
HGRN_TOKEN_TILE = 256
FFN_TOKEN_TILE = 1024
FFN_FF_TILE = 512


def _sigmoid(y):
    return 0.5 + 0.5 * jnp.tanh(0.5 * y)


def _rmsnorm(x, g):
    return x * lax.rsqrt(jnp.mean(x * x, axis=-1, keepdims=True) + RMS_EPS) * g


def _params(*sem):
    return pltpu.CompilerParams(dimension_semantics=sem, vmem_limit_bytes=VMEM_LIMIT_BYTES)


def _resident(shape):
    return pl.BlockSpec(shape, lambda *_: (0,) * len(shape), pipeline_mode=pl.Buffered(1))


def _inproj_a_kernel(x_ref, g_ref, w_ref, lbl_ref, u_ref, q_ref, k_ref, v_ref, og_ref, lf_ref, *, width):
    h = _rmsnorm(x_ref[...], g_ref[...]).astype(BF16)

    def proj(i):
        return jnp.dot(h, w_ref[:, i * width:(i + 1) * width], preferred_element_type=F32)

    a = proj(1)
    q_ref[...] = (a * _sigmoid(a)).astype(BF16)
    lbl = lbl_ref[...]
    e = jnp.exp(lbl - jnp.max(lbl, axis=0, keepdims=True))
    lb = e[0:1, :] / jnp.sum(e, axis=0, keepdims=True)
    sg = _sigmoid(proj(2))
    lf_ref[...] = jnp.log(lb + (1.0 - lb) * sg)
    k_ref[...] = ((1.0 - lb) * (1.0 - sg)).astype(BF16)
    a = proj(4)
    og_ref[...] = (a * _sigmoid(a)).astype(BF16)
    v_ref[...] = proj(3).astype(BF16)
    a = proj(0).astype(BF16)
    for j in range(u_ref.shape[0]):
        u_ref[j] = a[:, j * MXU_DIM_V7X:(j + 1) * MXU_DIM_V7X]


def _inproj_b_kernel(x_ref, g_ref, *refs, n_w, n_cast):
    w_refs, src_refs = refs[:n_w], refs[n_w:n_w + n_cast]
    o_refs, dst_refs = refs[n_w + n_cast:n_w + n_cast + 2], refs[n_w + n_cast + 2:]
    for src, dst in zip(src_refs, dst_refs):
        dst[...] = src[...].astype(BF16)
    per_out = len(w_refs) // len(o_refs)
    h = _rmsnorm(x_ref[...], g_ref[...]).astype(BF16)
    for n, w_ref in enumerate(w_refs):
        cols = w_ref.shape[1]
        lo = (n % per_out) * cols
        o_refs[n // per_out][:, lo:lo + cols] = _sigmoid(
            jnp.dot(h, w_ref[...], preferred_element_type=F32)).astype(BF16)


def _spread(x, reps):
    w = x.shape[1]
    r = lax.broadcasted_iota(jnp.int32, (w, reps * w), 0)
    c = lax.broadcasted_iota(jnp.int32, (w, reps * w), 1)
    sel = jnp.where(c % w == r, 1.0, 0.0).astype(BF16)
    out = jnp.zeros((x.shape[0], reps * w), F32)
    for piece in _split3(x):
        out = out + jnp.dot(piece, sel, preferred_element_type=F32)
    return out


def _own_group(x, row_div, lane_div):
    r = lax.broadcasted_iota(jnp.int32, x.shape, 0)
    c = lax.broadcasted_iota(jnp.int32, x.shape, 1)
    return jnp.where(r // row_div == c // lane_div, x, 0.0)


def _s5_prep_kernel(bre_ref, bim_ref, cre_ref, cim_ref, are_ref, aim_ref, ldt_ref, d_ref,
                    bc_ref, cc_ref, klag_ref, pw_ref):
    ns = S5_BLOCK_STATE
    gb, p, c = S5_BLOCK_GROUPS, S5_STATE, S5_GROUP
    b_tile = lambda ref: _own_group(_spread(ref[0], gb), p, c).T
    c_tile = lambda ref: _own_group(_spread(ref[0], gb), c, p).T
    per_lane = lambda x, w: jnp.sum(_own_group(x, 1, w), axis=0, keepdims=True)
    lam_re = jnp.minimum(per_lane(_spread(are_ref[0], gb), p), S5_MAX_RE)
    lam_im = per_lane(_spread(aim_ref[0], gb), p)
    dt = jnp.exp(per_lane(jnp.broadcast_to(ldt_ref[0], (gb, ns)), p))
    d_row = per_lane(_spread(d_ref[0], gb), c)
    mag = jnp.exp(lam_re * dt)
    ab_re = mag * jnp.cos(lam_im * dt)
    ab_im = mag * jnp.sin(lam_im * dt)
    den = lam_re * lam_re + lam_im * lam_im
    nr = ab_re - 1.0
    ni = ab_im
    cf_re = (nr * lam_re + ni * lam_im) / den
    cf_im = (ni * lam_re - nr * lam_im) / den
    bre = b_tile(bre_ref)
    bim = b_tile(bim_ref)
    b_re = cf_re * bre - cf_im * bim
    b_im = cf_re * bim + cf_im * bre
    cre = c_tile(cre_ref).astype(BF16)
    cim = c_tile(cim_ref).astype(BF16)
    cc_ref[0] = jnp.concatenate([cre, -cim], axis=0)

    p_re = jnp.ones_like(ab_re)
    p_im = jnp.zeros_like(ab_re)
    pw_ref[...] = jnp.zeros_like(pw_ref)
    for n in range(CHUNK + 1):
        pw_ref[0, n:n + 1, :] = jnp.concatenate([p_re, p_im], axis=-1)
        if n < S5_SUB:
            bc_ref[0, S5_SUB - 1 - n] = jnp.concatenate(
                [b_re * p_re - b_im * p_im, b_re * p_im + b_im * p_re], axis=-1).astype(BF16)
        if n < CHUNK:
            t_re = (b_re * p_re - b_im * p_im).astype(BF16)
            t_im = (b_re * p_im + b_im * p_re).astype(BF16)
            tap = (jnp.dot(t_re, cre, preferred_element_type=F32)
                   - jnp.dot(t_im, cim, preferred_element_type=F32))
            if n == 0:
                r = lax.broadcasted_iota(jnp.int32, tap.shape, 0)
                c = lax.broadcasted_iota(jnp.int32, tap.shape, 1)
                tap = tap + jnp.where(r == c, d_row, 0.0)
            klag_ref[0, n] = tap.astype(BF16)
        p_re, p_im = p_re * ab_re - p_im * ab_im, p_re * ab_im + p_im * ab_re


def _s5_scan_kernel(u_ref, bc_ref, cc_ref, klag_ref, pw_ref, y_ref, z_ref, carry_ref, *, tiles_per_seq):
    ns = S5_BLOCK_STATE
    tw = MXU_DIM_V7X
    rt = u_ref.shape[0] // CHUNK
    by_pos = jnp.swapaxes(u_ref[...].reshape(rt, CHUNK, tw), 0, 1)

    def u_tile(s):
        return by_pos[s]

    def pw(n):
        return pw_ref[n:n + 1, 0:ns], pw_ref[n:n + 1, ns:2 * ns]

    @pl.when(pl.program_id(1) % tiles_per_seq == 0)
    def _():
        carry_ref[...] = jnp.zeros_like(carry_ref)

    acc_re = jnp.zeros((rt, ns), F32)
    acc_im = jnp.zeros((rt, ns), F32)
    for sub in range(CHUNK // S5_SUB):
        bb = jnp.dot(u_tile(sub * S5_SUB), bc_ref[0], preferred_element_type=F32)
        for r in range(1, S5_SUB):
            bb = bb + jnp.dot(u_tile(sub * S5_SUB + r), bc_ref[r], preferred_element_type=F32)
        p_re, p_im = pw(CHUNK - S5_SUB * (sub + 1))
        acc_re = acc_re + bb[:, :ns] * p_re - bb[:, ns:] * p_im
        acc_im = acc_im + bb[:, :ns] * p_im + bb[:, ns:] * p_re
    z_ref[:, 0:ns] = acc_re
    z_ref[:, ns:2 * ns] = acc_im

    a_re, a_im = pw(CHUNK)

    def body(r, carry):
        s_re, s_im = carry
        z_re = z_ref[pl.ds(r, 1), 0:ns]
        z_im = z_ref[pl.ds(r, 1), ns:2 * ns]
        z_ref[pl.ds(r, 1), 0:ns] = s_re
        z_ref[pl.ds(r, 1), ns:2 * ns] = s_im
        return (a_re * s_re - a_im * s_im + z_re, a_re * s_im + a_im * s_re + z_im)

    s_re, s_im = lax.fori_loop(0, rt, body, (carry_ref[0:1, 0:ns], carry_ref[0:1, ns:2 * ns]))
    carry_ref[0:1, 0:ns] = s_re
    carry_ref[0:1, ns:2 * ns] = s_im

    s_re = z_ref[:, 0:ns]
    s_im = z_ref[:, ns:2 * ns]
    ys = []
    for t in range(CHUNK):
        y = jnp.dot(u_tile(t), klag_ref[0], preferred_element_type=F32)
        for s in range(t):
            y = y + jnp.dot(u_tile(s), klag_ref[t - s], preferred_element_type=F32)
        p_re, p_im = pw(t + 1)
        rot = jnp.concatenate([s_re * p_re - s_im * p_im, s_re * p_im + s_im * p_re], axis=-1)
        y = y + jnp.dot(rot.astype(BF16), cc_ref[...], preferred_element_type=F32)
        ys.append(y.astype(BF16))
    y_ref[...] = jnp.swapaxes(jnp.stack(ys, axis=0), 0, 1).reshape(rt * CHUNK, tw)


def _s5_tail_kernel(y_ref, gs_ref, wg_ref, bg_ref, wp_ref, o_ref):
    y = jnp.concatenate([y_ref[j] for j in range(y_ref.shape[0])], axis=-1).astype(F32)
    z = 0.5 * y * (1.0 + jnp.tanh(math.sqrt(2.0 / math.pi) * (y + 0.044715 * (y * y * y))))
    gl = jnp.dot(z.astype(BF16), wg_ref[...], preferred_element_type=F32) + bg_ref[...]
    zz = (z * _sigmoid(gl)).astype(BF16)
    ys = jnp.dot(zz, wp_ref[...], preferred_element_type=F32)
    o_ref[...] = (gs_ref[...].astype(F32) * ys).astype(BF16)


def _split3(x):
    hi = x.astype(BF16)
    r = x - hi.astype(F32)
    mid = r.astype(BF16)
    lo = (r - mid.astype(F32)).astype(BF16)
    return hi, mid, lo


def _hgrn_kernel(q_ref, k_ref, v_ref, og_ref, lf_ref, gh_ref, ms_ref, x_ref,
                 wph_ref, wout_ref, ng_ref, g2_ref, x1_ref, h2_ref,
                 st_ref, qt_ref, kh_ref, b_ref, gt_ref, o_ref, *, heads):
    tl = q_ref.shape[0]
    hd = HEAD_DIM

    @pl.when(pl.program_id(1) == 0)
    def _():
        st_ref[...] = jnp.zeros_like(st_ref)

    r = lax.broadcasted_iota(jnp.int32, (tl, tl), 0)
    c = lax.broadcasted_iota(jnp.int32, (tl, tl), 1)
    pieces = _split3(lf_ref[...])

    def chunk_sums(chunk):
        causal = jnp.logical_and((r // chunk) == (c // chunk), c <= r)
        tri = jnp.where(causal, 1.0, 0.0).astype(BF16)
        b = jnp.zeros((tl, heads * hd), F32)
        for piece in pieces:
            b = b + jnp.dot(tri, piece, preferred_element_type=F32)
        last = b.reshape(tl // chunk, chunk, heads * hd)[:, chunk - 1:chunk, :]
        bt = jnp.broadcast_to(last, (tl // chunk, chunk, heads * hd)).reshape(tl, heads * hd)
        return causal, b, bt

    def store_decayed(b, bt):
        b_ref[...] = b
        gt_ref[...] = jnp.exp(bt)
        qt_ref[...] = (q_ref[...].astype(F32) * jnp.exp(b)).astype(BF16)
        kh_ref[...] = (k_ref[...].astype(F32) * jnp.exp(bt - b)).astype(BF16)

    half = CHUNK // 2
    row = lax.broadcasted_iota(jnp.int32, (half, 1), 0)

    def pairs_exact(rows, lanes):
        q16 = q_ref[rows, lanes].astype(F32)
        k16 = k_ref[rows, lanes].astype(F32)
        v16 = v_ref[rows, lanes].astype(F32)
        b16 = b_ref[rows, lanes]
        q_top, q_bot = q16[:half], q16[half:]
        b_top, b_bot = b16[:half], b16[half:]
        acc_top = jnp.zeros((half, hd), F32)
        acc_bot = jnp.zeros((half, hd), F32)
        for s in range(CHUNK):
            bs = b16[s:s + 1]
            ks = k16[s:s + 1]
            vs = v16[s:s + 1]
            if s < half:
                w = jnp.sum(q_top * ks * jnp.exp(jnp.minimum(b_top - bs, 0.0)), axis=-1, keepdims=True)
                acc_top = acc_top + jnp.where(row >= s, w, 0.0) * vs
                w = jnp.sum(q_bot * ks * jnp.exp(b_bot - bs), axis=-1, keepdims=True)
                acc_bot = acc_bot + w * vs
            else:
                w = jnp.sum(q_bot * ks * jnp.exp(jnp.minimum(b_bot - bs, 0.0)), axis=-1, keepdims=True)
                acc_bot = acc_bot + jnp.where(row + half >= s, w, 0.0) * vs
        return jnp.concatenate([acc_top, acc_bot], axis=0)

    def carry_state(rows, first_row, h):
        lanes = slice(h * hd, (h + 1) * hd)
        st = st_ref[h]
        inter = lax.dot_general(qt_ref[rows, lanes], st.astype(BF16),
                                (((1,), (1,)), ((), ())), preferred_element_type=F32)
        upd = lax.dot_general(v_ref[rows, lanes], kh_ref[rows, lanes],
                              (((0,), (0,)), ((), ())), preferred_element_type=F32)
        st_ref[h] = gt_ref[first_row, lanes] * st + upd
        return inter

    def exact_step(n, _):
        r0 = pl.multiple_of(n * CHUNK, CHUNK)
        rows = pl.ds(r0, CHUNK)
        for h in range(heads):
            lanes = slice(h * hd, (h + 1) * hd)
            o_ref[rows, lanes] = carry_state(rows, pl.ds(r0, 1), h) + pairs_exact(rows, lanes)
        return 0

    causal, b, bt = chunk_sums(HGRN_WIDE_CHUNK)
    worst = jnp.max(jnp.max(-bt, axis=0, keepdims=True), axis=1, keepdims=True)[0, 0]
    factorable = worst <= HGRN_FACTOR_LIMIT

    @pl.when(factorable)
    def _():
        store_decayed(b, bt)
        kx = (k_ref[...].astype(F32) * jnp.exp(-b)).astype(BF16)
        for h in range(heads):
            lanes = slice(h * hd, (h + 1) * hd)
            sc = lax.dot_general(qt_ref[:, lanes], kx[:, lanes], (((1,), (1,)), ((), ())),
                                 preferred_element_type=F32)
            o_ref[:, lanes] = jnp.dot(jnp.where(causal, sc, 0.0).astype(BF16), v_ref[:, lanes],
                                      preferred_element_type=F32)
        for n in range(tl // HGRN_WIDE_CHUNK):
            rows = slice(n * HGRN_WIDE_CHUNK, (n + 1) * HGRN_WIDE_CHUNK)
            for h in range(heads):
                lanes = slice(h * hd, (h + 1) * hd)
                o_ref[rows, lanes] += carry_state(rows, slice(rows.start, rows.start + 1), h)

    @pl.when(jnp.logical_not(factorable))
    def _():
        store_decayed(*chunk_sums(CHUNK)[1:])
        lax.fori_loop(0, tl // CHUNK, exact_step, 0)

    parts = []
    for h in range(heads):
        lanes = slice(h * hd, (h + 1) * hd)
        o = o_ref[:, lanes]
        o = o * lax.rsqrt(jnp.mean(o * o, axis=-1, keepdims=True) + RMS_EPS)
        parts.append((o * ng_ref[:, lanes] * og_ref[:, lanes].astype(F32)).astype(BF16))
    on = jnp.concatenate(parts, axis=-1)
    yh = jnp.dot(on, wph_ref[...], preferred_element_type=F32)
    merged = ms_ref[...].astype(F32) + gh_ref[...].astype(F32) * yh
    x1 = x_ref[...] + jnp.dot(merged.astype(BF16), wout_ref[...], preferred_element_type=F32)
    x1_ref[...] = x1
    h2_ref[...] = _rmsnorm(x1, g2_ref[...]).astype(BF16)


def _ffn_kernel(h_ref, halo_ref, wug_ref, wuv_ref, cwg_ref, cwv_ref, cbg_ref, cbv_ref, wd_ref,
                x1_ref, gf_ref, o_ref, hcat_ref, *, tiles_per_seq):
    i = pl.program_id(0)
    j = pl.program_id(1)
    tm = h_ref.shape[0]
    nh = halo_ref.shape[0]

    @pl.when(j == 0)
    def _():
        keep = jnp.where(i % tiles_per_seq == 0, 0.0, 1.0).astype(BF16)
        hcat_ref[0:nh] = halo_ref[...] * keep
        hcat_ref[nh:nh + tm] = h_ref[...]
        o_ref[...] = x1_ref[...]

    def conv_up(w_ref, cw_ref, cb_ref):
        ext = jnp.dot(hcat_ref[...], w_ref[...], preferred_element_type=F32)
        out = cb_ref[...]
        for tap in range(CONV_WIDTH):
            back = CONV_WIDTH - 1 - tap
            out = out + cw_ref[tap:tap + 1, :] * ext[nh - back:nh - back + tm]
        return out

    gate = conv_up(wug_ref, cwg_ref, cbg_ref)
    val = conv_up(wuv_ref, cwv_ref, cbv_ref)
    act = (0.5 * gate * (1.0 + jnp.tanh(0.5 * gate)) * val).astype(BF16)
    o_ref[...] += jnp.dot(act, wd_ref[...], preferred_element_type=F32)

    @pl.when(j == pl.num_programs(1) - 1)
    def _():
        o_ref[...] = _rmsnorm(o_ref[...], gf_ref[...])


def _tile(n, want):
    t = min(n, want)
    assert n % t == 0, (n, want)
    return t


def _row(v):
    return v.reshape(1, -1).astype(F32)


def _inproj(x2, g, w_in, lb_logits, to_cast, *, wh):
    tok, dm = x2.shape
    tm = _tile(tok, TOKEN_TILE)
    nblk = wh // MXU_DIM_V7X
    n_a = 5 * wh
    assert w_in.shape[1] == n_a + 2 * dm and dm % wh == 0
    tok_spec = lambda w: pl.BlockSpec((tm, w), lambda i: (i, 0))
    cols = lambda n, width: pl.BlockSpec((dm, width), lambda i: (0, n), pipeline_mode=pl.Buffered(1))
    u, q, k, v, og, lf = pl.pallas_call(
        functools.partial(_inproj_a_kernel, width=wh),
        grid=(tok // tm,),
        in_specs=[tok_spec(dm), _resident((1, dm)), cols(0, n_a), _resident(lb_logits.shape)],
        out_specs=[pl.BlockSpec((nblk, tm, MXU_DIM_V7X), lambda i: (0, i, 0))] + [tok_spec(wh)] * 5,
        out_shape=[jax.ShapeDtypeStruct((nblk, tok, MXU_DIM_V7X), BF16)]
                  + [jax.ShapeDtypeStruct((tok, wh), BF16)] * 4 + [jax.ShapeDtypeStruct((tok, wh), F32)],
        compiler_params=_params("parallel"),
        name="inproj_a",
    )(x2, g, w_in, lb_logits)
    gate_blocks = 2 * dm // wh
    steps = tok // tm
    bf16_rows = 16
    assert all(w.shape[0] % (steps * bf16_rows) == 0 for w in to_cast)
    row_blocks = [pl.BlockSpec((w.shape[0] // steps, w.shape[1]), lambda i: (i, 0)) for w in to_cast]
    gs, gh, *casted = pl.pallas_call(
        functools.partial(_inproj_b_kernel, n_w=gate_blocks, n_cast=len(to_cast)),
        grid=(steps,),
        in_specs=[tok_spec(dm), _resident((1, dm))] + [cols(n_a // wh + n, wh) for n in range(gate_blocks)]
                 + row_blocks,
        out_specs=[tok_spec(dm)] * 2 + row_blocks,
        out_shape=[jax.ShapeDtypeStruct((tok, dm), BF16)] * 2
                  + [jax.ShapeDtypeStruct(w.shape, BF16) for w in to_cast],
        compiler_params=_params("parallel"),
        name="inproj_b",
    )(x2, g, *([w_in] * gate_blocks), *to_cast)
    return u, q, k, v, og, lf, gs, gh, casted


def _s5_prep(a_re, a_im, log_dt, b_re, b_im, c_re, c_im, d):
    groups = a_re.shape[0]
    nblk = groups // S5_BLOCK_GROUPS
    ns = S5_BLOCK_STATE
    tile_c = S5_BLOCK_GROUPS * S5_GROUP
    blk3 = lambda r, c: pl.BlockSpec((1, r, c), lambda j: (j, 0, 0))
    pw_rows = 24
    gb, p, c = S5_BLOCK_GROUPS, S5_STATE, S5_GROUP
    tiled = lambda t, r, w: t.astype(F32).reshape(nblk, r, w)
    return pl.pallas_call(
        _s5_prep_kernel,
        grid=(nblk,),
        in_specs=[blk3(gb * p, c), blk3(gb * p, c), blk3(gb * c, p), blk3(gb * c, p),
                  blk3(gb, p), blk3(gb, p), blk3(gb, 1), blk3(gb, c)],
        out_specs=[pl.BlockSpec((1, S5_SUB, tile_c, 2 * ns), lambda j: (j, 0, 0, 0)), blk3(2 * ns, tile_c),
                   pl.BlockSpec((1, CHUNK, tile_c, tile_c), lambda j: (j, 0, 0, 0)), blk3(pw_rows, 2 * ns)],
        out_shape=[jax.ShapeDtypeStruct((nblk, S5_SUB, tile_c, 2 * ns), BF16),
                   jax.ShapeDtypeStruct((nblk, 2 * ns, tile_c), BF16),
                   jax.ShapeDtypeStruct((nblk, CHUNK, tile_c, tile_c), BF16),
                   jax.ShapeDtypeStruct((nblk, pw_rows, 2 * ns), F32)],
        compiler_params=_params("parallel"),
        name="s5_prep",
    )(tiled(b_re, gb * p, c), tiled(b_im, gb * p, c), tiled(c_re, gb * c, p), tiled(c_im, gb * c, p),
      tiled(a_re, gb, p), tiled(a_im, gb, p), tiled(log_dt, gb, 1), tiled(d, gb, c))


def _s5_scan(u, bc, cc, klag, pw, *, seq):
    nblk, tok, tw = u.shape
    ns = S5_BLOCK_STATE
    rows_seq = seq // CHUNK
    rt = _tile(rows_seq, S5_ROW_TILE)
    per_tile = lambda a: pl.BlockSpec((None,) + a.shape[1:], lambda j, i: (j,) + (0,) * (a.ndim - 1))
    rows_spec = pl.BlockSpec((None, rt * CHUNK, tw), lambda j, i: (j, i, 0))
    return pl.pallas_call(
        functools.partial(_s5_scan_kernel, tiles_per_seq=rows_seq // rt),
        grid=(nblk, tok // (rt * CHUNK)),
        in_specs=[rows_spec, per_tile(bc), per_tile(cc), per_tile(klag), per_tile(pw)],
        out_specs=rows_spec,
        out_shape=jax.ShapeDtypeStruct(u.shape, BF16),
        scratch_shapes=[pltpu.VMEM((rt, 2 * ns), F32), pltpu.VMEM((8, 2 * ns), F32)],
        compiler_params=_params("parallel", "arbitrary"),
        name="s5_scan",
    )(u, bc, cc, klag, pw)


def _s5_tail(y, gs, w_glu, b_glu, w_proj):
    nblk, tok, tw = y.shape
    dm = gs.shape[1]
    tm = _tile(tok, S5_TAIL_TOKEN_TILE)
    tok_spec = lambda w: pl.BlockSpec((tm, w), lambda i: (i, 0))
    return pl.pallas_call(
        _s5_tail_kernel,
        grid=(tok // tm,),
        in_specs=[pl.BlockSpec((nblk, tm, tw), lambda i: (0, i, 0)), tok_spec(dm), _resident(w_glu.shape),
                  _resident(b_glu.shape), _resident(w_proj.shape)],
        out_specs=tok_spec(dm),
        out_shape=jax.ShapeDtypeStruct((tok, dm), BF16),
        compiler_params=_params("parallel"),
        name="s5_tail",
    )(y, gs, w_glu, b_glu, w_proj)


def _hgrn_merge(q, k, v, og, lf, gh, ms, x2, w_proj, w_out, norm_g, ffn_g, *, seq):
    tok, wh = q.shape
    dm = x2.shape[1]
    heads = wh // HEAD_DIM
    tl = _tile(seq, HGRN_TOKEN_TILE)
    lt = seq // tl
    seq_spec = lambda w: pl.BlockSpec((tl, w), lambda b, l: (b * lt + l, 0))
    return pl.pallas_call(
        functools.partial(_hgrn_kernel, heads=heads),
        grid=(tok // seq, lt),
        in_specs=[seq_spec(wh)] * 5 + [seq_spec(dm)] * 3
                 + [_resident(w_proj.shape), _resident(w_out.shape), _resident((1, wh)), _resident((1, dm))],
        out_specs=[seq_spec(dm)] * 2,
        out_shape=[jax.ShapeDtypeStruct((tok, dm), F32), jax.ShapeDtypeStruct((tok, dm), BF16)],
        scratch_shapes=[pltpu.VMEM((heads, HEAD_DIM, HEAD_DIM), F32),
                        pltpu.VMEM((tl, wh), BF16), pltpu.VMEM((tl, wh), BF16),
                        pltpu.VMEM((tl, wh), F32), pltpu.VMEM((tl, wh), F32), pltpu.VMEM((tl, wh), F32)],
        compiler_params=_params("arbitrary", "arbitrary"),
        name="hgrn_merge",
    )(q, k, v, og, lf, gh, ms, x2, w_proj, w_out, norm_g, ffn_g)


def _ffn(h2, x1, w_up, cw, cb, w_down, final_g, *, seq):
    tok, dm = h2.shape
    dff = w_down.shape[0]
    tf = _tile(dff, FFN_FF_TILE)
    nf = dff // tf
    tmf = _tile(seq, FFN_TOKEN_TILE)
    halo = 16
    return pl.pallas_call(
        functools.partial(_ffn_kernel, tiles_per_seq=seq // tmf),
        grid=(tok // tmf, nf),
        in_specs=[pl.BlockSpec((tmf, dm), lambda i, j: (i, 0)),
                  pl.BlockSpec((halo, dm), lambda i, j: (jnp.maximum(i * (tmf // halo) - 1, 0), 0)),
                  pl.BlockSpec((dm, tf), lambda i, j: (0, j)),
                  pl.BlockSpec((dm, tf), lambda i, j: (0, nf + j)),
                  pl.BlockSpec((CONV_WIDTH, tf), lambda i, j: (0, j)),
                  pl.BlockSpec((CONV_WIDTH, tf), lambda i, j: (0, nf + j)),
                  pl.BlockSpec((1, tf), lambda i, j: (0, j)),
                  pl.BlockSpec((1, tf), lambda i, j: (0, nf + j)),
                  pl.BlockSpec((tf, dm), lambda i, j: (j, 0)),
                  pl.BlockSpec((tmf, dm), lambda i, j: (i, 0), pipeline_mode=pl.Buffered(1)),
                  pl.BlockSpec((1, dm), lambda i, j: (0, 0))],
        out_specs=pl.BlockSpec((tmf, dm), lambda i, j: (i, 0), pipeline_mode=pl.Buffered(1)),
        out_shape=jax.ShapeDtypeStruct((tok, dm), F32),
        scratch_shapes=[pltpu.VMEM((halo + tmf, dm), BF16)],
        compiler_params=pltpu.CompilerParams(dimension_semantics=("parallel", "arbitrary"),
                                             vmem_limit_bytes=FFN_VMEM_LIMIT_BYTES),
        name="ffn",
    )(h2, h2, w_up, w_up, cw, cw, cb, cb, w_down, x1, final_g)


def kernel(x, ln_mix_g, w_in, s5_a_re, s5_a_im, s5_log_dt, s5_b_re, s5_b_im, s5_c_re, s5_c_im,
           s5_d, s5_w_glu, s5_b_glu, w_proj_s5, hgrn_lb_logits, hgrn_norm_g, w_proj_hgrn,
           w_out, ln_ffn_g, w_up, conv_w, conv_b, w_down, ln_final_g):
    bsz, seq, dm = x.shape
    assert ln_mix_g.shape[0] == 1, "single-layer block"
    tok = bsz * seq
    ws5 = s5_w_glu.shape[1]
    wh = hgrn_norm_g.shape[1]
    assert ws5 == wh and ws5 % MXU_DIM_V7X == 0 and seq % CHUNK == 0
    assert w_in.shape[2] == ws5 + 4 * wh + 2 * dm

    x2 = x.reshape(tok, dm)
    later_weights = [w.astype(F32) for w in (s5_w_glu[0], w_proj_s5[0], w_proj_hgrn[0], w_out[0], w_up[0], w_down[0])]
    u, q, k, v, og, lf, gs, gh, (w_glu_b, w_ps5_b, w_ph_b, w_out_b, w_up_b, w_down_b) = _inproj(
        x2, _row(ln_mix_g[0]), w_in[0].astype(BF16), hgrn_lb_logits.astype(F32), later_weights, wh=wh)

    bc, cc, klag, pw = _s5_prep(s5_a_re[0], s5_a_im[0], s5_log_dt[0], s5_b_re[0], s5_b_im[0],
                                s5_c_re[0], s5_c_im[0], s5_d[0])
    y = _s5_scan(u, bc, cc, klag, pw, seq=seq)
    ms = _s5_tail(y, gs, w_glu_b, _row(s5_b_glu[0]), w_ps5_b)

    x1, h2 = _hgrn_merge(q, k, v, og, lf, gh, ms, x2, w_ph_b, w_out_b,
                         _row(hgrn_norm_g[0]), _row(ln_ffn_g[0]), seq=seq)

    out = _ffn(h2, x1, w_up_b, conv_w[0].astype(F32), conv_b[0].astype(F32).reshape(1, -1),
               w_down_b, _row(ln_final_g), seq=seq)
    return out.reshape(bsz, seq, dm)
```

```python
import functools
import math

import jax
import jax.numpy as jnp
from jax import lax
from jax.experimental import pallas as pl
from jax.experimental.pallas import tpu as pltpu

F32 = jnp.float32
BF16 = jnp.bfloat16

RMS_EPS = 1e-6
S5_MAX_RE = -1e-4
S5_GROUP = 16
S5_STATE = 64
HEAD_DIM = 128
CONV_WIDTH = 3

MXU_DIM_V7X = 256
CHUNK = 16
S5_SUB = 4
S5_BLOCK_GROUPS = MXU_DIM_V7X // S5_GROUP
S5_BLOCK_STATE = S5_BLOCK_GROUPS * S5_STATE
VMEM_LIMIT_BYTES = 56 * 1024 * 1024
FFN_VMEM_LIMIT_BYTES = 62 * 1024 * 1024
HGRN_FACTOR_LIMIT = 60.0
HGRN_WIDE_CHUNK = 64

TOKEN_TILE = 512
S5_TAIL_TOKEN_TILE = 1024
S5_ROW_TILE = 256
HGRN_TOKEN_TILE = 256
FFN_TOKEN_TILE = 1024
FFN_FF_TILE = 512


def _sigmoid(y):
    return 0.5 + 0.5 * jnp.tanh(0.5 * y)


def _rmsnorm(x, g):
    return x * lax.rsqrt(jnp.mean(x * x, axis=-1, keepdims=True) + RMS_EPS) * g


def _params(*sem):
    return pltpu.CompilerParams(dimension_semantics=sem, vmem_limit_bytes=VMEM_LIMIT_BYTES)


def _resident(shape):
    return pl.BlockSpec(shape, lambda *_: (0,) * len(shape), pipeline_mode=pl.Buffered(1))


def _inproj_a_kernel(x_ref, g_ref, w_ref, lbl_ref, u_ref, q_ref, k_ref, v_ref, og_ref, lf_ref, *, width):
    h = _rmsnorm(x_ref[...], g_ref[...]).astype(BF16)

    def proj(i):
        return jnp.dot(h, w_ref[:, i * width:(i + 1) * width], preferred_element_type=F32)

    a = proj(1)
    q_ref[...] = (a * _sigmoid(a)).astype(BF16)
    lbl = lbl_ref[...]
    e = jnp.exp(lbl - jnp.max(lbl, axis=0, keepdims=True))
    lb = e[0:1, :] / jnp.sum(e, axis=0, keepdims=True)
    sg = _sigmoid(proj(2))
    lf_ref[...] = jnp.log(lb + (1.0 - lb) * sg)
    k_ref[...] = ((1.0 - lb) * (1.0 - sg)).astype(BF16)
    a = proj(4)
    og_ref[...] = (a * _sigmoid(a)).astype(BF16)
    v_ref[...] = proj(3).astype(BF16)
    a = proj(0).astype(BF16)
    for j in range(u_ref.shape[0]):
        u_ref[j] = a[:, j * MXU_DIM_V7X:(j + 1) * MXU_DIM_V7X]


def _inproj_b_kernel(x_ref, g_ref, *refs, n_w, n_cast):
    w_refs, src_refs = refs[:n_w], refs[n_w:n_w + n_cast]
    o_refs, dst_refs = refs[n_w + n_cast:n_w + n_cast + 2], refs[n_w + n_cast + 2:]
    for src, dst in zip(src_refs, dst_refs):
        dst[...] = src[...].astype(BF16)
    per_out = len(w_refs) // len(o_refs)
    h = _rmsnorm(x_ref[...], g_ref[...]).astype(BF16)
    for n, w_ref in enumerate(w_refs):
        cols = w_ref.shape[1]
        lo = (n % per_out) * cols
        o_refs[n // per_out][:, lo:lo + cols] = _sigmoid(
            jnp.dot(h, w_ref[...], preferred_element_type=F32)).astype(BF16)


def _spread(x, reps):
    w = x.shape[1]
    r = lax.broadcasted_iota(jnp.int32, (w, reps * w), 0)
    c = lax.broadcasted_iota(jnp.int32, (w, reps * w), 1)
    sel = jnp.where(c % w == r, 1.0, 0.0).astype(BF16)
    out = jnp.zeros((x.shape[0], reps * w), F32)
    for piece in _split3(x):
        out = out + jnp.dot(piece, sel, preferred_element_type=F32)
    return out


def _own_group(x, row_div, lane_div):
    r = lax.broadcasted_iota(jnp.int32, x.shape, 0)
    c = lax.broadcasted_iota(jnp.int32, x.shape, 1)
    return jnp.where(r // row_div == c // lane_div, x, 0.0)


def _s5_prep_kernel(bre_ref, bim_ref, cre_ref, cim_ref, are_ref, aim_ref, ldt_ref, d_ref,
                    bc_ref, cc_ref, klag_ref, pw_ref):
    ns = S5_BLOCK_STATE
    gb, p, c = S5_BLOCK_GROUPS, S5_STATE, S5_GROUP
    b_tile = lambda ref: _own_group(_spread(ref[0], gb), p, c).T
    c_tile = lambda ref: _own_group(_spread(ref[0], gb), c, p).T
    per_lane = lambda x, w: jnp.sum(_own_group(x, 1, w), axis=0, keepdims=True)
    lam_re = jnp.minimum(per_lane(_spread(are_ref[0], gb), p), S5_MAX_RE)
    lam_im = per_lane(_spread(aim_ref[0], gb), p)
    dt = jnp.exp(per_lane(jnp.broadcast_to(ldt_ref[0], (gb, ns)), p))
    d_row = per_lane(_spread(d_ref[0], gb), c)
    mag = jnp.exp(lam_re * dt)
    ab_re = mag * jnp.cos(lam_im * dt)
    ab_im = mag * jnp.sin(lam_im * dt)
    den = lam_re * lam_re + lam_im * lam_im
    nr = ab_re - 1.0
    ni = ab_im
    cf_re = (nr * lam_re + ni * lam_im) / den
    cf_im = (ni * lam_re - nr * lam_im) / den
    bre = b_tile(bre_ref)
    bim = b_tile(bim_ref)
    b_re = cf_re * bre - cf_im * bim
    b_im = cf_re * bim + cf_im * bre
    cre = c_tile(cre_ref).astype(BF16)
    cim = c_tile(cim_ref).astype(BF16)
    cc_ref[0] = jnp.concatenate([cre, -cim], axis=0)

    p_re = jnp.ones_like(ab_re)
    p_im = jnp.zeros_like(ab_re)
    pw_ref[...] = jnp.zeros_like(pw_ref)
    for n in range(CHUNK + 1):
        pw_ref[0, n:n + 1, :] = jnp.concatenate([p_re, p_im], axis=-1)
        if n < S5_SUB:
            bc_ref[0, S5_SUB - 1 - n] = jnp.concatenate(
                [b_re * p_re - b_im * p_im, b_re * p_im + b_im * p_re], axis=-1).astype(BF16)
        if n < CHUNK:
            t_re = (b_re * p_re - b_im * p_im).astype(BF16)
            t_im = (b_re * p_im + b_im * p_re).astype(BF16)
            tap = (jnp.dot(t_re, cre, preferred_element_type=F32)
                   - jnp.dot(t_im, cim, preferred_element_type=F32))
            if n == 0:
                r = lax.broadcasted_iota(jnp.int32, tap.shape, 0)
                c = lax.broadcasted_iota(jnp.int32, tap.shape, 1)
                tap = tap + jnp.where(r == c, d_row, 0.0)
            klag_ref[0, n] = tap.astype(BF16)
        p_re, p_im = p_re * ab_re - p_im * ab_im, p_re * ab_im + p_im * ab_re


def _s5_scan_kernel(u_ref, bc_ref, cc_ref, klag_ref, pw_ref, y_ref, z_ref, carry_ref, *, tiles_per_seq):
    ns = S5_BLOCK_STATE
    tw = MXU_DIM_V7X
    rt = u_ref.shape[0] // CHUNK
    by_pos = jnp.swapaxes(u_ref[...].reshape(rt, CHUNK, tw), 0, 1)

    def u_tile(s):
        return by_pos[s]

    def pw(n):
        return pw_ref[n:n + 1, 0:ns], pw_ref[n:n + 1, ns:2 * ns]

    @pl.when(pl.program_id(1) % tiles_per_seq == 0)
    def _():
        carry_ref[...] = jnp.zeros_like(carry_ref)

    acc_re = jnp.zeros((rt, ns), F32)
    acc_im = jnp.zeros((rt, ns), F32)
    for sub in range(CHUNK // S5_SUB):
        bb = jnp.dot(u_tile(sub * S5_SUB), bc_ref[0], preferred_element_type=F32)
        for r in range(1, S5_SUB):
            bb = bb + jnp.dot(u_tile(sub * S5_SUB + r), bc_ref[r], preferred_element_type=F32)
        p_re, p_im = pw(CHUNK - S5_SUB * (sub + 1))
        acc_re = acc_re + bb[:, :ns] * p_re - bb[:, ns:] * p_im
        acc_im = acc_im + bb[:, :ns] * p_im + bb[:, ns:] * p_re
    z_ref[:, 0:ns] = acc_re
    z_ref[:, ns:2 * ns] = acc_im

    a_re, a_im = pw(CHUNK)

    def body(r, carry):
        s_re, s_im = carry
        z_re = z_ref[pl.ds(r, 1), 0:ns]
        z_im = z_ref[pl.ds(r, 1), ns:2 * ns]
        z_ref[pl.ds(r, 1), 0:ns] = s_re
        z_ref[pl.ds(r, 1), ns:2 * ns] = s_im
        return (a_re * s_re - a_im * s_im + z_re, a_re * s_im + a_im * s_re + z_im)

    s_re, s_im = lax.fori_loop(0, rt, body, (carry_ref[0:1, 0:ns], carry_ref[0:1, ns:2 * ns]))
    carry_ref[0:1, 0:ns] = s_re
    carry_ref[0:1, ns:2 * ns] = s_im

    s_re = z_ref[:, 0:ns]
    s_im = z_ref[:, ns:2 * ns]
    ys = []
    for t in range(CHUNK):
        y = jnp.dot(u_tile(t), klag_ref[0], preferred_element_type=F32)
        for s in range(t):
            y = y + jnp.dot(u_tile(s), klag_ref[t - s], preferred_element_type=F32)
        p_re, p_im = pw(t + 1)
        rot = jnp.concatenate([s_re * p_re - s_im * p_im, s_re * p_im + s_im * p_re], axis=-1)
        y = y + jnp.dot(rot.astype(BF16), cc_ref[...], preferred_element_type=F32)
        ys.append(y.astype(BF16))
    y_ref[...] = jnp.swapaxes(jnp.stack(ys, axis=0), 0, 1).reshape(rt * CHUNK, tw)


def _s5_tail_kernel(y_ref, gs_ref, wg_ref, bg_ref, wp_ref, o_ref):
    y = jnp.concatenate([y_ref[j] for j in range(y_ref.shape[0])], axis=-1).astype(F32)
    z = 0.5 * y * (1.0 + jnp.tanh(math.sqrt(2.0 / math.pi) * (y + 0.044715 * (y * y * y))))
    gl = jnp.dot(z.astype(BF16), wg_ref[...], preferred_element_type=F32) + bg_ref[...]
    zz = (z * _sigmoid(gl)).astype(BF16)
    ys = jnp.dot(zz, wp_ref[...], preferred_element_type=F32)
    o_ref[...] = (gs_ref[...].astype(F32) * ys).astype(BF16)


def _split3(x):
    hi = x.astype(BF16)
    r = x - hi.astype(F32)
    mid = r.astype(BF16)
    lo = (r - mid.astype(F32)).astype(BF16)
    return hi, mid, lo


def _hgrn_kernel(q_ref, k_ref, v_ref, og_ref, lf_ref, gh_ref, ms_ref, x_ref,
                 wph_ref, wout_ref, ng_ref, g2_ref, x1_ref, h2_ref,
                 st_ref, qt_ref, kh_ref, b_ref, gt_ref, o_ref, *, heads):
    tl = q_ref.shape[0]
    hd = HEAD_DIM

    @pl.when(pl.program_id(1) == 0)
    def _():
        st_ref[...] = jnp.zeros_like(st_ref)

    r = lax.broadcasted_iota(jnp.int32, (tl, tl), 0)
    c = lax.broadcasted_iota(jnp.int32, (tl, tl), 1)
    pieces = _split3(lf_ref[...])

    def chunk_sums(chunk):
        causal = jnp.logical_and((r // chunk) == (c // chunk), c <= r)
        tri = jnp.where(causal, 1.0, 0.0).astype(BF16)
        b = jnp.zeros((tl, heads * hd), F32)
        for piece in pieces:
            b = b + jnp.dot(tri, piece, preferred_element_type=F32)
        last = b.reshape(tl // chunk, chunk, heads * hd)[:, chunk - 1:chunk, :]
        bt = jnp.broadcast_to(last, (tl // chunk, chunk, heads * hd)).reshape(tl, heads * hd)
        return causal, b, bt

    def store_decayed(b, bt):
        b_ref[...] = b
        gt_ref[...] = jnp.exp(bt)
        qt_ref[...] = (q_ref[...].astype(F32) * jnp.exp(b)).astype(BF16)
        kh_ref[...] = (k_ref[...].astype(F32) * jnp.exp(bt - b)).astype(BF16)

    half = CHUNK // 2
    row = lax.broadcasted_iota(jnp.int32, (half, 1), 0)

    def pairs_exact(rows, lanes):
        q16 = q_ref[rows, lanes].astype(F32)
        k16 = k_ref[rows, lanes].astype(F32)
        v16 = v_ref[rows, lanes].astype(F32)
        b16 = b_ref[rows, lanes]
        q_top, q_bot = q16[:half], q16[half:]
        b_top, b_bot = b16[:half], b16[half:]
        acc_top = jnp.zeros((half, hd), F32)
        acc_bot = jnp.zeros((half, hd), F32)
        for s in range(CHUNK):
            bs = b16[s:s + 1]
            ks = k16[s:s + 1]
            vs = v16[s:s + 1]
            if s < half:
                w = jnp.sum(q_top * ks * jnp.exp(jnp.minimum(b_top - bs, 0.0)), axis=-1, keepdims=True)
                acc_top = acc_top + jnp.where(row >= s, w, 0.0) * vs
                w = jnp.sum(q_bot * ks * jnp.exp(b_bot - bs), axis=-1, keepdims=True)
                acc_bot = acc_bot + w * vs
            else:
                w = jnp.sum(q_bot * ks * jnp.exp(jnp.minimum(b_bot - bs, 0.0)), axis=-1, keepdims=True)
                acc_bot = acc_bot + jnp.where(row + half >= s, w, 0.0) * vs
        return jnp.concatenate([acc_top, acc_bot], axis=0)

    def carry_state(rows, first_row, h):
        lanes = slice(h * hd, (h + 1) * hd)
        st = st_ref[h]
        inter = lax.dot_general(qt_ref[rows, lanes], st.astype(BF16),
                                (((1,), (1,)), ((), ())), preferred_element_type=F32)
        upd = lax.dot_general(v_ref[rows, lanes], kh_ref[rows, lanes],
                              (((0,), (0,)), ((), ())), preferred_element_type=F32)
        st_ref[h] = gt_ref[first_row, lanes] * st + upd
        return inter

    def exact_step(n, _):
        r0 = pl.multiple_of(n * CHUNK, CHUNK)
        rows = pl.ds(r0, CHUNK)
        for h in range(heads):
            lanes = slice(h * hd, (h + 1) * hd)
            o_ref[rows, lanes] = carry_state(rows, pl.ds(r0, 1), h) + pairs_exact(rows, lanes)
        return 0

    causal, b, bt = chunk_sums(HGRN_WIDE_CHUNK)
    worst = jnp.max(jnp.max(-bt, axis=0, keepdims=True), axis=1, keepdims=True)[0, 0]
    factorable = worst <= HGRN_FACTOR_LIMIT

    @pl.when(factorable)
    def _():
        store_decayed(b, bt)
        kx = (k_ref[...].astype(F32) * jnp.exp(-b)).astype(BF16)
        for h in range(heads):
            lanes = slice(h * hd, (h + 1) * hd)
            sc = lax.dot_general(qt_ref[:, lanes], kx[:, lanes], (((1,), (1,)), ((), ())),
                                 preferred_element_type=F32)
            o_ref[:, lanes] = jnp.dot(jnp.where(causal, sc, 0.0).astype(BF16), v_ref[:, lanes],
                                      preferred_element_type=F32)
        for n in range(tl // HGRN_WIDE_CHUNK):
            rows = slice(n * HGRN_WIDE_CHUNK, (n + 1) * HGRN_WIDE_CHUNK)
            for h in range(heads):
                lanes = slice(h * hd, (h + 1) * hd)
                o_ref[rows, lanes] += carry_state(rows, slice(rows.start, rows.start + 1), h)

    @pl.when(jnp.logical_not(factorable))
    def _():
        store_decayed(*chunk_sums(CHUNK)[1:])
        lax.fori_loop(0, tl // CHUNK, exact_step, 0)

    parts = []
    for h in range(heads):
        lanes = slice(h * hd, (h + 1) * hd)
        o = o_ref[:, lanes]
        o = o * lax.rsqrt(jnp.mean(o * o, axis=-1, keepdims=True) + RMS_EPS)
        parts.append((o * ng_ref[:, lanes] * og_ref[:, lanes].astype(F32)).astype(BF16))
    on = jnp.concatenate(parts, axis=-1)
    yh = jnp.dot(on, wph_ref[...], preferred_element_type=F32)
    merged = ms_ref[...].astype(F32) + gh_ref[...].astype(F32) * yh
    x1 = x_ref[...] + jnp.dot(merged.astype(BF16), wout_ref[...], preferred_element_type=F32)
    x1_ref[...] = x1
    h2_ref[...] = _rmsnorm(x1, g2_ref[...]).astype(BF16)


def _ffn_kernel(h_ref, halo_ref, wug_ref, wuv_ref, cwg_ref, cwv_ref, cbg_ref, cbv_ref, wd_ref,
                x1_ref, gf_ref, o_ref, hcat_ref, *, tiles_per_seq):
    i = pl.program_id(0)
    j = pl.program_id(1)
    tm = h_ref.shape[0]
    nh = halo_ref.shape[0]

    @pl.when(j == 0)
    def _():
        keep = jnp.where(i % tiles_per_seq == 0, 0.0, 1.0).astype(BF16)
        hcat_ref[0:nh] = halo_ref[...] * keep
        hcat_ref[nh:nh + tm] = h_ref[...]
        o_ref[...] = x1_ref[...]

    def conv_up(w_ref, cw_ref, cb_ref):
        ext = jnp.dot(hcat_ref[...], w_ref[...], preferred_element_type=F32)
        out = cb_ref[...]
        for tap in range(CONV_WIDTH):
            back = CONV_WIDTH - 1 - tap
            out = out + cw_ref[tap:tap + 1, :] * ext[nh - back:nh - back + tm]
        return out

    gate = conv_up(wug_ref, cwg_ref, cbg_ref)
    val = conv_up(wuv_ref, cwv_ref, cbv_ref)
    act = (0.5 * gate * (1.0 + jnp.tanh(0.5 * gate)) * val).astype(BF16)
    o_ref[...] += jnp.dot(act, wd_ref[...], preferred_element_type=F32)

    @pl.when(j == pl.num_programs(1) - 1)
    def _():
        o_ref[...] = _rmsnorm(o_ref[...], gf_ref[...])


def _tile(n, want):
    t = min(n, want)
    assert n % t == 0, (n, want)
    return t


def _row(v):
    return v.reshape(1, -1).astype(F32)


def _inproj(x2, g, w_in, lb_logits, to_cast, *, wh):
    tok, dm = x2.shape
    tm = _tile(tok, TOKEN_TILE)
    nblk = wh // MXU_DIM_V7X
    n_a = 5 * wh
    assert w_in.shape[1] == n_a + 2 * dm and dm % wh == 0
    tok_spec = lambda w: pl.BlockSpec((tm, w), lambda i: (i, 0))
    cols = lambda n, width: pl.BlockSpec((dm, width), lambda i: (0, n), pipeline_mode=pl.Buffered(1))
    gate_blocks = 2 * dm // wh
    steps = tok // tm
    bf16_rows = 16
    assert all(w.shape[0] % (steps * bf16_rows) == 0 for w in [w_in] + to_cast)
    row_block = lambda rows, width: pl.BlockSpec((rows // steps, width), lambda i: (i, 0))
    row_blocks = [row_block(dm, n_a)] + [row_block(*w.shape) for w in to_cast]
    gs, gh, w_a, *casted = pl.pallas_call(
        functools.partial(_inproj_b_kernel, n_w=gate_blocks, n_cast=1 + len(to_cast)),
        grid=(steps,),
        in_specs=[tok_spec(dm), _resident((1, dm))] + [cols(n, wh) for n in range(gate_blocks)] + row_blocks,
        out_specs=[tok_spec(dm)] * 2 + row_blocks,
        out_shape=[jax.ShapeDtypeStruct((tok, dm), BF16)] * 2 + [jax.ShapeDtypeStruct((dm, n_a), BF16)]
                  + [jax.ShapeDtypeStruct(w.shape, BF16) for w in to_cast],
        compiler_params=_params("parallel"),
        name="inproj_b",
    )(x2, g, *([w_in[:, n_a:].astype(BF16)] * gate_blocks), w_in, *to_cast)
    u, q, k, v, og, lf = pl.pallas_call(
        functools.partial(_inproj_a_kernel, width=wh),
        grid=(tok // tm,),
        in_specs=[tok_spec(dm), _resident((1, dm)), _resident((dm, n_a)), _resident(lb_logits.shape)],
        out_specs=[pl.BlockSpec((nblk, tm, MXU_DIM_V7X), lambda i: (0, i, 0))] + [tok_spec(wh)] * 5,
        out_shape=[jax.ShapeDtypeStruct((nblk, tok, MXU_DIM_V7X), BF16)]
                  + [jax.ShapeDtypeStruct((tok, wh), BF16)] * 4 + [jax.ShapeDtypeStruct((tok, wh), F32)],
        compiler_params=_params("parallel"),
        name="inproj_a",
    )(x2, g, w_a, lb_logits)
    return u, q, k, v, og, lf, gs, gh, casted


def _s5_prep(a_re, a_im, log_dt, b_re, b_im, c_re, c_im, d):
    groups = a_re.shape[0]
    nblk = groups // S5_BLOCK_GROUPS
    ns = S5_BLOCK_STATE
    tile_c = S5_BLOCK_GROUPS * S5_GROUP
    blk3 = lambda r, c: pl.BlockSpec((1, r, c), lambda j: (j, 0, 0))
    pw_rows = 24
    gb, p, c = S5_BLOCK_GROUPS, S5_STATE, S5_GROUP
    tiled = lambda t, r, w: t.astype(F32).reshape(nblk, r, w)
    return pl.pallas_call(
        _s5_prep_kernel,
        grid=(nblk,),
        in_specs=[blk3(gb * p, c), blk3(gb * p, c), blk3(gb * c, p), blk3(gb * c, p),
                  blk3(gb, p), blk3(gb, p), blk3(gb, 1), blk3(gb, c)],
        out_specs=[pl.BlockSpec((1, S5_SUB, tile_c, 2 * ns), lambda j: (j, 0, 0, 0)), blk3(2 * ns, tile_c),
                   pl.BlockSpec((1, CHUNK, tile_c, tile_c), lambda j: (j, 0, 0, 0)), blk3(pw_rows, 2 * ns)],
        out_shape=[jax.ShapeDtypeStruct((nblk, S5_SUB, tile_c, 2 * ns), BF16),
                   jax.ShapeDtypeStruct((nblk, 2 * ns, tile_c), BF16),
                   jax.ShapeDtypeStruct((nblk, CHUNK, tile_c, tile_c), BF16),
                   jax.ShapeDtypeStruct((nblk, pw_rows, 2 * ns), F32)],
        compiler_params=_params("parallel"),
        name="s5_prep",
    )(tiled(b_re, gb * p, c), tiled(b_im, gb * p, c), tiled(c_re, gb * c, p), tiled(c_im, gb * c, p),
      tiled(a_re, gb, p), tiled(a_im, gb, p), tiled(log_dt, gb, 1), tiled(d, gb, c))


def _s5_scan(u, bc, cc, klag, pw, *, seq):
    nblk, tok, tw = u.shape
    ns = S5_BLOCK_STATE
    rows_seq = seq // CHUNK
    rt = _tile(rows_seq, S5_ROW_TILE)
    per_tile = lambda a: pl.BlockSpec((None,) + a.shape[1:], lambda j, i: (j,) + (0,) * (a.ndim - 1))
    rows_spec = pl.BlockSpec((None, rt * CHUNK, tw), lambda j, i: (j, i, 0))
    return pl.pallas_call(
        functools.partial(_s5_scan_kernel, tiles_per_seq=rows_seq // rt),
        grid=(nblk, tok // (rt * CHUNK)),
        in_specs=[rows_spec, per_tile(bc), per_tile(cc), per_tile(klag), per_tile(pw)],
        out_specs=rows_spec,
        out_shape=jax.ShapeDtypeStruct(u.shape, BF16),
        scratch_shapes=[pltpu.VMEM((rt, 2 * ns), F32), pltpu.VMEM((8, 2 * ns), F32)],
        compiler_params=_params("parallel", "arbitrary"),
        name="s5_scan",
    )(u, bc, cc, klag, pw)


def _s5_tail(y, gs, w_glu, b_glu, w_proj):
    nblk, tok, tw = y.shape
    dm = gs.shape[1]
    tm = _tile(tok, S5_TAIL_TOKEN_TILE)
    tok_spec = lambda w: pl.BlockSpec((tm, w), lambda i: (i, 0))
    return pl.pallas_call(
        _s5_tail_kernel,
        grid=(tok // tm,),
        in_specs=[pl.BlockSpec((nblk, tm, tw), lambda i: (0, i, 0)), tok_spec(dm), _resident(w_glu.shape),
                  _resident(b_glu.shape), _resident(w_proj.shape)],
        out_specs=tok_spec(dm),
        out_shape=jax.ShapeDtypeStruct((tok, dm), BF16),
        compiler_params=_params("parallel"),
        name="s5_tail",
    )(y, gs, w_glu, b_glu, w_proj)


def _hgrn_merge(q, k, v, og, lf, gh, ms, x2, w_proj, w_out, norm_g, ffn_g, *, seq):
    tok, wh = q.shape
    dm = x2.shape[1]
    heads = wh // HEAD_DIM
    tl = _tile(seq, HGRN_TOKEN_TILE)
    lt = seq // tl
    seq_spec = lambda w: pl.BlockSpec((tl, w), lambda b, l: (b * lt + l, 0))
    return pl.pallas_call(
        functools.partial(_hgrn_kernel, heads=heads),
        grid=(tok // seq, lt),
        in_specs=[seq_spec(wh)] * 5 + [seq_spec(dm)] * 3
                 + [_resident(w_proj.shape), _resident(w_out.shape), _resident((1, wh)), _resident((1, dm))],
        out_specs=[seq_spec(dm)] * 2,
        out_shape=[jax.ShapeDtypeStruct((tok, dm), F32), jax.ShapeDtypeStruct((tok, dm), BF16)],
        scratch_shapes=[pltpu.VMEM((heads, HEAD_DIM, HEAD_DIM), F32),
                        pltpu.VMEM((tl, wh), BF16), pltpu.VMEM((tl, wh), BF16),
                        pltpu.VMEM((tl, wh), F32), pltpu.VMEM((tl, wh), F32), pltpu.VMEM((tl, wh), F32)],
        compiler_params=_params("arbitrary", "arbitrary"),
        name="hgrn_merge",
    )(q, k, v, og, lf, gh, ms, x2, w_proj, w_out, norm_g, ffn_g)


def _ffn(h2, x1, w_up, cw, cb, w_down, final_g, *, seq):
    tok, dm = h2.shape
    dff = w_down.shape[0]
    tf = _tile(dff, FFN_FF_TILE)
    nf = dff // tf
    tmf = _tile(seq, FFN_TOKEN_TILE)
    halo = 16
    return pl.pallas_call(
        functools.partial(_ffn_kernel, tiles_per_seq=seq // tmf),
        grid=(tok // tmf, nf),
        in_specs=[pl.BlockSpec((tmf, dm), lambda i, j: (i, 0)),
                  pl.BlockSpec((halo, dm), lambda i, j: (jnp.maximum(i * (tmf // halo) - 1, 0), 0)),
                  pl.BlockSpec((dm, tf), lambda i, j: (0, j)),
                  pl.BlockSpec((dm, tf), lambda i, j: (0, nf + j)),
                  pl.BlockSpec((CONV_WIDTH, tf), lambda i, j: (0, j)),
                  pl.BlockSpec((CONV_WIDTH, tf), lambda i, j: (0, nf + j)),
                  pl.BlockSpec((1, tf), lambda i, j: (0, j)),
                  pl.BlockSpec((1, tf), lambda i, j: (0, nf + j)),
                  pl.BlockSpec((tf, dm), lambda i, j: (j, 0)),
                  pl.BlockSpec((tmf, dm), lambda i, j: (i, 0), pipeline_mode=pl.Buffered(1)),
                  pl.BlockSpec((1, dm), lambda i, j: (0, 0))],
        out_specs=pl.BlockSpec((tmf, dm), lambda i, j: (i, 0), pipeline_mode=pl.Buffered(1)),
        out_shape=jax.ShapeDtypeStruct((tok, dm), F32),
        scratch_shapes=[pltpu.VMEM((halo + tmf, dm), BF16)],
        compiler_params=pltpu.CompilerParams(dimension_semantics=("parallel", "arbitrary"),
                                             vmem_limit_bytes=FFN_VMEM_LIMIT_BYTES),
        name="ffn",
    )(h2, h2, w_up, w_up, cw, cw, cb, cb, w_down, x1, final_g)


def kernel(x, ln_mix_g, w_in, s5_a_re, s5_a_im, s5_log_dt, s5_b_re, s5_b_im, s5_c_re, s5_c_im,
           s5_d, s5_w_glu, s5_b_glu, w_proj_s5, hgrn_lb_logits, hgrn_norm_g, w_proj_hgrn,
           w_out, ln_ffn_g, w_up, conv_w, conv_b, w_down, ln_final_g):
    bsz, seq, dm = x.shape
    assert ln_mix_g.shape[0] == 1, "single-layer block"
    tok = bsz * seq
    ws5 = s5_w_glu.shape[1]
    wh = hgrn_norm_g.shape[1]
    assert ws5 == wh and ws5 % MXU_DIM_V7X == 0 and seq % CHUNK == 0
    assert w_in.shape[2] == ws5 + 4 * wh + 2 * dm

    x2 = x.reshape(tok, dm)
    later_weights = [w.astype(F32) for w in (s5_w_glu[0], w_proj_s5[0], w_proj_hgrn[0], w_out[0], w_up[0], w_down[0])]
    u, q, k, v, og, lf, gs, gh, (w_glu_b, w_ps5_b, w_ph_b, w_out_b, w_up_b, w_down_b) = _inproj(
        x2, _row(ln_mix_g[0]), w_in[0].astype(F32), hgrn_lb_logits.astype(F32), later_weights, wh=wh)

    bc, cc, klag, pw = _s5_prep(s5_a_re[0], s5_a_im[0], s5_log_dt[0], s5_b_re[0], s5_b_im[0],
                                s5_c_re[0], s5_c_im[0], s5_d[0])
    y = _s5_scan(u, bc, cc, klag, pw, seq=seq)
    ms = _s5_tail(y, gs, w_glu_b, _row(s5_b_glu[0]), w_ps5_b)

    x1, h2 = _hgrn_merge(q, k, v, og, lf, gh, ms, x2, w_ph_b, w_out_b,
                         _row(hgrn_norm_g[0]), _row(ln_ffn_g[0]), seq=seq)

    out = _ffn(h2, x1, w_up_b, conv_w[0].astype(F32), conv_b[0].astype(F32).reshape(1, -1),
               w_down_b, _row(ln_final_g), seq=seq)
    return out.reshape(bsz, seq, dm)
```
